```python
import math
import numpy as np
import jax
import jax.numpy as jnp
from jax import lax

D_MODEL = 2048
BATCH = 32
SEQ = 256
DEPTH = 2
DEC_BATCH = 2
DEC_SEQ = 1024
PAST_LEN = 256

GRID_W = 64
SSD_INNER = D_MODEL // 2
SSD_HEAD_DIM = 64
SSD_HEADS = SSD_INNER // SSD_HEAD_DIM
SSD_GROUPS = 2
SSD_STATE = 128
SSD_CONV = 3
SSD_XBC = SSD_INNER + 2 * SSD_GROUPS * SSD_STATE
SSD_CHUNK = 128
DIFF_HEAD_DIM = 64
DIFF_WIDTH = D_MODEL // 2
DIFF_HEADS = DIFF_WIDTH // (2 * DIFF_HEAD_DIM)
NA_HEAD_DIM = 64
NA_WIDTH = D_MODEL // 2
NA_HEADS = NA_WIDTH // NA_HEAD_DIM
NA_KH_MAX = 8
NA_KW = 16
NA_QB = 16
NA_KB = NA_QB + NA_KW
D_FF = 256 * ((8 * D_MODEL // 3 + 255) // 256)
FFN_CONV = 3
ROPE_BASE = 10000.0
Q_BLOCK = 128
EPS = 1e-6
IN_SIZES = (SSD_INNER, SSD_XBC, 2 * SSD_HEADS, DIFF_WIDTH, DIFF_WIDTH, DIFF_WIDTH, NA_WIDTH, NA_WIDTH, NA_WIDTH, 3 * D_MODEL)
IN_COLS = sum(IN_SIZES)

kernel_name = 'hybrid_ssd_diffattn_natten_flow_step'

F32 = jnp.float32


def rms_norm(x, w):
    xf = x.astype(F32)
    y = xf * lax.rsqrt(jnp.mean(xf * xf, axis=-1, keepdims=True) + EPS)
    return (y * w.astype(F32)).astype(x.dtype)


def dwconv_centred(x, w, bias):
    k = w.shape[0]
    y = lax.conv_general_dilated(x, w[:, None, :].astype(x.dtype), window_strides=(1,),
                                 padding=[(k // 2, k // 2)], dimension_numbers=('NWC', 'WIO', 'NWC'),
                                 feature_group_count=x.shape[-1])
    return y + bias.astype(x.dtype)


def _segsum_exp(a):
    t = a.shape[-1]
    acum = jnp.cumsum(a, axis=-1)
    diff = acum[..., :, None] - acum[..., None, :]
    mask = jnp.tril(jnp.ones((t, t), dtype=bool))
    return jnp.exp(jnp.where(mask, diff, -jnp.inf))


def ssd_scan(x, dt, a, bm, cm, h0):
    b, l, h, p = x.shape
    n = bm.shape[-1]
    nc = l // SSD_CHUNK
    xd = (x.astype(F32) * dt[..., None]).reshape(b, nc, SSD_CHUNK, h, p)
    bc = bm.astype(F32).reshape(b, nc, SSD_CHUNK, h, n)
    cc = cm.astype(F32).reshape(b, nc, SSD_CHUNK, h, n)
    la = (dt * a).reshape(b, nc, SSD_CHUNK, h).transpose(0, 3, 1, 2)
    acum = jnp.cumsum(la, axis=-1)
    lmat = _segsum_exp(la)
    gmat = jnp.einsum('bclhn,bcshn->bhcls', cc, bc) * lmat
    y_diag = jnp.einsum('bhcls,bcshp->bclhp', gmat, xd)
    decay_states = jnp.exp(acum[..., -1:] - acum)
    states = jnp.einsum('bclhn,bhcl,bclhp->bchpn', bc, decay_states, xd)
    states = jnp.concatenate([h0.astype(F32)[:, None], states], axis=1)
    chunk_a = jnp.pad(acum[..., -1], ((0, 0), (0, 0), (1, 0)))
    decay_chunk = _segsum_exp(chunk_a)
    new_states = jnp.einsum('bhzc,bchpn->bzhpn', decay_chunk, states)
    states_in, h_final = new_states[:, :-1], new_states[:, -1]
    y_off = jnp.einsum('bclhn,bchpn,bhcl->bclhp', cc, states_in, jnp.exp(acum))
    return (y_diag + y_off).reshape(b, l, h, p), h_final


def _rotate(x, ang):
    f = ang.shape[-1]
    shape = (ang.shape[0],) + (1,) * (x.ndim - 3) + (f,)
    cos = jnp.cos(ang).reshape(shape)
    sin = jnp.sin(ang).reshape(shape)
    x1, x2 = x[..., :f], x[..., f:]
    return jnp.concatenate([x1 * cos - x2 * sin, x1 * sin + x2 * cos], axis=-1)


def apply_axial_rope(x):
    l, dim = x.shape[1], x.shape[-1]
    t = jnp.arange(l)
    row = (t // GRID_W).astype(F32)
    col = (t % GRID_W).astype(F32)
    n_freq = dim // 4
    inv = ROPE_BASE ** (-jnp.arange(n_freq, dtype=F32) / n_freq)
    xf = x.astype(F32)
    half = dim // 2
    out = jnp.concatenate([_rotate(xf[..., :half], row[:, None] * inv),
                           _rotate(xf[..., half:], col[:, None] * inv)], axis=-1)
    return out.astype(x.dtype)


def diff_attention(q, k, v, lam, subln_w, lam_init):
    b, lq, hh, _, d = q.shape
    nb = lq // Q_BLOCK
    qb = q.reshape(b, nb, Q_BLOCK, hh, 2, d).transpose(1, 0, 2, 3, 4, 5)
    scale = d ** -0.5

    def block(qi):
        s = jnp.einsum('bqhtd,bkhtd->bhtqk', qi, k, preferred_element_type=F32) * scale
        p = jax.nn.softmax(s, axis=-1)
        att = p[:, :, 0] - lam * p[:, :, 1]
        return jnp.einsum('bhqk,bkhe->bqhe', att.astype(v.dtype), v)

    o = lax.map(block, qb).transpose(1, 0, 2, 3, 4).reshape(b, lq, hh, v.shape[-1])
    o = rms_norm(o, subln_w) * (1.0 - lam_init)
    return o.reshape(b, lq, hh * v.shape[-1])


def softmax_attention(q, k, v):
    b, lq, hh, d = q.shape
    nb = lq // Q_BLOCK
    qb = q.reshape(b, nb, Q_BLOCK, hh, d).transpose(1, 0, 2, 3, 4)
    scale = d ** -0.5

    def block(qi):
        s = jnp.einsum('bqhd,bkhd->bhqk', qi, k, preferred_element_type=F32) * scale
        p = jax.nn.softmax(s, axis=-1)
        return jnp.einsum('bhqk,bkhd->bqhd', p.astype(v.dtype), v)

    return lax.map(block, qb).transpose(1, 0, 2, 3, 4).reshape(b, lq, hh * d)


def neighborhood_attention(q, k, v, kc, vc, rpb):
    b, l, hh, d = q.shape
    rows = l // GRID_W
    kh = min(NA_KH_MAX, rows)
    ncb = GRID_W // NA_QB
    r = np.arange(rows)
    row_idx = np.clip(r - kh // 2, 0, rows - kh)[:, None] + np.arange(kh)[None, :]
    qcol = np.arange(ncb)[:, None] * NA_QB + np.arange(NA_QB)[None, :]
    band0 = np.clip(np.arange(ncb) * NA_QB - NA_KW // 2, 0, GRID_W - NA_KB)
    kcol = band0[:, None] + np.arange(NA_KB)[None, :]
    wstart = np.clip(qcol - NA_KW // 2, 0, GRID_W - NA_KW)
    valid = (kcol[:, None, :] >= wstart[:, :, None]) & (kcol[:, None, :] < wstart[:, :, None] + NA_KW)
    dr = row_idx - r[:, None] + NA_KH_MAX - 1
    dc = np.clip(kcol[:, None, :] - qcol[:, :, None] + NA_KW - 1, 0, 2 * NA_KW - 2)
    bias = rpb.astype(F32)[:, dr[:, None, None, :, None], dc[None, :, :, None, :]]
    bias = jnp.transpose(bias, (1, 2, 3, 0, 4, 5))
    qg = q.reshape(b, rows, ncb, NA_QB, hh, d)
    kg = k.reshape(b, rows, GRID_W, hh, d)[:, row_idx][:, :, :, kcol]
    vg = v.reshape(b, rows, GRID_W, hh, d)[:, row_idx][:, :, :, kcol]
    scale = d ** -0.5
    s_loc = jnp.einsum('brjuhd,brijkhd->brjuhik', qg, kg, preferred_element_type=F32) * scale + bias
    s_loc = jnp.where(valid[:, :, None, None, :], s_loc, -jnp.inf)
    s_ctx = jnp.einsum('brjuhd,bmhd->brjuhm', qg, kc, preferred_element_type=F32) * scale
    n_loc = kh * NA_KB
    s = jnp.concatenate([s_loc.reshape(s_loc.shape[:5] + (n_loc,)), s_ctx], axis=-1)
    p = jax.nn.softmax(s, axis=-1).astype(v.dtype)
    p_loc = p[..., :n_loc].reshape(s_loc.shape)
    o = (jnp.einsum('brjuhik,brijkhd->brjuhd', p_loc, vg)
         + jnp.einsum('brjuhm,bmhd->brjuhd', p[..., n_loc:], vc))
    return o.reshape(b, l, hh * d)


def mixer(h, P, layer, cache):
    b, l, _ = h.shape
    odt = h.dtype
    u = h @ P['w_in']
    splits = np.cumsum(IN_SIZES)[:-1].tolist()
    z, xbc, dt_raw, qd, kd, vd, qn, kn, vn, g = jnp.split(u, splits, axis=-1)

    xbc = jax.nn.silu(dwconv_centred(xbc, P['ssd_conv_w'], P['ssd_conv_b']))
    xs, bm, cm = jnp.split(xbc, [SSD_INNER, SSD_INNER + SSD_GROUPS * SSD_STATE], axis=-1)
    xs = xs.reshape(b, l, SSD_HEADS, SSD_HEAD_DIM)
    rep = SSD_HEADS // SSD_GROUPS
    bm = jnp.repeat(bm.reshape(b, l, SSD_GROUPS, SSD_STATE), rep, axis=2)
    cm = jnp.repeat(cm.reshape(b, l, SSD_GROUPS, SSD_STATE), rep, axis=2)
    dt = jax.nn.softplus(dt_raw.reshape(b, l, 2, SSD_HEADS).astype(F32) + P['ssd_dt_bias'].astype(F32))
    a = -jnp.exp(P['ssd_a_log'].astype(F32))
    if cache is None:
        h0 = jnp.zeros((b, 2, SSD_HEADS, SSD_HEAD_DIM, SSD_STATE), F32)
    else:
        h0 = cache[4]
    flip = lambda t: jnp.flip(t, axis=1)
    y_f, hf = ssd_scan(xs, dt[:, :, 0], a[0], bm, cm, h0[:, 0])
    y_r, hr = ssd_scan(flip(xs), flip(dt[:, :, 1]), a[1], flip(bm), flip(cm), h0[:, 1])
    y_a = y_f + flip(y_r) + P['ssd_d'].astype(F32)[:, None] * xs.astype(F32)
    y_a = y_a.reshape(b, l, SSD_INNER) * jax.nn.silu(z.astype(F32))
    y_a = rms_norm(y_a.reshape(b, l, SSD_GROUPS, SSD_INNER // SSD_GROUPS),
                   P['ssd_norm_w'].reshape(SSD_GROUPS, SSD_INNER // SSD_GROUPS)).reshape(b, l, SSD_INNER).astype(odt)
    ssm_state = jnp.stack([hf, hr], axis=1).astype(odt)

    qd = rms_norm(qd.reshape(b, l, DIFF_HEADS, 2, DIFF_HEAD_DIM), P['diff_q_norm'])
    kd = rms_norm(kd.reshape(b, l, DIFF_HEADS, 2, DIFF_HEAD_DIM), P['diff_k_norm'])
    vd = vd.reshape(b, l, DIFF_HEADS, 2 * DIFF_HEAD_DIM)
    lam_init = 0.8 - 0.6 * math.exp(-0.3 * layer)
    lp = P['diff_lam'].astype(F32)
    lam = jnp.exp(jnp.sum(lp[0] * lp[1])) - jnp.exp(jnp.sum(lp[2] * lp[3])) + lam_init
    if cache is None:
        y_b = diff_attention(qd, kd, vd, lam, P['diff_subln_w'], lam_init)
    else:
        keys = jnp.concatenate([apply_axial_rope(kd), cache[0]], axis=1)
        vals = jnp.concatenate([vd, cache[1]], axis=1)
        y_b = diff_attention(apply_axial_rope(qd), keys, vals, lam, P['diff_subln_w'], lam_init)

    qn = rms_norm(qn.reshape(b, l, NA_HEADS, NA_HEAD_DIM), P['na_q_norm'])
    kn = rms_norm(kn.reshape(b, l, NA_HEADS, NA_HEAD_DIM), P['na_k_norm'])
    vn = vn.reshape(b, l, NA_HEADS, NA_HEAD_DIM)
    if cache is None:
        y_c = softmax_attention(qn, kn, vn)
    else:
        y_c = neighborhood_attention(qn, kn, vn, cache[2], cache[3], P['na_rpb'])

    g_a, g_b, g_c = jnp.split(jax.nn.sigmoid(g), 3, axis=-1)
    merged = (g_a * (y_a @ P['w_branch_a']) + g_b * (y_b @ P['w_branch_b'])
              + g_c * (y_c @ P['w_branch_c']))
    return merged @ P['w_out'], (kd, vd, kn, vn, ssm_state)


def conv_ffn(h, P):
    u = dwconv_centred(h @ P['ffn_w_up'], P['ffn_conv_w'], P['ffn_conv_b'])
    val, gt = jnp.split(u, 2, axis=-1)
    return (jax.nn.silu(gt) * val) @ P['ffn_w_down']


def trunk_layer(x, mod, P, layer, cache):
    sh1, sc1, g1, sh2, sc2, g2 = jnp.split(mod[:, None, :].astype(x.dtype), 6, axis=-1)
    h = rms_norm(x, P['norm1_w']) * (1 + sc1) + sh1
    m, ctx = mixer(h, P, layer, cache)
    x = x + g1 * m
    h = rms_norm(x, P['norm2_w']) * (1 + sc2) + sh2
    x = x + g2 * conv_ffn(h, P)
    return x, ctx


def setup_inputs(seed: int = 0) -> dict:
    key = jax.random.key(seed)
    ks = iter(jax.random.split(key, 48))

    def nrm(shape, scale):
        return scale * jax.random.normal(next(ks), shape, F32)

    d = D_MODEL
    dt0 = jnp.exp(jax.random.uniform(next(ks), (DEPTH, 2, SSD_HEADS), F32, math.log(1e-3), math.log(1e-1)))
    return {
        'x_prompt': nrm((BATCH, SEQ, d), 1.0),
        'x_sample': nrm((DEC_BATCH, DEC_SEQ, d), 1.0),
        'c': nrm((DEC_BATCH, d), 1.0),
        'cache_diff_k': nrm((DEC_BATCH, DEPTH, PAST_LEN, DIFF_HEADS, 2, DIFF_HEAD_DIM), 1.0),
        'cache_diff_v': nrm((DEC_BATCH, DEPTH, PAST_LEN, DIFF_HEADS, 2 * DIFF_HEAD_DIM), 1.0),
        'cache_na_k': nrm((DEC_BATCH, DEPTH, PAST_LEN, NA_HEADS, NA_HEAD_DIM), 1.0),
        'cache_na_v': nrm((DEC_BATCH, DEPTH, PAST_LEN, NA_HEADS, NA_HEAD_DIM), 1.0),
        'state_ssm': nrm((DEC_BATCH, DEPTH, 2, SSD_HEADS, SSD_HEAD_DIM, SSD_STATE), 0.5),
        'c_ctx': nrm((d,), 1.0),
        'norm1_w': 1.0 + nrm((DEPTH, d), 0.02),
        'norm2_w': 1.0 + nrm((DEPTH, d), 0.02),
        'w_ada': nrm((DEPTH, d, 6 * d), 0.5 * d ** -0.5),
        'b_ada': nrm((DEPTH, 6 * d), 0.02),
        'w_in': nrm((DEPTH, d, IN_COLS), d ** -0.5),
        'ssd_conv_w': nrm((DEPTH, SSD_CONV, SSD_XBC), SSD_CONV ** -0.5),
        'ssd_conv_b': nrm((DEPTH, SSD_XBC), 0.02),
        'ssd_dt_bias': jnp.log(jnp.expm1(dt0)),
        'ssd_a_log': jnp.log(jax.random.uniform(next(ks), (DEPTH, 2, SSD_HEADS), F32, 1.0, 16.0)),
        'ssd_d': 1.0 + nrm((DEPTH, SSD_HEADS), 0.02),
        'ssd_norm_w': 1.0 + nrm((DEPTH, SSD_INNER), 0.02),
        'diff_q_norm': 1.0 + nrm((DEPTH, DIFF_HEAD_DIM), 0.02),
        'diff_k_norm': 1.0 + nrm((DEPTH, DIFF_HEAD_DIM), 0.02),
        'diff_lam': nrm((DEPTH, 4, DIFF_HEAD_DIM), 0.1),
        'diff_subln_w': 1.0 + nrm((DEPTH, 2 * DIFF_HEAD_DIM), 0.02),
        'na_q_norm': 1.0 + nrm((DEPTH, NA_HEAD_DIM), 0.02),
        'na_k_norm': 1.0 + nrm((DEPTH, NA_HEAD_DIM), 0.02),
        'na_rpb': nrm((DEPTH, NA_HEADS, 2 * NA_KH_MAX - 1, 2 * NA_KW - 1), 0.1),
        'w_branch_a': nrm((DEPTH, SSD_INNER, d), SSD_INNER ** -0.5),
        'w_branch_b': nrm((DEPTH, DIFF_WIDTH, d), DIFF_WIDTH ** -0.5),
        'w_branch_c': nrm((DEPTH, NA_WIDTH, d), NA_WIDTH ** -0.5),
        'w_out': nrm((DEPTH, d, d), d ** -0.5),
        'ffn_w_up': nrm((DEPTH, d, 2 * D_FF), d ** -0.5),
        'ffn_conv_w': nrm((DEPTH, FFN_CONV, 2 * D_FF), FFN_CONV ** -0.5),
        'ffn_conv_b': nrm((DEPTH, 2 * D_FF), 0.02),
        'ffn_w_down': nrm((DEPTH, D_FF, d), D_FF ** -0.5),
    }


def reference(x_prompt, x_sample, c, cache_diff_k, cache_diff_v, cache_na_k, cache_na_v, state_ssm, c_ctx,
              norm1_w, norm2_w, w_ada, b_ada, w_in, ssd_conv_w, ssd_conv_b, ssd_dt_bias, ssd_a_log, ssd_d,
              ssd_norm_w, diff_q_norm, diff_k_norm, diff_lam, diff_subln_w, na_q_norm, na_k_norm, na_rpb,
              w_branch_a, w_branch_b, w_branch_c, w_out, ffn_w_up, ffn_conv_w, ffn_conv_b, ffn_w_down):
    y_p = x_prompt
    y_s = x_sample
    new_dk, new_dv, new_nk, new_nv, new_s = [], [], [], [], []
    for l in range(DEPTH):
        P = dict(norm1_w=norm1_w[l], norm2_w=norm2_w[l], w_in=w_in[l], ssd_conv_w=ssd_conv_w[l],
                 ssd_conv_b=ssd_conv_b[l], ssd_dt_bias=ssd_dt_bias[l], ssd_a_log=ssd_a_log[l], ssd_d=ssd_d[l],
                 ssd_norm_w=ssd_norm_w[l], diff_q_norm=diff_q_norm[l], diff_k_norm=diff_k_norm[l],
                 diff_lam=diff_lam[l], diff_subln_w=diff_subln_w[l], na_q_norm=na_q_norm[l],
                 na_k_norm=na_k_norm[l], na_rpb=na_rpb[l], w_branch_a=w_branch_a[l], w_branch_b=w_branch_b[l],
                 w_branch_c=w_branch_c[l], w_out=w_out[l], ffn_w_up=ffn_w_up[l], ffn_conv_w=ffn_conv_w[l],
                 ffn_conv_b=ffn_conv_b[l], ffn_w_down=ffn_w_down[l])
        mod_ctx = jax.nn.silu(c_ctx)[None, :] @ w_ada[l] + b_ada[l]
        y_p, ctx_t = trunk_layer(y_p, mod_ctx, P, l, None)
        new_dk.append(ctx_t[0])
        new_dv.append(ctx_t[1])
        new_nk.append(ctx_t[2])
        new_nv.append(ctx_t[3])
        new_s.append(ctx_t[4])
        mod_lat = jax.nn.silu(c) @ w_ada[l] + b_ada[l]
        cache = (cache_diff_k[:, l], cache_diff_v[:, l], cache_na_k[:, l], cache_na_v[:, l], state_ssm[:, l])
        y_s, _ = trunk_layer(y_s, mod_lat, P, l, cache)
    return (y_p, y_s, jnp.stack(new_dk, axis=1), jnp.stack(new_dv, axis=1), jnp.stack(new_nk, axis=1),
            jnp.stack(new_nv, axis=1), jnp.stack(new_s, axis=1))
```

```python
import functools
import math

import numpy as np
import jax
import jax.numpy as jnp
from jax import lax
from jax.experimental import pallas as pl
from jax.experimental.pallas import tpu as pltpu

F32 = jnp.float32
BF16 = jnp.bfloat16

D_MODEL = 2048
GRID_W = 64
SSD_INNER = 1024
SSD_HEAD_DIM = 64
SSD_HEADS = 16
SSD_GROUPS = 2
SSD_STATE = 128
SSD_CHUNK = 128
DIFF_HEADS = 8
DIFF_HEAD_DIM = 64
NA_HEADS = 16
NA_HEAD_DIM = 64
NA_KH = 8
NA_KW = 16
D_FF = 5632
ROPE_BASE = 10000.0
EPS = 1e-6

LANES = 128
VMEM_LIMIT_BYTES = 56 * 1024 * 1024

TN = 512
COL_G = 0
COL_QD = 6144
COL_KD = 7168
COL_VD = 8192
COL_QN = 9216
COL_KN = 10240
COL_VN = 11264
COL_Z = 12288
COL_XS = 13312
COL_BC = 14336
COL_DT = 14848
IN_COLS_P = 15360
N_GATE_TILES = COL_QD // TN


def _cparams(*sem):
    return pltpu.CompilerParams(dimension_semantics=sem, vmem_limit_bytes=VMEM_LIMIT_BYTES)


def _sigmoid(x):
    return 1.0 / (1.0 + jnp.exp(-x))


def _silu(x):
    return x * _sigmoid(x)


def _bdot(a, b):
    return jnp.dot(a, b, preferred_element_type=F32)


def _bdot_nt(a, b):
    return lax.dot_general(a, b, (((1,), (1,)), ((), ())), preferred_element_type=F32)


def _split3(x):
    p1 = x.astype(BF16)
    r1 = x - p1.astype(F32)
    p2 = r1.astype(BF16)
    p3 = (r1 - p2.astype(F32)).astype(BF16)
    return p1, p2, p3


def _sel_right(x, e):
    p1, p2, p3 = _split3(x)
    return _bdot(p1, e) + _bdot(p2, e) + _bdot(p3, e)


def _sel_left(t, x):
    p1, p2, p3 = _split3(x)
    return _bdot(t, p1) + _bdot(t, p2) + _bdot(t, p3)


def _seg64_rms(a, bd):
    sq = a * a
    hi = sq.astype(BF16)
    lo = (sq - hi.astype(F32)).astype(BF16)
    ss = _bdot(hi, bd) + _bdot(lo, bd)
    return a * lax.rsqrt(ss * (1.0 / 64.0) + EPS)


def _ada_kernel(c_ref, w_ref, b_ref, o_ref):
    c = c_ref[...]
    s = _silu(c).astype(BF16)
    o_ref[0] = _bdot(s, w_ref[0].astype(BF16)) + b_ref[0]


def _ada(c_rows, w_ada, b_ada):
    depth, d, n = w_ada.shape
    tn = 1024
    return pl.pallas_call(
        _ada_kernel,
        grid=(depth, n // tn),
        in_specs=[
            pl.BlockSpec((8, d), lambda l, j: (0, 0)),
            pl.BlockSpec((1, d, tn), lambda l, j: (l, 0, j)),
            pl.BlockSpec((1, 1, tn), lambda l, j: (l, 0, j)),
        ],
        out_specs=pl.BlockSpec((1, 8, tn), lambda l, j: (l, 0, j)),
        out_shape=jax.ShapeDtypeStruct((depth, 8, n), F32),
        compiler_params=_cparams("arbitrary", "arbitrary"),
        name="ada_mod",
    )(c_rows, w_ada, b_ada.reshape(depth, 1, n))


def _modulated_norm(x, nw, shift, scale):
    ms = jnp.mean(x * x, axis=-1, keepdims=True)
    y = x * lax.rsqrt(ms + EPS) * nw
    return y * (1.0 + scale) + shift


def _inproj_kernel(x_ref, nw_ref, mod_ref, w_ref, ew_ref, bd_ref, o_ref, h_ref):
    j = pl.program_id(1)

    @pl.when(j == 0)
    def _():
        h = _modulated_norm(x_ref[...], nw_ref[...], mod_ref[0, 0:1, :], mod_ref[0, 1:2, :])
        h_ref[...] = h.astype(BF16)

    acc = _bdot(h_ref[...], w_ref[...])
    is_gate = j < N_GATE_TILES
    is_norm = ((j >= COL_QD // TN) & (j < COL_VD // TN)) | ((j >= COL_QN // TN) & (j < COL_VN // TN))

    @pl.when(is_gate)
    def _():
        o_ref[...] = _sigmoid(acc)

    @pl.when(is_norm)
    def _():
        bd = bd_ref[...]
        parts = [_seg64_rms(acc[:, c * LANES:(c + 1) * LANES], bd) for c in range(TN // LANES)]
        o_ref[...] = jnp.concatenate(parts, axis=1) * ew_ref[...]

    @pl.when(jnp.logical_not(is_gate | is_norm))
    def _():
        o_ref[...] = acc


def _inproj(x2d, nw, mod, w_p, ew, bd, tm, mod_row):
    t, d = x2d.shape
    n = w_p.shape[1]
    return pl.pallas_call(
        _inproj_kernel,
        grid=(t // tm, n // TN),
        in_specs=[
            pl.BlockSpec((tm, d), lambda i, j: (i, 0)),
            pl.BlockSpec((1, d), lambda i, j: (0, 0)),
            pl.BlockSpec((1, 6, d), lambda i, j: (mod_row(i), 0, 0)),
            pl.BlockSpec((d, TN), lambda i, j: (0, j)),
            pl.BlockSpec((1, TN), lambda i, j: (0, j)),
            pl.BlockSpec((LANES, LANES), lambda i, j: (0, 0)),
        ],
        out_specs=pl.BlockSpec((tm, TN), lambda i, j: (i, j)),
        out_shape=jax.ShapeDtypeStruct((t, n), F32),
        scratch_shapes=[pltpu.VMEM((tm, d), BF16)],
        compiler_params=_cparams("arbitrary", "arbitrary"),
        name="norm_inproj",
    )(x2d, nw, mod, w_p, ew, bd)


def _ssd_kernel(*refs, seq, has_h0, has_state_out):
    (z_ref, xs_ref, bc_ref, dt_ref, cwx_ref, cbx_ref, cwb_ref, cbb_ref, dtb_ref, alog_ref,
     dexp_ref, nw_ref, e_ref) = refs[:13]
    rest = list(refs[13:])
    h0_ref = rest.pop(0) if has_h0 else None
    y_ref = rest.pop(0)
    st_ref = rest.pop(0) if has_state_out else None
    xs_s, bc_s, ya_s, st_s = rest

    q = SSD_CHUNK
    nc = seq // q
    half = SSD_INNER // SSD_GROUPS

    def conv_chunk(c, carry):
        r0 = pl.multiple_of(c * q, q)
        for src, dst, cw, cb in ((xs_ref, xs_s, cwx_ref, cbx_ref), (bc_ref, bc_s, cwb_ref, cbb_ref)):
            x = src[pl.ds(r0, q), :]
            xp = src[pl.ds(jnp.maximum(r0 - 1, 0), 1), :]
            xn = src[pl.ds(jnp.minimum(r0 + q, seq - 1), 1), :]
            xp = jnp.where(c == 0, 0.0, xp)
            xn = jnp.where(c == nc - 1, 0.0, xn)
            rows = lax.broadcasted_iota(jnp.int32, x.shape, 0)
            prev = jnp.where(rows == 0, xp, pltpu.roll(x, 1, 0))
            nxt = jnp.where(rows == q - 1, xn, pltpu.roll(x, q - 1, 0))
            y = cw[0:1, :] * prev + cw[1:2, :] * x + cw[2:3, :] * nxt + cb[...]
            dst[pl.ds(r0, q), :] = _silu(y)
        return carry

    lax.fori_loop(0, nc, conv_chunk, 0)

    a_vec = -jnp.exp(alog_ref[...])
    dt_bias = dtb_ref[...]
    ri = lax.broadcasted_iota(jnp.int32, (q, q), 0)
    ci = lax.broadcasted_iota(jnp.int32, (q, q), 1)

    for d in (0, 1):
        tri = (ri >= ci) if d == 0 else (ci >= ri)
        tri_b = jnp.where(tri, 1.0, 0.0).astype(BF16)
        for g in range(SSD_GROUPS):
            if has_h0:
                st_s[g] = h0_ref[0, d, g * half:(g + 1) * half, :].T
            else:
                st_s[g] = jnp.zeros((SSD_STATE, half), F32)

        def chunk(i, carry, d=d, tri=tri, tri_b=tri_b):
            c = i if d == 0 else nc - 1 - i
            r0 = pl.multiple_of(c * q, q)
            x_dt = dt_ref[pl.ds(r0, q), :] + dt_bias
            dtc = jnp.maximum(x_dt, 0.0) + jnp.log1p(jnp.exp(-jnp.abs(x_dt)))
            la = dtc * a_vec
            cum = _sel_left(tri_b, la)
            cum_t = cum.T
            e = e_ref[d]
            dt_x = _sel_right(dtc, e)
            cum_x = _sel_right(cum, e)
            last = q - 1 if d == 0 else 0
            cl = cum_x[last:last + 1, :]
            xd = xs_s[pl.ds(r0, q), :] * dt_x
            xdb = xd.astype(BF16)
            xdd = (xd * jnp.exp(cl - cum_x)).astype(BF16)
            ecum = jnp.exp(cum_x)
            bcv = bc_s[pl.ds(r0, q), :]
            for g in range(SSD_GROUPS):
                bg = bcv[:, g * SSD_STATE:(g + 1) * SSD_STATE]
                cg = bcv[:, (SSD_GROUPS + g) * SSD_STATE:(SSD_GROUPS + g + 1) * SSD_STATE]
                bgb = bg.astype(BF16)
                cgb = cg.astype(BF16)
                gm = _bdot_nt(cgb, bgb)
                st = st_s[g]
                y_off = _bdot(cgb, st.astype(BF16)) * ecum[:, g * half:(g + 1) * half]
                ys = []
                for hh in range(SSD_HEADS // SSD_GROUPS):
                    h = g * (SSD_HEADS // SSD_GROUPS) + hh
                    k = d * SSD_HEADS + h
                    decay = jnp.where(tri, jnp.exp(cum[:, k:k + 1] - cum_t[k:k + 1, :]), 0.0)
                    m = (gm * decay).astype(BF16)
                    ys.append(_bdot(m, xdb[:, h * SSD_HEAD_DIM:(h + 1) * SSD_HEAD_DIM]))
                yg = jnp.concatenate(ys, axis=1) + y_off
                if d == 0:
                    ya_s[pl.ds(r0, q), g * half:(g + 1) * half] = yg
                else:
                    ya_s[pl.ds(r0, q), g * half:(g + 1) * half] += yg
                st_s[g] = (st * jnp.exp(cl[:, g * half:(g + 1) * half])
                           + _bdot(bg.T.astype(BF16), xdd[:, g * half:(g + 1) * half]))
            return carry

        lax.fori_loop(0, nc, chunk, 0)
        if has_state_out:
            for g in range(SSD_GROUPS):
                st_ref[0, d, g * half:(g + 1) * half, :] = st_s[g].T

    def fin_chunk(c, carry):
        r0 = pl.multiple_of(c * q, q)
        y = ya_s[pl.ds(r0, q), :] + dexp_ref[...] * xs_s[pl.ds(r0, q), :]
        y = y * _silu(z_ref[pl.ds(r0, q), :])
        for g in range(SSD_GROUPS):
            v = y[:, g * half:(g + 1) * half]
            ms = jnp.mean(v * v, axis=-1, keepdims=True)
            out = v * lax.rsqrt(ms + EPS) * nw_ref[:, g * half:(g + 1) * half]
            y_ref[pl.ds(r0, q), g * half:(g + 1) * half] = out.astype(BF16)
        return carry

    lax.fori_loop(0, nc, fin_chunk, 0)


def _ssd(u, nb, seq, pw, h0):
    has_h0 = h0 is not None
    has_state_out = not has_h0
    c1 = lambda b: (0, 0)
    in_specs = [
        pl.BlockSpec((seq, SSD_INNER), lambda b: (b, COL_Z // SSD_INNER)),
        pl.BlockSpec((seq, SSD_INNER), lambda b: (b, COL_XS // SSD_INNER)),
        pl.BlockSpec((seq, 512), lambda b: (b, COL_BC // 512)),
        pl.BlockSpec((seq, LANES), lambda b: (b, COL_DT // LANES)),
        pl.BlockSpec((3, SSD_INNER), c1),
        pl.BlockSpec((1, SSD_INNER), c1),
        pl.BlockSpec((3, 512), c1),
        pl.BlockSpec((1, 512), c1),
        pl.BlockSpec((1, LANES), c1),
        pl.BlockSpec((1, LANES), c1),
        pl.BlockSpec((1, SSD_INNER), c1),
        pl.BlockSpec((1, SSD_INNER), c1),
        pl.BlockSpec((2, LANES, SSD_INNER), lambda b: (0, 0, 0)),
    ]
    args = [u, u, u, u, pw["cw_xs"], pw["cb_xs"], pw["cw_bc"], pw["cb_bc"], pw["dt_bias"], pw["a_log"],
            pw["d_exp"], pw["ssd_nw"], pw["e_heads"]]
    if has_h0:
        in_specs.append(pl.BlockSpec((1, 2, SSD_INNER, SSD_STATE), lambda b: (b, 0, 0, 0)))
        args.append(h0)
    out_specs = [pl.BlockSpec((seq, SSD_INNER), lambda b: (b, 0))]
    out_shape = [jax.ShapeDtypeStruct((nb * seq, SSD_INNER), BF16)]
    if has_state_out:
        out_specs.append(pl.BlockSpec((1, 2, SSD_INNER, SSD_STATE), lambda b: (b, 0, 0, 0)))
        out_shape.append(jax.ShapeDtypeStruct((nb, 2, SSD_INNER, SSD_STATE), F32))
    res = pl.pallas_call(
        functools.partial(_ssd_kernel, seq=seq, has_h0=has_h0, has_state_out=has_state_out),
        grid=(nb,),
        in_specs=in_specs,
        out_specs=out_specs,
        out_shape=out_shape,
        scratch_shapes=[
            pltpu.VMEM((seq, SSD_INNER), F32),
            pltpu.VMEM((seq, 512), F32),
            pltpu.VMEM((seq, SSD_INNER), F32),
            pltpu.VMEM((SSD_GROUPS, SSD_STATE, SSD_INNER // SSD_GROUPS), F32),
        ],
        compiler_params=_cparams("arbitrary"),
        name="ssd_bidir",
    )(*args)
    return res if has_state_out else (res[0], None)


def _softmax_rows(parts):
    m = functools.reduce(jnp.maximum, [jnp.max(s, axis=-1, keepdims=True) for s in parts])
    es = [jnp.exp(s - m) for s in parts]
    den = functools.reduce(lambda a, b: a + b, [jnp.sum(e, axis=-1, keepdims=True) for e in es])
    inv = 1.0 / den
    return [e * inv for e in es]


def _lambda(lp, lam_init):
    a = jnp.sum(lp[0:1, :] * lp[1:2, :], axis=-1, keepdims=True)
    b = jnp.sum(lp[2:3, :] * lp[3:4, :], axis=-1, keepdims=True)
    return jnp.exp(a) - jnp.exp(b) + lam_init


def _subln(o, w, lam_init):
    ms = jnp.mean(o * o, axis=-1, keepdims=True)
    return o * lax.rsqrt(ms + EPS) * w * (1.0 - lam_init)


def _rope(x, cos, sin):
    lane = lax.broadcasted_iota(jnp.int32, x.shape, 1)
    swapped = jnp.where((lane & 16) == 0, pltpu.roll(x, LANES - 16, 1), pltpu.roll(x, 16, 1))
    return x * cos + swapped * sin


def _ctx_attn_kernel(q_ref, k_ref, v_ref, lamp_ref, sw_ref, o_ref, *, diff, lam_init):
    scale = 64 ** -0.5
    q = q_ref[...]
    k = k_ref[...]
    v = v_ref[...].astype(BF16)
    ps = []
    for t in (0, 1):
        qt = q[:, t * 64:(t + 1) * 64].astype(BF16)
        kt = k[:, t * 64:(t + 1) * 64].astype(BF16)
        s = _bdot_nt(qt, kt) * scale
        ps.append(_softmax_rows([s])[0])
    if diff:
        lam = _lambda(lamp_ref[...], lam_init)
        att = (ps[0] - lam * ps[1]).astype(BF16)
        o = _subln(_bdot(att, v), sw_ref[...], lam_init)
    else:
        o = jnp.concatenate([_bdot(ps[t].astype(BF16), v[:, t * 64:(t + 1) * 64]) for t in (0, 1)], axis=1)
    o_ref[...] = o.astype(BF16)


def _ctx_attn(u, nb, seq, col_q, col_k, col_v, lamp, sw, diff, lam_init):
    cq, ck, cv = col_q // LANES, col_k // LANES, col_v // LANES
    return pl.pallas_call(
        functools.partial(_ctx_attn_kernel, diff=diff, lam_init=lam_init),
        grid=(nb, 8),
        in_specs=[
            pl.BlockSpec((seq, LANES), lambda b, h: (b, cq + h)),
            pl.BlockSpec((seq, LANES), lambda b, h: (b, ck + h)),
            pl.BlockSpec((seq, LANES), lambda b, h: (b, cv + h)),
            pl.BlockSpec((4, 64), lambda b, h: (0, 0)),
            pl.BlockSpec((1, LANES), lambda b, h: (0, 0)),
        ],
        out_specs=pl.BlockSpec((seq, LANES), lambda b, h: (b, h)),
        out_shape=jax.ShapeDtypeStruct((nb * seq, 8 * LANES), BF16),
        compiler_params=_cparams("arbitrary", "arbitrary"),
        name="ctx_diff_attn" if diff else "ctx_softmax_attn",
    )(u, u, u, lamp, sw)


def _lat_diff_kernel(q_ref, k_ref, v_ref, ck_ref, cv_ref, cosq_ref, sinq_ref, cosk_ref, sink_ref,
                     lamp_ref, sw_ref, o_ref, *, lam_init):
    scale = 64 ** -0.5
    q = _rope(q_ref[...], cosq_ref[...], sinq_ref[...])
    k = _rope(k_ref[...], cosk_ref[...], sink_ref[...])
    ck = ck_ref[0, 0]
    v = v_ref[...].astype(BF16)
    cv = cv_ref[0, 0].astype(BF16)
    pl_, pc_ = [], []
    for t in (0, 1):
        sl = slice(t * 64, (t + 1) * 64)
        qt = q[:, sl].astype(BF16)
        s_loc = _bdot_nt(qt, k[:, sl].astype(BF16)) * scale
        s_ctx = _bdot_nt(qt, ck[:, sl].astype(BF16)) * scale
        p_loc, p_ctx = _softmax_rows([s_loc, s_ctx])
        pl_.append(p_loc)
        pc_.append(p_ctx)
    lam = _lambda(lamp_ref[...], lam_init)
    a_loc = (pl_[0] - lam * pl_[1]).astype(BF16)
    a_ctx = (pc_[0] - lam * pc_[1]).astype(BF16)
    o = _bdot(a_loc, v) + _bdot(a_ctx, cv)
    o_ref[...] = _subln(o, sw_ref[...], lam_init).astype(BF16)


def _lat_diff(u, nb, seq, cache_k, cache_v, layer, cos, sin, lamp, sw, lam_init):
    tq = 256
    nq = seq // tq
    cq, ck, cv = COL_QD // LANES, COL_KD // LANES, COL_VD // LANES
    past = cache_k.shape[2]
    return pl.pallas_call(
        functools.partial(_lat_diff_kernel, lam_init=lam_init),
        grid=(nb, DIFF_HEADS, nq),
        in_specs=[
            pl.BlockSpec((tq, LANES), lambda b, h, i: (b * nq + i, cq + h)),
            pl.BlockSpec((seq, LANES), lambda b, h, i: (b, ck + h)),
            pl.BlockSpec((seq, LANES), lambda b, h, i: (b, cv + h)),
            pl.BlockSpec((1, 1, past, LANES), lambda b, h, i: (b, layer, 0, h)),
            pl.BlockSpec((1, 1, past, LANES), lambda b, h, i: (b, layer, 0, h)),
            pl.BlockSpec((tq, LANES), lambda b, h, i: (i, 0)),
            pl.BlockSpec((tq, LANES), lambda b, h, i: (i, 0)),
            pl.BlockSpec((seq, LANES), lambda b, h, i: (0, 0)),
            pl.BlockSpec((seq, LANES), lambda b, h, i: (0, 0)),
            pl.BlockSpec((4, 64), lambda b, h, i: (0, 0)),
            pl.BlockSpec((1, LANES), lambda b, h, i: (0, 0)),
        ],
        out_specs=pl.BlockSpec((tq, LANES), lambda b, h, i: (b * nq + i, h)),
        out_shape=jax.ShapeDtypeStruct((nb * seq, DIFF_HEADS * LANES), BF16),
        compiler_params=_cparams("arbitrary", "arbitrary", "arbitrary"),
        name="lat_diff_attn",
    )(u, u, u, cache_k, cache_v, cos, sin, cos, sin, lamp, sw)


def _lat_na_kernel(q_ref, k_ref, v_ref, ck_ref, cv_ref, tb_ref, o_ref, *, rows):
    scale = NA_HEAD_DIM ** -0.5
    kh = min(NA_KH, rows)
    for t in (0, 1):
        sl = slice(t * 64, (t + 1) * 64)
        qh = q_ref[:, sl].astype(BF16)
        kh_all = k_ref[:, sl].astype(BF16)
        vh_all = v_ref[:, sl].astype(BF16)
        ckh = ck_ref[0, 0][:, sl].astype(BF16)
        cvh = cv_ref[0, 0][:, sl].astype(BF16)
        for qr in range(rows):
            r0 = min(max(qr - kh // 2, 0), rows - kh)
            a0 = r0 - qr + NA_KH - 1
            qb = qh[qr * GRID_W:(qr + 1) * GRID_W]
            kw = kh_all[r0 * GRID_W:(r0 + kh) * GRID_W]
            vw = vh_all[r0 * GRID_W:(r0 + kh) * GRID_W]
            bias = tb_ref[t, :, a0 * GRID_W:(a0 + kh) * GRID_W]
            s_loc = _bdot_nt(qb, kw) * scale + bias
            s_ctx = _bdot_nt(qb, ckh) * scale
            p_loc, p_ctx = _softmax_rows([s_loc, s_ctx])
            o = _bdot(p_loc.astype(BF16), vw) + _bdot(p_ctx.astype(BF16), cvh)
            o_ref[qr * GRID_W:(qr + 1) * GRID_W, sl] = o.astype(BF16)


def _lat_na(u, nb, seq, cache_k, cache_v, layer, table):
    cq, ck, cv = COL_QN // LANES, COL_KN // LANES, COL_VN // LANES
    past = cache_k.shape[2]
    rows = seq // GRID_W
    ncol = table.shape[-1]
    return pl.pallas_call(
        functools.partial(_lat_na_kernel, rows=rows),
        grid=(nb, NA_HEADS // 2),
        in_specs=[
            pl.BlockSpec((seq, LANES), lambda b, h: (b, cq + h)),
            pl.BlockSpec((seq, LANES), lambda b, h: (b, ck + h)),
            pl.BlockSpec((seq, LANES), lambda b, h: (b, cv + h)),
            pl.BlockSpec((1, 1, past, LANES), lambda b, h: (b, layer, 0, h)),
            pl.BlockSpec((1, 1, past, LANES), lambda b, h: (b, layer, 0, h)),
            pl.BlockSpec((2, GRID_W, ncol), lambda b, h: (h, 0, 0)),
        ],
        out_specs=pl.BlockSpec((seq, LANES), lambda b, h: (b, h)),
        out_shape=jax.ShapeDtypeStruct((nb * seq, NA_HEADS * NA_HEAD_DIM), BF16),
        compiler_params=_cparams("arbitrary", "arbitrary"),
        name="lat_nbr_attn",
    )(u, u, u, cache_k, cache_v, table)


def _merge_kernel(ya_ref, yb_ref, yc_ref, wa_ref, wb_ref, wc_ref, ga_ref, gb_ref, gc_ref, o_ref):
    m = (ga_ref[...] * _bdot(ya_ref[...], wa_ref[...])
         + gb_ref[...] * _bdot(yb_ref[...], wb_ref[...])
         + gc_ref[...] * _bdot(yc_ref[...], wc_ref[...]))
    o_ref[...] = m.astype(BF16)


def _merge(ya, yb, yc, wa, wb, wc, u, tm):
    t, kk = ya.shape
    n = wa.shape[1]
    nt = n // TN
    ysp = pl.BlockSpec((tm, kk), lambda i, j: (i, 0))
    wsp = pl.BlockSpec((kk, TN), lambda i, j: (0, j))
    return pl.pallas_call(
        _merge_kernel,
        grid=(t // tm, nt),
        in_specs=[ysp, ysp, ysp, wsp, wsp, wsp,
                  pl.BlockSpec((tm, TN), lambda i, j: (i, j)),
                  pl.BlockSpec((tm, TN), lambda i, j: (i, nt + j)),
                  pl.BlockSpec((tm, TN), lambda i, j: (i, 2 * nt + j))],
        out_specs=pl.BlockSpec((tm, TN), lambda i, j: (i, j)),
        out_shape=jax.ShapeDtypeStruct((t, n), BF16),
        compiler_params=_cparams("arbitrary", "arbitrary"),
        name="branch_merge",
    )(ya, yb, yc, wa, wb, wc, u, u, u)


def _resid_mm_kernel(a_ref, w_ref, x_ref, mod_ref, o_ref, *, gate_row):
    o_ref[...] = x_ref[...] + mod_ref[0, gate_row:gate_row + 1, :] * _bdot(a_ref[...], w_ref[...])


def _resid_mm(a, w, x2d, mod, gate_row, tm, mod_row, name):
    t, kk = a.shape
    n = w.shape[1]
    return pl.pallas_call(
        functools.partial(_resid_mm_kernel, gate_row=gate_row),
        grid=(t // tm, n // TN),
        in_specs=[
            pl.BlockSpec((tm, kk), lambda i, j: (i, 0)),
            pl.BlockSpec((kk, TN), lambda i, j: (0, j)),
            pl.BlockSpec((tm, TN), lambda i, j: (i, j)),
            pl.BlockSpec((1, 6, TN), lambda i, j: (mod_row(i), 0, j)),
        ],
        out_specs=pl.BlockSpec((tm, TN), lambda i, j: (i, j)),
        out_shape=jax.ShapeDtypeStruct((t, n), F32),
        compiler_params=_cparams("arbitrary", "arbitrary"),
        name=name,
    )(a, w, x2d, mod)


def _ffn_up_kernel(x_ref, nw_ref, mod_ref, wv_ref, wg_ref, cwv_ref, cwg_ref, cbv_ref, cbg_ref, o_ref, h_ref,
                   *, seq):
    j = pl.program_id(1)

    @pl.when(j == 0)
    def _():
        h = _modulated_norm(x_ref[...], nw_ref[...], mod_ref[0, 3:4, :], mod_ref[0, 4:5, :])
        h_ref[...] = h.astype(BF16)

    h = h_ref[...]
    tm = h.shape[0]
    pos = lax.broadcasted_iota(jnp.int32, (tm, TN), 0) % seq

    def conv(acc, cw, cb):
        prev = jnp.where(pos == 0, 0.0, pltpu.roll(acc, 1, 0))
        nxt = jnp.where(pos == seq - 1, 0.0, pltpu.roll(acc, tm - 1, 0))
        return cw[0:1, :] * prev + cw[1:2, :] * acc + cw[2:3, :] * nxt + cb[...]

    val = conv(_bdot(h, wv_ref[...]), cwv_ref, cbv_ref)
    gt = conv(_bdot(h, wg_ref[...]), cwg_ref, cbg_ref)
    o_ref[...] = (_silu(gt) * val).astype(BF16)


def _ffn_up(x2d, nw, mod, w_up, cw, cb, tm, seq, mod_row):
    t, d = x2d.shape
    nt = D_FF // TN
    return pl.pallas_call(
        functools.partial(_ffn_up_kernel, seq=seq),
        grid=(t // tm, nt),
        in_specs=[
            pl.BlockSpec((tm, d), lambda i, j: (i, 0)),
            pl.BlockSpec((1, d), lambda i, j: (0, 0)),
            pl.BlockSpec((1, 6, d), lambda i, j: (mod_row(i), 0, 0)),
            pl.BlockSpec((d, TN), lambda i, j: (0, j)),
            pl.BlockSpec((d, TN), lambda i, j: (0, nt + j)),
            pl.BlockSpec((3, TN), lambda i, j: (0, j)),
            pl.BlockSpec((3, TN), lambda i, j: (0, nt + j)),
            pl.BlockSpec((1, TN), lambda i, j: (0, j)),
            pl.BlockSpec((1, TN), lambda i, j: (0, nt + j)),
        ],
        out_specs=pl.BlockSpec((tm, TN), lambda i, j: (i, j)),
        out_shape=jax.ShapeDtypeStruct((t, D_FF), BF16),
        scratch_shapes=[pltpu.VMEM((tm, d), BF16)],
        compiler_params=_cparams("arbitrary", "arbitrary"),
        name="norm_ffn_up",
    )(x2d, nw, mod, w_up, w_up, cw, cw, cb, cb)


def _rope_tables(seq):
    t = np.arange(seq)
    row = (t // GRID_W).astype(np.float32)
    col = (t % GRID_W).astype(np.float32)
    n_freq = DIFF_HEAD_DIM // 4
    inv = jnp.asarray(ROPE_BASE, F32) ** (-jnp.arange(n_freq, dtype=F32) / n_freq)
    ang_r = jnp.asarray(row)[:, None] * inv
    ang_c = jnp.asarray(col)[:, None] * inv
    cr, sr, cc, sc = jnp.cos(ang_r), jnp.sin(ang_r), jnp.cos(ang_c), jnp.sin(ang_c)
    cos64 = jnp.concatenate([cr, cr, cc, cc], axis=-1)
    sin64 = jnp.concatenate([-sr, sr, -sc, sc], axis=-1)
    return jnp.tile(cos64, (1, 2)), jnp.tile(sin64, (1, 2))


def _na_bias_table(rpb):
    qc = np.arange(GRID_W)[:, None]
    kc = np.arange(GRID_W)[None, :]
    dc = np.clip(kc - qc + NA_KW - 1, 0, 2 * NA_KW - 2)
    wstart = np.clip(qc - NA_KW // 2, 0, GRID_W - NA_KW)
    valid = (kc >= wstart) & (kc < wstart + NA_KW)
    tbl = rpb.astype(F32)[:, :, dc]
    tbl = jnp.where(valid[None, None], tbl, -jnp.inf)
    return jnp.transpose(tbl, (0, 2, 1, 3)).reshape(rpb.shape[0], GRID_W, -1)


def _head_expand():
    e = np.zeros((2, LANES, SSD_INNER), np.float32)
    for d in range(2):
        for h in range(SSD_HEADS):
            e[d, d * SSD_HEADS + h, h * SSD_HEAD_DIM:(h + 1) * SSD_HEAD_DIM] = 1.0
    return jnp.asarray(e, BF16)


def _seg_ones():
    i = np.arange(LANES)
    return jnp.asarray((i[:, None] // 64 == i[None, :] // 64).astype(np.float32), BF16)


def _pad_lanes(v, n):
    return jnp.pad(v.reshape(1, -1), ((0, 0), (0, n - v.size)))


def _trunk_layer(x2d, nb, seq, mod, mod_row, pw, layer, cache, consts):
    tm = 512
    u = _inproj(x2d, pw["norm1_w"], mod, pw["w_in"], pw["ew"], consts["bd"], tm, mod_row(tm))
    lam_init = 0.8 - 0.6 * math.exp(-0.3 * layer)
    if cache is None:
        y_a, states = _ssd(u, nb, seq, pw, None)
        y_b = _ctx_attn(u, nb, seq, COL_QD, COL_KD, COL_VD, pw["diff_lam"], pw["subln_w"], True, lam_init)
        y_c = _ctx_attn(u, nb, seq, COL_QN, COL_KN, COL_VN, pw["diff_lam"], pw["subln_w"], False, lam_init)
    else:
        cdk, cdv, cnk, cnv, h0 = cache
        y_a, states = _ssd(u, nb, seq, pw, h0)
        y_b = _lat_diff(u, nb, seq, cdk, cdv, layer, consts["cos"], consts["sin"], pw["diff_lam"],
                        pw["subln_w"], lam_init)
        y_c = _lat_na(u, nb, seq, cnk, cnv, layer, pw["na_table"])
    merged = _merge(y_a, y_b, y_c, pw["w_a"], pw["w_b"], pw["w_c"], u, tm)
    x2d = _resid_mm(merged, pw["w_out"], x2d, mod, 2, tm, mod_row(tm), "out_proj")
    tmf = max(tm, seq)
    act = _ffn_up(x2d, pw["norm2_w"], mod, pw["w_up"], pw["ffn_cw"], pw["ffn_cb"], tmf, seq, mod_row(tmf))
    x2d = _resid_mm(act, pw["w_down"], x2d, mod, 5, tm, mod_row(tm), "ffn_down")
    return x2d, u, states


def _layer_params(l, norm1_w, norm2_w, w_in, ssd_conv_w, ssd_conv_b, ssd_dt_bias, ssd_a_log, ssd_d, ssd_norm_w,
                  diff_q_norm, diff_k_norm, diff_lam, diff_subln_w, na_q_norm, na_k_norm, na_rpb,
                  w_branch_a, w_branch_b, w_branch_c, w_out, ffn_w_up, ffn_conv_w, ffn_conv_b, ffn_w_down):
    w = w_in[l]
    o = np.cumsum([0, 1024, 1536, 32, 1024, 1024, 1024, 1024, 1024, 1024, 6144])
    z, xbc, dt, qd, kd, vd, qn, kn, vn, g = [w[:, o[i]:o[i + 1]] for i in range(10)]
    pad = jnp.zeros((w.shape[0], IN_COLS_P - COL_DT - 32), w.dtype)
    w_p = jnp.concatenate([g, qd, kd, vd, qn, kn, vn, z, xbc, dt, pad], axis=1).astype(BF16)
    ones = jnp.ones((1024,), F32)
    ew = jnp.concatenate([
        jnp.ones((COL_QD,), F32), jnp.tile(diff_q_norm[l], 16), jnp.tile(diff_k_norm[l], 16), ones,
        jnp.tile(na_q_norm[l], 16), jnp.tile(na_k_norm[l], 16), jnp.ones((IN_COLS_P - COL_VN,), F32)]).reshape(1, -1)
    return {
        "norm1_w": norm1_w[l].reshape(1, -1), "norm2_w": norm2_w[l].reshape(1, -1),
        "w_in": w_p, "ew": ew,
        "cw_xs": ssd_conv_w[l][:, :SSD_INNER], "cw_bc": ssd_conv_w[l][:, SSD_INNER:],
        "cb_xs": ssd_conv_b[l][:SSD_INNER].reshape(1, -1), "cb_bc": ssd_conv_b[l][SSD_INNER:].reshape(1, -1),
        "dt_bias": _pad_lanes(ssd_dt_bias[l], LANES), "a_log": _pad_lanes(ssd_a_log[l], LANES),
        "d_exp": jnp.repeat(ssd_d[l], SSD_HEAD_DIM).reshape(1, -1), "ssd_nw": ssd_norm_w[l].reshape(1, -1),
        "e_heads": _head_expand(),
        "diff_lam": diff_lam[l], "subln_w": diff_subln_w[l].reshape(1, -1),
        "na_table": _na_bias_table(na_rpb[l]),
        "w_a": w_branch_a[l].astype(BF16), "w_b": w_branch_b[l].astype(BF16), "w_c": w_branch_c[l].astype(BF16),
        "w_out": w_out[l].astype(BF16), "w_up": ffn_w_up[l].astype(BF16), "w_down": ffn_w_down[l].astype(BF16),
        "ffn_cw": ffn_conv_w[l], "ffn_cb": ffn_conv_b[l].reshape(1, -1),
    }


def kernel(x_prompt, x_sample, c, cache_diff_k, cache_diff_v, cache_na_k, cache_na_v, state_ssm, c_ctx, norm1_w, norm2_w, w_ada, b_ada, w_in, ssd_conv_w, ssd_conv_b, ssd_dt_bias, ssd_a_log, ssd_d, ssd_norm_w, diff_q_norm, diff_k_norm, diff_lam, diff_subln_w, na_q_norm, na_k_norm, na_rpb, w_branch_a, w_branch_b, w_branch_c, w_out, ffn_w_up, ffn_conv_w, ffn_conv_b, ffn_w_down):
    batch, seq, d = x_prompt.shape
    dec_batch, dec_seq, _ = x_sample.shape
    depth = w_in.shape[0]
    past = cache_diff_k.shape[2]
    assert d == D_MODEL and dec_batch + 1 <= 8 and seq % SSD_CHUNK == 0 and dec_seq % (8 * GRID_W) == 0

    c_rows = jnp.concatenate([c_ctx.reshape(1, d), c, jnp.zeros((8 - 1 - dec_batch, d), F32)], axis=0)
    mod_all = _ada(c_rows, w_ada, b_ada).reshape(depth, 8, 6, d)

    cos, sin = _rope_tables(dec_seq)
    consts = {"bd": _seg_ones(), "cos": cos, "sin": sin}
    cdk = cache_diff_k.reshape(dec_batch, depth, past, DIFF_HEADS * 2 * DIFF_HEAD_DIM)
    cdv = cache_diff_v.reshape(dec_batch, depth, past, DIFF_HEADS * 2 * DIFF_HEAD_DIM)
    cnk = cache_na_k.reshape(dec_batch, depth, past, NA_HEADS * NA_HEAD_DIM)
    cnv = cache_na_v.reshape(dec_batch, depth, past, NA_HEADS * NA_HEAD_DIM)
    h0 = state_ssm.reshape(dec_batch, depth, 2, SSD_INNER, SSD_STATE)

    y_p = x_prompt.reshape(batch * seq, d)
    y_s = x_sample.reshape(dec_batch * dec_seq, d)
    ctx_row = lambda tm: (lambda i: 0)
    lat_row = lambda tm: (lambda i: 1 + (i * tm) // dec_seq)
    new_dk, new_dv, new_nk, new_nv, new_s = [], [], [], [], []
    for l in range(depth):
        pw = _layer_params(l, norm1_w, norm2_w, w_in, ssd_conv_w, ssd_conv_b, ssd_dt_bias, ssd_a_log, ssd_d,
                           ssd_norm_w, diff_q_norm, diff_k_norm, diff_lam, diff_subln_w, na_q_norm, na_k_norm,
                           na_rpb, w_branch_a, w_branch_b, w_branch_c, w_out, ffn_w_up, ffn_conv_w, ffn_conv_b,
                           ffn_w_down)
        mod = mod_all[l]
        y_p, u, states = _trunk_layer(y_p, batch, seq, mod, ctx_row, pw, l, None, consts)
        new_dk.append(u[:, COL_KD:COL_KD + 1024].reshape(batch, seq, DIFF_HEADS, 2, DIFF_HEAD_DIM))
        new_dv.append(u[:, COL_VD:COL_VD + 1024].reshape(batch, seq, DIFF_HEADS, 2 * DIFF_HEAD_DIM))
        new_nk.append(u[:, COL_KN:COL_KN + 1024].reshape(batch, seq, NA_HEADS, NA_HEAD_DIM))
        new_nv.append(u[:, COL_VN:COL_VN + 1024].reshape(batch, seq, NA_HEADS, NA_HEAD_DIM))
        new_s.append(states.reshape(batch, 2, SSD_HEADS, SSD_HEAD_DIM, SSD_STATE))
        cache = (cdk, cdv, cnk, cnv, h0[:, l])
        y_s, _, _ = _trunk_layer(y_s, dec_batch, dec_seq, mod, lat_row, pw, l, cache, consts)
    return (y_p.reshape(batch, seq, d), y_s.reshape(dec_batch, dec_seq, d), jnp.stack(new_dk, axis=1),
            jnp.stack(new_dv, axis=1), jnp.stack(new_nk, axis=1), jnp.stack(new_nv, axis=1),
            jnp.stack(new_s, axis=1))
```

```python
import functools
import math

import numpy as np
import jax
import jax.numpy as jnp
from jax import lax
from jax.experimental import pallas as pl
from jax.experimental.pallas import tpu as pltpu

F32 = jnp.float32
BF16 = jnp.bfloat16

D_MODEL = 2048
GRID_W = 64
SSD_INNER = 1024
SSD_HEAD_DIM = 64
SSD_HEADS = 16
SSD_GROUPS = 2
SSD_STATE = 128
SSD_CHUNK = 128
SSD_BC = 2 * SSD_GROUPS * SSD_STATE
DIFF_HEADS = 8
DIFF_HEAD_DIM = 64
NA_HEADS = 16
NA_HEAD_DIM = 64
NA_KH = 8
NA_KW = 16
HEAD_COLS = 1024
D_FF = 5632
ROPE_BASE = 10000.0
EPS = 1e-6

LANES = 128
VMEM_LIMIT_BYTES = 56 * 1024 * 1024
SUB = 512

COL_G = 0
COL_Z = 6144
COL_XS = 7168
COL_BC = 8192
COL_DT = 8704
P1_COLS = 9216
P1_CHUNK = 3072
N_GATE_CHUNKS = COL_Z // P1_CHUNK

TM_NORM = 1024
TM_PROJ = 512
TM_MERGE = 512
TM_OUT = 512
TM_UP = 1024
TM_DOWN = 256


def _cparams(*sem):
    return pltpu.CompilerParams(dimension_semantics=sem, vmem_limit_bytes=VMEM_LIMIT_BYTES)


def _resident(shape):
    nd = len(shape)
    return pl.BlockSpec(shape, lambda *_: (0,) * nd, pipeline_mode=pl.Buffered(1))


def _sigmoid(x):
    return 1.0 / (1.0 + jnp.exp(-x))


def _silu(x):
    return x * _sigmoid(x)


def _bdot(a, b):
    return jnp.dot(a, b, preferred_element_type=F32)


def _bdot_nt(a, b):
    return lax.dot_general(a, b, (((1,), (1,)), ((), ())), preferred_element_type=F32)


def _split3(x):
    p1 = x.astype(BF16)
    r1 = x - p1.astype(F32)
    p2 = r1.astype(BF16)
    p3 = (r1 - p2.astype(F32)).astype(BF16)
    return p1, p2, p3


def _sel_right(x, e):
    p1, p2, p3 = _split3(x)
    return _bdot(p1, e) + _bdot(p2, e) + _bdot(p3, e)


def _sel_left(t, x):
    p1, p2, p3 = _split3(x)
    return _bdot(t, p1) + _bdot(t, p2) + _bdot(t, p3)


def _seg64_rms(a, bd):
    sq = a * a
    hi = sq.astype(BF16)
    lo = (sq - hi.astype(F32)).astype(BF16)
    ss = _bdot(hi, bd) + _bdot(lo, bd)
    return a * lax.rsqrt(ss * (1.0 / 64.0) + EPS)


def _modulated_norm(x, nw, shift, scale):
    ms = jnp.mean(x * x, axis=-1, keepdims=True)
    y = x * lax.rsqrt(ms + EPS) * nw
    return y * (1.0 + scale) + shift


def _ada_kernel(c_ref, w_ref, b_ref, o_ref):
    c = c_ref[...]
    s = _silu(c).astype(BF16)
    o_ref[0] = _bdot(s, w_ref[0].astype(BF16)) + b_ref[0]


def _ada(c_rows, w_ada, b_ada):
    depth, d, n = w_ada.shape
    tn = 1024
    return pl.pallas_call(
        _ada_kernel,
        grid=(depth, n // tn),
        in_specs=[
            pl.BlockSpec((8, d), lambda l, j: (0, 0)),
            pl.BlockSpec((1, d, tn), lambda l, j: (l, 0, j)),
            pl.BlockSpec((1, 1, tn), lambda l, j: (l, 0, j)),
        ],
        out_specs=pl.BlockSpec((1, 8, tn), lambda l, j: (l, 0, j)),
        out_shape=jax.ShapeDtypeStruct((depth, 8, n), F32),
        compiler_params=_cparams("arbitrary", "arbitrary"),
        name="ada_mod",
    )(c_rows, w_ada, b_ada.reshape(depth, 1, n))


def _norm_kernel(x_ref, nw_ref, mod_ref, h_ref):
    h_ref[...] = _modulated_norm(x_ref[...], nw_ref[...], mod_ref[0, 0:1, :], mod_ref[0, 1:2, :]).astype(BF16)


def _norm(x2d, nw, mod, mod_row):
    t, d = x2d.shape
    tm = TM_NORM
    return pl.pallas_call(
        _norm_kernel,
        grid=(t // tm,),
        in_specs=[
            pl.BlockSpec((tm, d), lambda i: (i, 0)),
            pl.BlockSpec((1, d), lambda i: (0, 0)),
            pl.BlockSpec((1, 6, d), lambda i: (mod_row(i, tm), 0, 0)),
        ],
        out_specs=pl.BlockSpec((tm, d), lambda i: (i, 0)),
        out_shape=jax.ShapeDtypeStruct((t, d), BF16),
        compiler_params=_cparams("arbitrary"),
        name="norm_mod",
    )(x2d, nw, mod)


def _proj1_kernel(h_ref, w_ref, o_ref):
    c = pl.program_id(0)
    h = h_ref[...]

    def tiles(epilogue):
        for s in range(P1_CHUNK // SUB):
            cols = slice(s * SUB, (s + 1) * SUB)
            o_ref[:, cols] = epilogue(_bdot(h, w_ref[:, cols]))

    @pl.when(c < N_GATE_CHUNKS)
    def _():
        tiles(_sigmoid)

    @pl.when(c >= N_GATE_CHUNKS)
    def _():
        tiles(lambda a: a)


def _proj1(h, w1):
    t, d = h.shape
    tm = TM_PROJ
    return pl.pallas_call(
        _proj1_kernel,
        grid=(P1_COLS // P1_CHUNK, t // tm),
        in_specs=[
            pl.BlockSpec((tm, d), lambda c, i: (i, 0)),
            pl.BlockSpec((d, P1_CHUNK), lambda c, i: (0, c)),
        ],
        out_specs=pl.BlockSpec((tm, P1_CHUNK), lambda c, i: (i, c)),
        out_shape=jax.ShapeDtypeStruct((t, P1_COLS), F32),
        compiler_params=_cparams("arbitrary", "arbitrary"),
        name="proj_gates_ssd",
    )(h, w1)


def _qkv_kernel(*refs, n_alias):
    h_ref, w_ref, qnw_ref, knw_ref, bd_ref = refs[:5]
    q_ref, k_ref, v_ref = refs[5 + n_alias:]
    h = h_ref[...]
    bd = bd_ref[...]
    for sec, (o_ref, nw_ref) in enumerate(((q_ref, qnw_ref), (k_ref, knw_ref), (v_ref, None))):
        for s in range(HEAD_COLS // SUB):
            acc = _bdot(h, w_ref[:, sec * HEAD_COLS + s * SUB: sec * HEAD_COLS + (s + 1) * SUB])
            if nw_ref is not None:
                parts = [_seg64_rms(acc[:, c * LANES:(c + 1) * LANES], bd) for c in range(SUB // LANES)]
                acc = jnp.concatenate(parts, axis=1) * nw_ref[:, s * SUB:(s + 1) * SUB]
            o_ref[..., s * SUB:(s + 1) * SUB] = acc.reshape(o_ref.shape[:-1] + (SUB,))


def _qkv(h, w, qnw, knw, bd, nb, seq, layer, kv_prev, name):
    t, d = h.shape
    tm = TM_PROJ
    in_specs = [
        pl.BlockSpec((tm, d), lambda i: (i, 0)),
        _resident((d, 3 * HEAD_COLS)),
        pl.BlockSpec((1, HEAD_COLS), lambda i: (0, 0)),
        pl.BlockSpec((1, HEAD_COLS), lambda i: (0, 0)),
        pl.BlockSpec((LANES, LANES), lambda i: (0, 0)),
    ]
    args = [h, w, qnw, knw, bd]
    q_spec = pl.BlockSpec((tm, HEAD_COLS), lambda i: (i, 0))
    q_shape = jax.ShapeDtypeStruct((t, HEAD_COLS), F32)
    aliases = {}
    n_alias = 0
    if kv_prev is None:
        kv_spec, kv_shape = q_spec, q_shape
    else:
        depth, k_prev, v_prev = kv_prev
        assert tm % seq == 0
        kv_spec = pl.BlockSpec((tm // seq, None, seq, HEAD_COLS), lambda i: (i, layer, 0, 0))
        kv_shape = jax.ShapeDtypeStruct((nb, depth, seq, HEAD_COLS), F32)
        if k_prev is not None:
            in_specs += [pl.BlockSpec(memory_space=pl.ANY)] * 2
            args += [k_prev, v_prev]
            aliases = {5: 1, 6: 2}
            n_alias = 2
    return pl.pallas_call(
        functools.partial(_qkv_kernel, n_alias=n_alias),
        grid=(t // tm,),
        in_specs=in_specs,
        out_specs=[q_spec, kv_spec, kv_spec],
        out_shape=[q_shape, kv_shape, kv_shape],
        input_output_aliases=aliases,
        compiler_params=_cparams("arbitrary"),
        name=name,
    )(*args)


def _ssd_kernel(*refs, seq, has_h0, has_state_out, n_alias):
    (z_ref, xs_ref, bc_ref, dt_ref, cwx_ref, cbx_ref, cwb_ref, cbb_ref, dtb_ref, alog_ref,
     dexp_ref, nw_ref, e_ref) = refs[:13]
    rest = list(refs[13:])
    h0_ref = rest.pop(0) if has_h0 else None
    rest = rest[n_alias:]
    y_ref = rest.pop(0)
    st_ref = rest.pop(0) if has_state_out else None
    xs_s, bc_s, ya_s, st_s = rest

    q = SSD_CHUNK
    nc = seq // q
    half = SSD_INNER // SSD_GROUPS

    def conv_chunk(c, carry):
        r0 = pl.multiple_of(c * q, q)
        for src, dst, cw, cb in ((xs_ref, xs_s, cwx_ref, cbx_ref), (bc_ref, bc_s, cwb_ref, cbb_ref)):
            x = src[pl.ds(r0, q), :]
            xp = src[pl.ds(jnp.maximum(r0 - 1, 0), 1), :]
            xn = src[pl.ds(jnp.minimum(r0 + q, seq - 1), 1), :]
            xp = jnp.where(c == 0, 0.0, xp)
            xn = jnp.where(c == nc - 1, 0.0, xn)
            rows = lax.broadcasted_iota(jnp.int32, x.shape, 0)
            prev = jnp.where(rows == 0, xp, pltpu.roll(x, 1, 0))
            nxt = jnp.where(rows == q - 1, xn, pltpu.roll(x, q - 1, 0))
            y = cw[0:1, :] * prev + cw[1:2, :] * x + cw[2:3, :] * nxt + cb[...]
            dst[pl.ds(r0, q), :] = _silu(y)
        return carry

    lax.fori_loop(0, nc, conv_chunk, 0)

    a_vec = -jnp.exp(alog_ref[...])
    dt_bias = dtb_ref[...]
    ri = lax.broadcasted_iota(jnp.int32, (q, q), 0)
    ci = lax.broadcasted_iota(jnp.int32, (q, q), 1)

    for d in (0, 1):
        tri = (ri >= ci) if d == 0 else (ci >= ri)
        tri_b = jnp.where(tri, 1.0, 0.0).astype(BF16)
        for g in range(SSD_GROUPS):
            if has_h0:
                st_s[g] = h0_ref[d, g * half:(g + 1) * half, :].T
            else:
                st_s[g] = jnp.zeros((SSD_STATE, half), F32)

        def chunk(i, carry, d=d, tri=tri, tri_b=tri_b):
            c = i if d == 0 else nc - 1 - i
            r0 = pl.multiple_of(c * q, q)
            x_dt = dt_ref[pl.ds(r0, q), :] + dt_bias
            dtc = jnp.maximum(x_dt, 0.0) + jnp.log1p(jnp.exp(-jnp.abs(x_dt)))
            la = dtc * a_vec
            cum = _sel_left(tri_b, la)
            cum_t = cum.T
            e = e_ref[d]
            dt_x = _sel_right(dtc, e)
            cum_x = _sel_right(cum, e)
            last = q - 1 if d == 0 else 0
            cl = cum_x[last:last + 1, :]
            xd = xs_s[pl.ds(r0, q), :] * dt_x
            xdb = xd.astype(BF16)
            xdd = (xd * jnp.exp(cl - cum_x)).astype(BF16)
            ecum = jnp.exp(cum_x)
            bcv = bc_s[pl.ds(r0, q), :]
            for g in range(SSD_GROUPS):
                bg = bcv[:, g * SSD_STATE:(g + 1) * SSD_STATE]
                cg = bcv[:, (SSD_GROUPS + g) * SSD_STATE:(SSD_GROUPS + g + 1) * SSD_STATE]
                bgb = bg.astype(BF16)
                cgb = cg.astype(BF16)
                gm = _bdot_nt(cgb, bgb)
                st = st_s[g]
                y_off = _bdot(cgb, st.astype(BF16)) * ecum[:, g * half:(g + 1) * half]
                ys = []
                for hh in range(SSD_HEADS // SSD_GROUPS):
                    h = g * (SSD_HEADS // SSD_GROUPS) + hh
                    k = d * SSD_HEADS + h
                    decay = jnp.where(tri, jnp.exp(cum[:, k:k + 1] - cum_t[k:k + 1, :]), 0.0)
                    m = (gm * decay).astype(BF16)
                    ys.append(_bdot(m, xdb[:, h * SSD_HEAD_DIM:(h + 1) * SSD_HEAD_DIM]))
                yg = jnp.concatenate(ys, axis=1) + y_off
                if d == 0:
                    ya_s[pl.ds(r0, q), g * half:(g + 1) * half] = yg
                else:
                    ya_s[pl.ds(r0, q), g * half:(g + 1) * half] += yg
                st_s[g] = (st * jnp.exp(cl[:, g * half:(g + 1) * half])
                           + _bdot(bg.T.astype(BF16), xdd[:, g * half:(g + 1) * half]))
            return carry

        lax.fori_loop(0, nc, chunk, 0)
        if has_state_out:
            for g in range(SSD_GROUPS):
                st_ref[d, g * half:(g + 1) * half, :] = st_s[g].T

    def fin_chunk(c, carry):
        r0 = pl.multiple_of(c * q, q)
        y = ya_s[pl.ds(r0, q), :] + dexp_ref[...] * xs_s[pl.ds(r0, q), :]
        y = y * _silu(z_ref[pl.ds(r0, q), :])
        for g in range(SSD_GROUPS):
            v = y[:, g * half:(g + 1) * half]
            ms = jnp.mean(v * v, axis=-1, keepdims=True)
            out = v * lax.rsqrt(ms + EPS) * nw_ref[:, g * half:(g + 1) * half]
            y_ref[pl.ds(r0, q), g * half:(g + 1) * half] = out.astype(BF16)
        return carry

    lax.fori_loop(0, nc, fin_chunk, 0)


def _ssd(u1, nb, seq, pw, h0, layer, state_prev):
    has_h0 = h0 is not None
    has_state_out = state_prev is not None
    c1 = lambda b: (0, 0)
    in_specs = [
        pl.BlockSpec((seq, SSD_INNER), lambda b: (b, COL_Z // SSD_INNER)),
        pl.BlockSpec((seq, SSD_INNER), lambda b: (b, COL_XS // SSD_INNER)),
        pl.BlockSpec((seq, SSD_BC), lambda b: (b, COL_BC // SSD_BC)),
        pl.BlockSpec((seq, LANES), lambda b: (b, COL_DT // LANES)),
        pl.BlockSpec((3, SSD_INNER), c1),
        pl.BlockSpec((1, SSD_INNER), c1),
        pl.BlockSpec((3, SSD_BC), c1),
        pl.BlockSpec((1, SSD_BC), c1),
        pl.BlockSpec((1, LANES), c1),
        pl.BlockSpec((1, LANES), c1),
        pl.BlockSpec((1, SSD_INNER), c1),
        pl.BlockSpec((1, SSD_INNER), c1),
        pl.BlockSpec((2, LANES, SSD_INNER), lambda b: (0, 0, 0)),
    ]
    args = [u1, u1, u1, u1, pw["cw_xs"], pw["cb_xs"], pw["cw_bc"], pw["cb_bc"], pw["dt_bias"], pw["a_log"],
            pw["d_exp"], pw["ssd_nw"], pw["e_heads"]]
    if has_h0:
        in_specs.append(pl.BlockSpec((None, None, 2, SSD_INNER, SSD_STATE), lambda b: (b, layer, 0, 0, 0)))
        args.append(h0)
    out_specs = [pl.BlockSpec((seq, SSD_INNER), lambda b: (b, 0))]
    out_shape = [jax.ShapeDtypeStruct((nb * seq, SSD_INNER), BF16)]
    aliases = {}
    n_alias = 0
    if has_state_out:
        depth, prev = state_prev
        out_specs.append(pl.BlockSpec((None, None, 2, SSD_INNER, SSD_STATE), lambda b: (b, layer, 0, 0, 0)))
        out_shape.append(jax.ShapeDtypeStruct((nb, depth, 2, SSD_INNER, SSD_STATE), F32))
        if prev is not None:
            aliases = {len(args): 1}
            in_specs.append(pl.BlockSpec(memory_space=pl.ANY))
            args.append(prev)
            n_alias = 1
    res = pl.pallas_call(
        functools.partial(_ssd_kernel, seq=seq, has_h0=has_h0, has_state_out=has_state_out, n_alias=n_alias),
        grid=(nb,),
        in_specs=in_specs,
        out_specs=out_specs,
        out_shape=out_shape,
        input_output_aliases=aliases,
        scratch_shapes=[
            pltpu.VMEM((seq, SSD_INNER), F32),
            pltpu.VMEM((seq, SSD_BC), F32),
            pltpu.VMEM((seq, SSD_INNER), F32),
            pltpu.VMEM((SSD_GROUPS, SSD_STATE, SSD_INNER // SSD_GROUPS), F32),
        ],
        compiler_params=_cparams("arbitrary"),
        name="ssd_bidir",
    )(*args)
    return res if has_state_out else (res[0], None)


def _softmax_rows(parts):
    m = functools.reduce(jnp.maximum, [jnp.max(s, axis=-1, keepdims=True) for s in parts])
    es = [jnp.exp(s - m) for s in parts]
    den = functools.reduce(lambda a, b: a + b, [jnp.sum(e, axis=-1, keepdims=True) for e in es])
    inv = 1.0 / den
    return [e * inv for e in es]


def _lambda(lp, lam_init):
    a = jnp.sum(lp[0:1, :] * lp[1:2, :], axis=-1, keepdims=True)
    b = jnp.sum(lp[2:3, :] * lp[3:4, :], axis=-1, keepdims=True)
    return jnp.exp(a) - jnp.exp(b) + lam_init


def _subln(o, w, lam_init):
    ms = jnp.mean(o * o, axis=-1, keepdims=True)
    return o * lax.rsqrt(ms + EPS) * w * (1.0 - lam_init)


def _rope(x, cos, sin):
    lane = lax.broadcasted_iota(jnp.int32, x.shape, 1)
    swapped = jnp.where((lane & 16) == 0, pltpu.roll(x, LANES - 16, 1), pltpu.roll(x, 16, 1))
    return x * cos + swapped * sin


def _ctx_diff_kernel(q_ref, k_ref, v_ref, lamp_ref, sw_ref, o_ref, *, lam_init):
    scale = DIFF_HEAD_DIM ** -0.5
    q = q_ref[...]
    k = k_ref[...]
    ps = []
    for t in (0, 1):
        sl = slice(t * DIFF_HEAD_DIM, (t + 1) * DIFF_HEAD_DIM)
        s = _bdot_nt(q[:, sl].astype(BF16), k[:, sl].astype(BF16)) * scale
        ps.append(_softmax_rows([s])[0])
    lam = _lambda(lamp_ref[...], lam_init)
    att = (ps[0] - lam * ps[1]).astype(BF16)
    o = _subln(_bdot(att, v_ref[...].astype(BF16)), sw_ref[...], lam_init)
    o_ref[...] = o.astype(BF16)


def _ctx_softmax_kernel(q_ref, k_ref, v_ref, o_ref):
    scale = NA_HEAD_DIM ** -0.5
    q = q_ref[...]
    k = k_ref[...]
    v = v_ref[...].astype(BF16)
    outs = []
    for t in (0, 1):
        sl = slice(t * NA_HEAD_DIM, (t + 1) * NA_HEAD_DIM)
        s = _bdot_nt(q[:, sl].astype(BF16), k[:, sl].astype(BF16)) * scale
        outs.append(_bdot(_softmax_rows([s])[0].astype(BF16), v[:, sl]))
    o_ref[...] = jnp.concatenate(outs, axis=1).astype(BF16)


def _ctx_attn(q, k_all, v_all, nb, seq, layer, extra, kern, name):
    n_blocks = HEAD_COLS // LANES
    kv_spec = pl.BlockSpec((None, None, seq, LANES), lambda b, h: (b, layer, 0, h))
    extra_specs = [pl.BlockSpec(a.shape, lambda b, h: (0, 0)) for a in extra]
    return pl.pallas_call(
        kern,
        grid=(nb, n_blocks),
        in_specs=[pl.BlockSpec((seq, LANES), lambda b, h: (b, h)), kv_spec, kv_spec] + extra_specs,
        out_specs=pl.BlockSpec((seq, LANES), lambda b, h: (b, h)),
        out_shape=jax.ShapeDtypeStruct((nb * seq, HEAD_COLS), BF16),
        compiler_params=_cparams("arbitrary", "arbitrary"),
        name=name,
    )(q, k_all, v_all, *extra)


def _lat_diff_kernel(q_ref, k_ref, v_ref, ck_ref, cv_ref, cosq_ref, sinq_ref, cosk_ref, sink_ref,
                     lamp_ref, sw_ref, o_ref, *, lam_init):
    scale = DIFF_HEAD_DIM ** -0.5
    q = _rope(q_ref[...], cosq_ref[...], sinq_ref[...])
    k = _rope(k_ref[...], cosk_ref[...], sink_ref[...])
    ck = ck_ref[...]
    v = v_ref[...].astype(BF16)
    cv = cv_ref[...].astype(BF16)
    pl_, pc_ = [], []
    for t in (0, 1):
        sl = slice(t * DIFF_HEAD_DIM, (t + 1) * DIFF_HEAD_DIM)
        qt = q[:, sl].astype(BF16)
        s_loc = _bdot_nt(qt, k[:, sl].astype(BF16)) * scale
        s_ctx = _bdot_nt(qt, ck[:, sl].astype(BF16)) * scale
        p_loc, p_ctx = _softmax_rows([s_loc, s_ctx])
        pl_.append(p_loc)
        pc_.append(p_ctx)
    lam = _lambda(lamp_ref[...], lam_init)
    a_loc = (pl_[0] - lam * pl_[1]).astype(BF16)
    a_ctx = (pc_[0] - lam * pc_[1]).astype(BF16)
    o = _bdot(a_loc, v) + _bdot(a_ctx, cv)
    o_ref[...] = _subln(o, sw_ref[...], lam_init).astype(BF16)


def _lat_diff(q, k, v, nb, seq, cache_k, cache_v, layer, cos, sin, lamp, sw, lam_init):
    tq = 256
    nq = seq // tq
    past = cache_k.shape[2]
    cache_spec = pl.BlockSpec((None, None, past, LANES), lambda b, h, i: (b, layer, 0, h))
    loc_spec = pl.BlockSpec((seq, LANES), lambda b, h, i: (b, h))
    return pl.pallas_call(
        functools.partial(_lat_diff_kernel, lam_init=lam_init),
        grid=(nb, DIFF_HEADS, nq),
        in_specs=[
            pl.BlockSpec((tq, LANES), lambda b, h, i: (b * nq + i, h)),
            loc_spec, loc_spec, cache_spec, cache_spec,
            pl.BlockSpec((tq, LANES), lambda b, h, i: (i, 0)),
            pl.BlockSpec((tq, LANES), lambda b, h, i: (i, 0)),
            pl.BlockSpec((seq, LANES), lambda b, h, i: (0, 0)),
            pl.BlockSpec((seq, LANES), lambda b, h, i: (0, 0)),
            pl.BlockSpec((4, DIFF_HEAD_DIM), lambda b, h, i: (0, 0)),
            pl.BlockSpec((1, LANES), lambda b, h, i: (0, 0)),
        ],
        out_specs=pl.BlockSpec((tq, LANES), lambda b, h, i: (b * nq + i, h)),
        out_shape=jax.ShapeDtypeStruct((nb * seq, HEAD_COLS), BF16),
        compiler_params=_cparams("arbitrary", "arbitrary", "arbitrary"),
        name="lat_diff_attn",
    )(q, k, v, cache_k, cache_v, cos, sin, cos, sin, lamp, sw)


def _lat_na_kernel(q_ref, k_ref, v_ref, ck_ref, cv_ref, tb_ref, o_ref, *, rows):
    scale = NA_HEAD_DIM ** -0.5
    kh = min(NA_KH, rows)
    for t in (0, 1):
        sl = slice(t * NA_HEAD_DIM, (t + 1) * NA_HEAD_DIM)
        qh = q_ref[:, sl].astype(BF16)
        kh_all = k_ref[:, sl].astype(BF16)
        vh_all = v_ref[:, sl].astype(BF16)
        ckh = ck_ref[:, sl].astype(BF16)
        cvh = cv_ref[:, sl].astype(BF16)
        for qr in range(rows):
            r0 = min(max(qr - kh // 2, 0), rows - kh)
            a0 = r0 - qr + NA_KH - 1
            qb = qh[qr * GRID_W:(qr + 1) * GRID_W]
            kw = kh_all[r0 * GRID_W:(r0 + kh) * GRID_W]
            vw = vh_all[r0 * GRID_W:(r0 + kh) * GRID_W]
            bias = tb_ref[t, :, a0 * GRID_W:(a0 + kh) * GRID_W]
            s_loc = _bdot_nt(qb, kw) * scale + bias
            s_ctx = _bdot_nt(qb, ckh) * scale
            p_loc, p_ctx = _softmax_rows([s_loc, s_ctx])
            o = _bdot(p_loc.astype(BF16), vw) + _bdot(p_ctx.astype(BF16), cvh)
            o_ref[qr * GRID_W:(qr + 1) * GRID_W, sl] = o.astype(BF16)


def _lat_na(q, k, v, nb, seq, cache_k, cache_v, layer, table):
    past = cache_k.shape[2]
    rows = seq // GRID_W
    ncol = table.shape[-1]
    loc_spec = pl.BlockSpec((seq, LANES), lambda b, h: (b, h))
    cache_spec = pl.BlockSpec((None, None, past, LANES), lambda b, h: (b, layer, 0, h))
    return pl.pallas_call(
        functools.partial(_lat_na_kernel, rows=rows),
        grid=(nb, NA_HEADS // 2),
        in_specs=[loc_spec, loc_spec, loc_spec, cache_spec, cache_spec,
                  pl.BlockSpec((2, GRID_W, ncol), lambda b, h: (h, 0, 0))],
        out_specs=pl.BlockSpec((seq, LANES), lambda b, h: (b, h)),
        out_shape=jax.ShapeDtypeStruct((nb * seq, HEAD_COLS), BF16),
        compiler_params=_cparams("arbitrary", "arbitrary"),
        name="lat_nbr_attn",
    )(q, k, v, cache_k, cache_v, table)


def _merge_kernel(ya_ref, yb_ref, yc_ref, wa_ref, wb_ref, wc_ref, ga_ref, gb_ref, gc_ref, o_ref):
    ya, yb, yc = ya_ref[...], yb_ref[...], yc_ref[...]
    for s in range(D_MODEL // SUB):
        cols = slice(s * SUB, (s + 1) * SUB)
        m = (ga_ref[:, cols] * _bdot(ya, wa_ref[:, cols])
             + gb_ref[:, cols] * _bdot(yb, wb_ref[:, cols])
             + gc_ref[:, cols] * _bdot(yc, wc_ref[:, cols]))
        o_ref[:, cols] = m.astype(BF16)


def _merge(ya, yb, yc, wa, wb, wc, u1):
    t, kk = ya.shape
    n = wa.shape[1]
    tm = TM_MERGE
    ysp = pl.BlockSpec((tm, kk), lambda i: (i, 0))
    return pl.pallas_call(
        _merge_kernel,
        grid=(t // tm,),
        in_specs=[ysp, ysp, ysp, _resident((kk, n)), _resident((kk, n)), _resident((kk, n)),
                  pl.BlockSpec((tm, n), lambda i: (i, 0)),
                  pl.BlockSpec((tm, n), lambda i: (i, 1)),
                  pl.BlockSpec((tm, n), lambda i: (i, 2))],
        out_specs=pl.BlockSpec((tm, n), lambda i: (i, 0)),
        out_shape=jax.ShapeDtypeStruct((t, n), BF16),
        compiler_params=_cparams("arbitrary"),
        name="branch_merge",
    )(ya, yb, yc, wa, wb, wc, u1, u1, u1)


def _resid_mm_kernel(*refs, gate_row, norm_rows):
    if norm_rows is None:
        a_ref, w_ref, x_ref, mod_ref, xo_ref = refs
    else:
        a_ref, w_ref, x_ref, mod_ref, nw_ref, nmod_ref, xo_ref, h_ref = refs
    a = a_ref[...]
    for s in range(D_MODEL // SUB):
        cols = slice(s * SUB, (s + 1) * SUB)
        xo_ref[:, cols] = x_ref[:, cols] + mod_ref[0, gate_row:gate_row + 1, cols] * _bdot(a, w_ref[:, cols])
    if norm_rows is not None:
        shift_row, scale_row = norm_rows
        h = _modulated_norm(xo_ref[...], nw_ref[...], nmod_ref[0, shift_row:shift_row + 1, :],
                            nmod_ref[0, scale_row:scale_row + 1, :])
        h_ref[...] = h.astype(BF16)


def _resid_mm(a, w, x2d, mod, gate_row, tm, mod_row, norm, name):
    t, kk = a.shape
    d = w.shape[1]
    row = lambda i: (i, 0)
    mrow = lambda i: (mod_row(i, tm), 0, 0)
    in_specs = [pl.BlockSpec((tm, kk), row), _resident((kk, d)), pl.BlockSpec((tm, d), row),
                pl.BlockSpec((1, 6, d), mrow)]
    args = [a, w, x2d, mod]
    out_specs = [pl.BlockSpec((tm, d), row)]
    out_shape = [jax.ShapeDtypeStruct((t, d), F32)]
    norm_rows = None
    if norm is not None:
        nw, nmod, shift_row, scale_row = norm
        norm_rows = (shift_row, scale_row)
        in_specs += [pl.BlockSpec((1, d), lambda i: (0, 0)), pl.BlockSpec((1, 6, d), mrow)]
        args += [nw, nmod]
        out_specs.append(pl.BlockSpec((tm, d), row))
        out_shape.append(jax.ShapeDtypeStruct((t, d), BF16))
    res = pl.pallas_call(
        functools.partial(_resid_mm_kernel, gate_row=gate_row, norm_rows=norm_rows),
        grid=(t // tm,),
        in_specs=in_specs,
        out_specs=out_specs,
        out_shape=out_shape,
        compiler_params=_cparams("arbitrary"),
        name=name,
    )(*args)
    return (res[0], res[1]) if norm is not None else (res[0], None)


def _ffn_up_kernel(h_ref, wv_ref, wg_ref, cwv_ref, cwg_ref, cbv_ref, cbg_ref, o_ref, *, seq):
    h = h_ref[...]
    tm = h.shape[0]
    pos = lax.broadcasted_iota(jnp.int32, (tm, SUB), 0) % seq

    def conv(acc, cw, cb):
        prev = jnp.where(pos == 0, 0.0, pltpu.roll(acc, 1, 0))
        nxt = jnp.where(pos == seq - 1, 0.0, pltpu.roll(acc, tm - 1, 0))
        return cw[0:1, :] * prev + cw[1:2, :] * acc + cw[2:3, :] * nxt + cb[...]

    val = conv(_bdot(h, wv_ref[...]), cwv_ref, cbv_ref)
    gt = conv(_bdot(h, wg_ref[...]), cwg_ref, cbg_ref)
    o_ref[...] = (_silu(gt) * val).astype(BF16)


def _ffn_up(h, w_up, cw, cb, seq):
    t, d = h.shape
    tm = max(TM_UP, seq)
    assert tm % seq == 0
    nt = D_FF // SUB
    return pl.pallas_call(
        functools.partial(_ffn_up_kernel, seq=seq),
        grid=(nt, t // tm),
        in_specs=[
            pl.BlockSpec((tm, d), lambda j, i: (i, 0)),
            pl.BlockSpec((d, SUB), lambda j, i: (0, j)),
            pl.BlockSpec((d, SUB), lambda j, i: (0, nt + j)),
            pl.BlockSpec((3, SUB), lambda j, i: (0, j)),
            pl.BlockSpec((3, SUB), lambda j, i: (0, nt + j)),
            pl.BlockSpec((1, SUB), lambda j, i: (0, j)),
            pl.BlockSpec((1, SUB), lambda j, i: (0, nt + j)),
        ],
        out_specs=pl.BlockSpec((tm, SUB), lambda j, i: (i, j)),
        out_shape=jax.ShapeDtypeStruct((t, D_FF), BF16),
        compiler_params=_cparams("arbitrary", "arbitrary"),
        name="ffn_up_conv",
    )(h, w_up, w_up, cw, cw, cb, cb)


def _rope_tables(seq):
    t = np.arange(seq)
    row = (t // GRID_W).astype(np.float32)
    col = (t % GRID_W).astype(np.float32)
    n_freq = DIFF_HEAD_DIM // 4
    inv = jnp.asarray(ROPE_BASE, F32) ** (-jnp.arange(n_freq, dtype=F32) / n_freq)
    ang_r = jnp.asarray(row)[:, None] * inv
    ang_c = jnp.asarray(col)[:, None] * inv
    cr, sr, cc, sc = jnp.cos(ang_r), jnp.sin(ang_r), jnp.cos(ang_c), jnp.sin(ang_c)
    cos64 = jnp.concatenate([cr, cr, cc, cc], axis=-1)
    sin64 = jnp.concatenate([-sr, sr, -sc, sc], axis=-1)
    return jnp.tile(cos64, (1, 2)), jnp.tile(sin64, (1, 2))


def _na_bias_table(rpb):
    lead = rpb.shape[:-1]
    period = 2 * GRID_W
    n_side = NA_KW - 1
    r = rpb.astype(F32)
    base = jnp.concatenate([r[..., n_side:], jnp.zeros(lead + (period - 2 * n_side - 1,), F32),
                            r[..., :n_side]], axis=-1)
    flat = jnp.tile(base, (1,) * len(lead) + (GRID_W,))[..., :GRID_W * (period - 1)]
    toe = flat.reshape(lead + (GRID_W, period - 1))[..., :GRID_W]
    qc = np.arange(GRID_W)[:, None]
    kc = np.arange(GRID_W)[None, :]
    wstart = np.clip(qc - NA_KW // 2, 0, GRID_W - NA_KW)
    valid = (kc >= wstart) & (kc < wstart + NA_KW)
    tbl = jnp.where(valid, toe, -jnp.inf)
    tbl = jnp.swapaxes(tbl, -3, -2)
    return tbl.reshape(tbl.shape[:-2] + (-1,))


def _head_expand():
    e = np.zeros((2, LANES, SSD_INNER), np.float32)
    for d in range(2):
        for h in range(SSD_HEADS):
            e[d, d * SSD_HEADS + h, h * SSD_HEAD_DIM:(h + 1) * SSD_HEAD_DIM] = 1.0
    return jnp.asarray(e, BF16)


def _seg_ones():
    i = np.arange(LANES)
    return jnp.asarray((i[:, None] // 64 == i[None, :] // 64).astype(np.float32), BF16)


def _pad_lanes(v, n):
    return jnp.pad(v.reshape(1, -1), ((0, 0), (0, n - v.size)))


def _layer_params(l, norm1_w, norm2_w, w_in, ssd_conv_w, ssd_conv_b, ssd_dt_bias, ssd_a_log, ssd_d, ssd_norm_w,
                  diff_q_norm, diff_k_norm, diff_lam, diff_subln_w, na_q_norm, na_k_norm,
                  w_branch_a, w_branch_b, w_branch_c, w_out, ffn_w_up, ffn_conv_w, ffn_conv_b, ffn_w_down):
    w = w_in[l]
    o = np.cumsum([0, 1024, 1536, 32, 3 * HEAD_COLS, 3 * HEAD_COLS, 6144])
    z, xbc, dt, wd, wn, g = [w[:, o[i]:o[i + 1]].astype(BF16) for i in range(6)]
    pad = jnp.zeros((w.shape[0], P1_COLS - COL_DT - 32), BF16)
    rep = HEAD_COLS // DIFF_HEAD_DIM
    return {
        "norm1_w": norm1_w[l].reshape(1, -1), "norm2_w": norm2_w[l].reshape(1, -1),
        "w1": jnp.concatenate([g, z, xbc, dt, pad], axis=1), "w_diff": wd, "w_na": wn,
        "diff_qn": jnp.tile(diff_q_norm[l], rep).reshape(1, -1), "diff_kn": jnp.tile(diff_k_norm[l], rep).reshape(1, -1),
        "na_qn": jnp.tile(na_q_norm[l], rep).reshape(1, -1), "na_kn": jnp.tile(na_k_norm[l], rep).reshape(1, -1),
        "cw_xs": ssd_conv_w[l][:, :SSD_INNER], "cw_bc": ssd_conv_w[l][:, SSD_INNER:],
        "cb_xs": ssd_conv_b[l][:SSD_INNER].reshape(1, -1), "cb_bc": ssd_conv_b[l][SSD_INNER:].reshape(1, -1),
        "dt_bias": _pad_lanes(ssd_dt_bias[l], LANES), "a_log": _pad_lanes(ssd_a_log[l], LANES),
        "d_exp": jnp.repeat(ssd_d[l], SSD_HEAD_DIM).reshape(1, -1), "ssd_nw": ssd_norm_w[l].reshape(1, -1),
        "e_heads": _head_expand(),
        "diff_lam": diff_lam[l], "subln_w": diff_subln_w[l].reshape(1, -1),
        "w_a": w_branch_a[l].astype(BF16), "w_b": w_branch_b[l].astype(BF16), "w_c": w_branch_c[l].astype(BF16),
        "w_out": w_out[l].astype(BF16), "w_up": ffn_w_up[l].astype(BF16), "w_down": ffn_w_down[l].astype(BF16),
        "ffn_cw": ffn_conv_w[l], "ffn_cb": ffn_conv_b[l].reshape(1, -1),
    }


def _trunk_layer(x2d, h1, nb, seq, mod, mod_row, pw, layer, ctx_out, cache, consts, next_norm):
    u1 = _proj1(h1, pw["w1"])
    lam_init = 0.8 - 0.6 * math.exp(-0.3 * layer)
    bd = consts["bd"]
    if cache is None:
        depth, dk, dv, nk, nv, st = ctx_out
        qd, dk, dv = _qkv(h1, pw["w_diff"], pw["diff_qn"], pw["diff_kn"], bd, nb, seq, layer, (depth, dk, dv),
                          "proj_diff_qkv")
        qn, nk, nv = _qkv(h1, pw["w_na"], pw["na_qn"], pw["na_kn"], bd, nb, seq, layer, (depth, nk, nv),
                          "proj_na_qkv")
        y_a, st = _ssd(u1, nb, seq, pw, None, layer, (depth, st))
        y_b = _ctx_attn(qd, dk, dv, nb, seq, layer, [pw["diff_lam"], pw["subln_w"]],
                        functools.partial(_ctx_diff_kernel, lam_init=lam_init), "ctx_diff_attn")
        y_c = _ctx_attn(qn, nk, nv, nb, seq, layer, [], _ctx_softmax_kernel, "ctx_softmax_attn")
        ctx_out = (depth, dk, dv, nk, nv, st)
    else:
        cdk, cdv, cnk, cnv, h0 = cache
        qd, kd, vd = _qkv(h1, pw["w_diff"], pw["diff_qn"], pw["diff_kn"], bd, nb, seq, layer, None,
                          "proj_diff_qkv")
        qn, kn, vn = _qkv(h1, pw["w_na"], pw["na_qn"], pw["na_kn"], bd, nb, seq, layer, None, "proj_na_qkv")
        y_a, _ = _ssd(u1, nb, seq, pw, h0, layer, None)
        y_b = _lat_diff(qd, kd, vd, nb, seq, cdk, cdv, layer, consts["cos"], consts["sin"], pw["diff_lam"],
                        pw["subln_w"], lam_init)
        y_c = _lat_na(qn, kn, vn, nb, seq, cnk, cnv, layer, pw["na_table"])
    merged = _merge(y_a, y_b, y_c, pw["w_a"], pw["w_b"], pw["w_c"], u1)
    x2d, h2 = _resid_mm(merged, pw["w_out"], x2d, mod, 2, TM_OUT, mod_row, (pw["norm2_w"], mod, 3, 4), "out_proj")
    act = _ffn_up(h2, pw["w_up"], pw["ffn_cw"], pw["ffn_cb"], seq)
    x2d, h1_next = _resid_mm(act, pw["w_down"], x2d, mod, 5, TM_DOWN, mod_row, next_norm, "ffn_down")
    return x2d, h1_next, ctx_out


def kernel(x_prompt, x_sample, c, cache_diff_k, cache_diff_v, cache_na_k, cache_na_v, state_ssm, c_ctx, norm1_w, norm2_w, w_ada, b_ada, w_in, ssd_conv_w, ssd_conv_b, ssd_dt_bias, ssd_a_log, ssd_d, ssd_norm_w, diff_q_norm, diff_k_norm, diff_lam, diff_subln_w, na_q_norm, na_k_norm, na_rpb, w_branch_a, w_branch_b, w_branch_c, w_out, ffn_w_up, ffn_conv_w, ffn_conv_b, ffn_w_down):
    batch, seq, d = x_prompt.shape
    dec_batch, dec_seq, _ = x_sample.shape
    depth = w_in.shape[0]
    past = cache_diff_k.shape[2]
    assert d == D_MODEL and dec_batch + 1 <= 8 and seq % SSD_CHUNK == 0 and dec_seq % (NA_KH * GRID_W) == 0

    c_rows = jnp.concatenate([c_ctx.reshape(1, d), c, jnp.zeros((8 - 1 - dec_batch, d), F32)], axis=0)
    mod_all = _ada(c_rows, w_ada, b_ada).reshape(depth, 8, 6, d)

    cos, sin = _rope_tables(dec_seq)
    consts = {"bd": _seg_ones(), "cos": cos, "sin": sin}
    na_tables = _na_bias_table(na_rpb)
    cdk = cache_diff_k.reshape(dec_batch, depth, past, HEAD_COLS)
    cdv = cache_diff_v.reshape(dec_batch, depth, past, HEAD_COLS)
    cnk = cache_na_k.reshape(dec_batch, depth, past, HEAD_COLS)
    cnv = cache_na_v.reshape(dec_batch, depth, past, HEAD_COLS)
    h0 = state_ssm.reshape(dec_batch, depth, 2, SSD_INNER, SSD_STATE)

    pws = []
    for l in range(depth):
        pw = _layer_params(l, norm1_w, norm2_w, w_in, ssd_conv_w, ssd_conv_b, ssd_dt_bias, ssd_a_log, ssd_d,
                           ssd_norm_w, diff_q_norm, diff_k_norm, diff_lam, diff_subln_w, na_q_norm, na_k_norm,
                           w_branch_a, w_branch_b, w_branch_c, w_out, ffn_w_up, ffn_conv_w, ffn_conv_b, ffn_w_down)
        pw["na_table"] = na_tables[l]
        pws.append(pw)

    ctx_row = lambda i, tm: 0
    lat_row = lambda i, tm: 1 + (i * tm) // dec_seq
    y_p = x_prompt.reshape(batch * seq, d)
    y_s = x_sample.reshape(dec_batch * dec_seq, d)
    h_p = _norm(y_p, pws[0]["norm1_w"], mod_all[0], ctx_row)
    h_s = _norm(y_s, pws[0]["norm1_w"], mod_all[0], lat_row)
    ctx_out = (depth, None, None, None, None, None)
    cache = (cdk, cdv, cnk, cnv, h0)
    for l in range(depth):
        next_norm = (pws[l + 1]["norm1_w"], mod_all[l + 1], 0, 1) if l + 1 < depth else None
        y_p, h_p, ctx_out = _trunk_layer(y_p, h_p, batch, seq, mod_all[l], ctx_row, pws[l], l, ctx_out, None,
                                         consts, next_norm)
        y_s, h_s, _ = _trunk_layer(y_s, h_s, dec_batch, dec_seq, mod_all[l], lat_row, pws[l], l, None, cache,
                                   consts, next_norm)
    _, dk, dv, nk, nv, st = ctx_out
    return (y_p.reshape(batch, seq, d), y_s.reshape(dec_batch, dec_seq, d),
            dk.reshape(batch, depth, seq, DIFF_HEADS, 2, DIFF_HEAD_DIM),
            dv.reshape(batch, depth, seq, DIFF_HEADS, 2 * DIFF_HEAD_DIM),
            nk.reshape(batch, depth, seq, NA_HEADS, NA_HEAD_DIM),
            nv.reshape(batch, depth, seq, NA_HEADS, NA_HEAD_DIM),
            st.reshape(batch, depth, 2, SSD_HEADS, SSD_HEAD_DIM, SSD_STATE))
```

```python
import functools
import math

import numpy as np
import jax
import jax.numpy as jnp
from jax import lax
from jax.experimental import pallas as pl
from jax.experimental.pallas import tpu as pltpu

F32 = jnp.float32
BF16 = jnp.bfloat16

D_MODEL = 2048
GRID_W = 64
SSD_INNER = 1024
SSD_HEAD_DIM = 64
SSD_HEADS = 16
SSD_GROUPS = 2
SSD_STATE = 128
SSD_CHUNK = 128
SSD_BC = 2 * SSD_GROUPS * SSD_STATE
DIFF_HEADS = 8
DIFF_HEAD_DIM = 64
NA_HEADS = 16
NA_HEAD_DIM = 64
NA_KH = 8
NA_KW = 16
HEAD_COLS = 1024
D_FF = 5632
ROPE_BASE = 10000.0
EPS = 1e-6

LANES = 128
VMEM_LIMIT_BYTES = 56 * 1024 * 1024
SUB = 512

COL_G = 0
COL_Z = 6144
COL_XS = 7168
COL_BC = 8192
COL_DT = 8704
P1_COLS = 9216
P1_CHUNK = 3072
N_GATE_CHUNKS = COL_Z // P1_CHUNK

TM_NORM = 1024
TM_PROJ = 512
TM_MERGE = 512
TM_OUT = 512
TM_UP = 1024
TM_DOWN = 256


def _cparams(*sem):
    return pltpu.CompilerParams(dimension_semantics=sem, vmem_limit_bytes=VMEM_LIMIT_BYTES)


def _resident(shape):
    nd = len(shape)
    return pl.BlockSpec(shape, lambda *_: (0,) * nd, pipeline_mode=pl.Buffered(1))


def _sigmoid(x):
    return 1.0 / (1.0 + jnp.exp(-x))


def _silu(x):
    return x * _sigmoid(x)


def _bdot(a, b):
    return jnp.dot(a, b, preferred_element_type=F32)


def _bdot_nt(a, b):
    return lax.dot_general(a, b, (((1,), (1,)), ((), ())), preferred_element_type=F32)


def _split3(x):
    p1 = x.astype(BF16)
    r1 = x - p1.astype(F32)
    p2 = r1.astype(BF16)
    p3 = (r1 - p2.astype(F32)).astype(BF16)
    return p1, p2, p3


def _sel_right(x, e):
    p1, p2, p3 = _split3(x)
    return _bdot(p1, e) + _bdot(p2, e) + _bdot(p3, e)


def _sel_left(t, x):
    p1, p2, p3 = _split3(x)
    return _bdot(t, p1) + _bdot(t, p2) + _bdot(t, p3)


def _seg64_rms(a, bd):
    sq = a * a
    hi = sq.astype(BF16)
    lo = (sq - hi.astype(F32)).astype(BF16)
    ss = _bdot(hi, bd) + _bdot(lo, bd)
    return a * lax.rsqrt(ss * (1.0 / 64.0) + EPS)


def _modulated_norm(x, nw, shift, scale):
    ms = jnp.mean(x * x, axis=-1, keepdims=True)
    y = x * lax.rsqrt(ms + EPS) * nw
    return y * (1.0 + scale) + shift


def _ada_kernel(c_ref, w_ref, b_ref, o_ref):
    c = c_ref[...]
    s = _silu(c).astype(BF16)
    o_ref[0] = _bdot(s, w_ref[0].astype(BF16)) + b_ref[0]


def _ada(c_rows, w_ada, b_ada):
    depth, d, n = w_ada.shape
    tn = 1024
    return pl.pallas_call(
        _ada_kernel,
        grid=(depth, n // tn),
        in_specs=[
            pl.BlockSpec((8, d), lambda l, j: (0, 0)),
            pl.BlockSpec((1, d, tn), lambda l, j: (l, 0, j)),
            pl.BlockSpec((1, 1, tn), lambda l, j: (l, 0, j)),
        ],
        out_specs=pl.BlockSpec((1, 8, tn), lambda l, j: (l, 0, j)),
        out_shape=jax.ShapeDtypeStruct((depth, 8, n), F32),
        compiler_params=_cparams("arbitrary", "arbitrary"),
        name="ada_mod",
    )(c_rows, w_ada, b_ada.reshape(depth, 1, n))


def _norm_kernel(x_ref, nw_ref, mod_ref, h_ref):
    h_ref[...] = _modulated_norm(x_ref[...], nw_ref[...], mod_ref[0, 0:1, :], mod_ref[0, 1:2, :]).astype(BF16)


def _norm(x2d, nw, mod, mod_row):
    t, d = x2d.shape
    tm = TM_NORM
    return pl.pallas_call(
        _norm_kernel,
        grid=(t // tm,),
        in_specs=[
            pl.BlockSpec((tm, d), lambda i: (i, 0)),
            pl.BlockSpec((1, d), lambda i: (0, 0)),
            pl.BlockSpec((1, 6, d), lambda i: (mod_row(i, tm), 0, 0)),
        ],
        out_specs=pl.BlockSpec((tm, d), lambda i: (i, 0)),
        out_shape=jax.ShapeDtypeStruct((t, d), BF16),
        compiler_params=_cparams("arbitrary"),
        name="norm_mod",
    )(x2d, nw, mod)


def _proj1_kernel(h_ref, w_ref, o_ref):
    c = pl.program_id(0)
    h = h_ref[...]

    def tiles(epilogue):
        for s in range(P1_CHUNK // SUB):
            cols = slice(s * SUB, (s + 1) * SUB)
            o_ref[:, cols] = epilogue(_bdot(h, w_ref[:, cols]))

    @pl.when(c < N_GATE_CHUNKS)
    def _():
        tiles(_sigmoid)

    @pl.when(c >= N_GATE_CHUNKS)
    def _():
        tiles(lambda a: a)


def _proj1(h, w1):
    t, d = h.shape
    tm = TM_PROJ
    return pl.pallas_call(
        _proj1_kernel,
        grid=(P1_COLS // P1_CHUNK, t // tm),
        in_specs=[
            pl.BlockSpec((tm, d), lambda c, i: (i, 0)),
            pl.BlockSpec((d, P1_CHUNK), lambda c, i: (0, c)),
        ],
        out_specs=pl.BlockSpec((tm, P1_CHUNK), lambda c, i: (i, c)),
        out_shape=jax.ShapeDtypeStruct((t, P1_COLS), F32),
        compiler_params=_cparams("arbitrary", "arbitrary"),
        name="proj_gates_ssd",
    )(h, w1)


def _qkv_kernel(*refs, n_alias):
    h_ref, w_ref, qnw_ref, knw_ref, bd_ref = refs[:5]
    q_ref, k_ref, v_ref = refs[5 + n_alias:]
    h = h_ref[...]
    bd = bd_ref[...]
    for sec, (o_ref, nw_ref) in enumerate(((q_ref, qnw_ref), (k_ref, knw_ref), (v_ref, None))):
        for s in range(HEAD_COLS // SUB):
            acc = _bdot(h, w_ref[:, sec * HEAD_COLS + s * SUB: sec * HEAD_COLS + (s + 1) * SUB])
            if nw_ref is not None:
                parts = [_seg64_rms(acc[:, c * LANES:(c + 1) * LANES], bd) for c in range(SUB // LANES)]
                acc = jnp.concatenate(parts, axis=1) * nw_ref[:, s * SUB:(s + 1) * SUB]
            o_ref[..., s * SUB:(s + 1) * SUB] = acc.reshape(o_ref.shape[:-1] + (SUB,)).astype(o_ref.dtype)


def _qkv(h, w, qnw, knw, bd, nb, seq, layer, kv_prev, name):
    t, d = h.shape
    tm = TM_PROJ
    in_specs = [
        pl.BlockSpec((tm, d), lambda i: (i, 0)),
        _resident((d, 3 * HEAD_COLS)),
        pl.BlockSpec((1, HEAD_COLS), lambda i: (0, 0)),
        pl.BlockSpec((1, HEAD_COLS), lambda i: (0, 0)),
        pl.BlockSpec((LANES, LANES), lambda i: (0, 0)),
    ]
    args = [h, w, qnw, knw, bd]
    q_spec = pl.BlockSpec((tm, HEAD_COLS), lambda i: (i, 0))
    q_shape = jax.ShapeDtypeStruct((t, HEAD_COLS), F32 if kv_prev is None else BF16)
    aliases = {}
    n_alias = 0
    if kv_prev is None:
        kv_spec, kv_shape = q_spec, q_shape
    else:
        depth, k_prev, v_prev = kv_prev
        assert tm % seq == 0
        kv_spec = pl.BlockSpec((tm // seq, None, seq, HEAD_COLS), lambda i: (i, layer, 0, 0))
        kv_shape = jax.ShapeDtypeStruct((nb, depth, seq, HEAD_COLS), F32)
        if k_prev is not None:
            in_specs += [pl.BlockSpec(memory_space=pl.ANY)] * 2
            args += [k_prev, v_prev]
            aliases = {5: 1, 6: 2}
            n_alias = 2
    return pl.pallas_call(
        functools.partial(_qkv_kernel, n_alias=n_alias),
        grid=(t // tm,),
        in_specs=in_specs,
        out_specs=[q_spec, kv_spec, kv_spec],
        out_shape=[q_shape, kv_shape, kv_shape],
        input_output_aliases=aliases,
        compiler_params=_cparams("arbitrary"),
        name=name,
    )(*args)


def _ssd_kernel(*refs, seq, has_h0, has_state_out, n_alias):
    (z_ref, xs_ref, bc_ref, dt_ref, cwx_ref, cbx_ref, cwb_ref, cbb_ref, dtb_ref, alog_ref,
     dexp_ref, nw_ref, e_ref) = refs[:13]
    rest = list(refs[13:])
    h0_ref = rest.pop(0) if has_h0 else None
    rest = rest[n_alias:]
    y_ref = rest.pop(0)
    st_ref = rest.pop(0) if has_state_out else None
    xs_s, bc_s, ya_s, st_s = rest

    q = SSD_CHUNK
    nc = seq // q
    half = SSD_INNER // SSD_GROUPS

    def conv_chunk(c, carry):
        r0 = pl.multiple_of(c * q, q)
        for src, dst, cw, cb in ((xs_ref, xs_s, cwx_ref, cbx_ref), (bc_ref, bc_s, cwb_ref, cbb_ref)):
            x = src[pl.ds(r0, q), :]
            xp = src[pl.ds(jnp.maximum(r0 - 1, 0), 1), :]
            xn = src[pl.ds(jnp.minimum(r0 + q, seq - 1), 1), :]
            xp = jnp.where(c == 0, 0.0, xp)
            xn = jnp.where(c == nc - 1, 0.0, xn)
            rows = lax.broadcasted_iota(jnp.int32, x.shape, 0)
            prev = jnp.where(rows == 0, xp, pltpu.roll(x, 1, 0))
            nxt = jnp.where(rows == q - 1, xn, pltpu.roll(x, q - 1, 0))
            y = cw[0:1, :] * prev + cw[1:2, :] * x + cw[2:3, :] * nxt + cb[...]
            dst[pl.ds(r0, q), :] = _silu(y)
        return carry

    lax.fori_loop(0, nc, conv_chunk, 0)

    a_vec = -jnp.exp(alog_ref[...])
    dt_bias = dtb_ref[...]
    ri = lax.broadcasted_iota(jnp.int32, (q, q), 0)
    ci = lax.broadcasted_iota(jnp.int32, (q, q), 1)

    for d in (0, 1):
        tri = (ri >= ci) if d == 0 else (ci >= ri)
        tri_b = jnp.where(tri, 1.0, 0.0).astype(BF16)
        for g in range(SSD_GROUPS):
            if has_h0:
                st_s[g] = h0_ref[d, g * half:(g + 1) * half, :].T
            else:
                st_s[g] = jnp.zeros((SSD_STATE, half), F32)

        def chunk(i, carry, d=d, tri=tri, tri_b=tri_b):
            c = i if d == 0 else nc - 1 - i
            r0 = pl.multiple_of(c * q, q)
            x_dt = dt_ref[pl.ds(r0, q), :] + dt_bias
            dtc = jnp.maximum(x_dt, 0.0) + jnp.log1p(jnp.exp(-jnp.abs(x_dt)))
            la = dtc * a_vec
            cum = _sel_left(tri_b, la)
            cum_t = cum.T
            e = e_ref[d]
            dt_x = _sel_right(dtc, e)
            cum_x = _sel_right(cum, e)
            last = q - 1 if d == 0 else 0
            cl = cum_x[last:last + 1, :]
            xd = xs_s[pl.ds(r0, q), :] * dt_x
            xdb = xd.astype(BF16)
            xdd = (xd * jnp.exp(cl - cum_x)).astype(BF16)
            ecum = jnp.exp(cum_x)
            bcv = bc_s[pl.ds(r0, q), :]
            for g in range(SSD_GROUPS):
                bg = bcv[:, g * SSD_STATE:(g + 1) * SSD_STATE]
                cg = bcv[:, (SSD_GROUPS + g) * SSD_STATE:(SSD_GROUPS + g + 1) * SSD_STATE]
                bgb = bg.astype(BF16)
                cgb = cg.astype(BF16)
                gm = _bdot_nt(cgb, bgb)
                st = st_s[g]
                y_off = _bdot(cgb, st.astype(BF16)) * ecum[:, g * half:(g + 1) * half]
                ys = []
                for hh in range(SSD_HEADS // SSD_GROUPS):
                    h = g * (SSD_HEADS // SSD_GROUPS) + hh
                    k = d * SSD_HEADS + h
                    decay = jnp.where(tri, jnp.exp(cum[:, k:k + 1] - cum_t[k:k + 1, :]), 0.0)
                    m = (gm * decay).astype(BF16)
                    ys.append(_bdot(m, xdb[:, h * SSD_HEAD_DIM:(h + 1) * SSD_HEAD_DIM]))
                yg = jnp.concatenate(ys, axis=1) + y_off
                if d == 0:
                    ya_s[pl.ds(r0, q), g * half:(g + 1) * half] = yg
                else:
                    ya_s[pl.ds(r0, q), g * half:(g + 1) * half] += yg
                st_s[g] = (st * jnp.exp(cl[:, g * half:(g + 1) * half])
                           + _bdot(bg.T.astype(BF16), xdd[:, g * half:(g + 1) * half]))
            return carry

        lax.fori_loop(0, nc, chunk, 0)
        if has_state_out:
            for g in range(SSD_GROUPS):
                st_ref[d, g * half:(g + 1) * half, :] = st_s[g].T

    def fin_chunk(c, carry):
        r0 = pl.multiple_of(c * q, q)
        y = ya_s[pl.ds(r0, q), :] + dexp_ref[...] * xs_s[pl.ds(r0, q), :]
        y = y * _silu(z_ref[pl.ds(r0, q), :])
        for g in range(SSD_GROUPS):
            v = y[:, g * half:(g + 1) * half]
            ms = jnp.mean(v * v, axis=-1, keepdims=True)
            out = v * lax.rsqrt(ms + EPS) * nw_ref[:, g * half:(g + 1) * half]
            y_ref[pl.ds(r0, q), g * half:(g + 1) * half] = out.astype(BF16)
        return carry

    lax.fori_loop(0, nc, fin_chunk, 0)


def _ssd(u1, nb, seq, pw, h0, layer, state_prev):
    has_h0 = h0 is not None
    has_state_out = state_prev is not None
    c1 = lambda b: (0, 0)
    in_specs = [
        pl.BlockSpec((seq, SSD_INNER), lambda b: (b, COL_Z // SSD_INNER)),
        pl.BlockSpec((seq, SSD_INNER), lambda b: (b, COL_XS // SSD_INNER)),
        pl.BlockSpec((seq, SSD_BC), lambda b: (b, COL_BC // SSD_BC)),
        pl.BlockSpec((seq, LANES), lambda b: (b, COL_DT // LANES)),
        pl.BlockSpec((3, SSD_INNER), c1),
        pl.BlockSpec((1, SSD_INNER), c1),
        pl.BlockSpec((3, SSD_BC), c1),
        pl.BlockSpec((1, SSD_BC), c1),
        pl.BlockSpec((1, LANES), c1),
        pl.BlockSpec((1, LANES), c1),
        pl.BlockSpec((1, SSD_INNER), c1),
        pl.BlockSpec((1, SSD_INNER), c1),
        pl.BlockSpec((2, LANES, SSD_INNER), lambda b: (0, 0, 0)),
    ]
    args = [u1, u1, u1, u1, pw["cw_xs"], pw["cb_xs"], pw["cw_bc"], pw["cb_bc"], pw["dt_bias"], pw["a_log"],
            pw["d_exp"], pw["ssd_nw"], pw["e_heads"]]
    if has_h0:
        in_specs.append(pl.BlockSpec((None, None, 2, SSD_INNER, SSD_STATE), lambda b: (b, layer, 0, 0, 0)))
        args.append(h0)
    out_specs = [pl.BlockSpec((seq, SSD_INNER), lambda b: (b, 0))]
    out_shape = [jax.ShapeDtypeStruct((nb * seq, SSD_INNER), BF16)]
    aliases = {}
    n_alias = 0
    if has_state_out:
        depth, prev = state_prev
        out_specs.append(pl.BlockSpec((None, None, 2, SSD_INNER, SSD_STATE), lambda b: (b, layer, 0, 0, 0)))
        out_shape.append(jax.ShapeDtypeStruct((nb, depth, 2, SSD_INNER, SSD_STATE), F32))
        if prev is not None:
            aliases = {len(args): 1}
            in_specs.append(pl.BlockSpec(memory_space=pl.ANY))
            args.append(prev)
            n_alias = 1
    res = pl.pallas_call(
        functools.partial(_ssd_kernel, seq=seq, has_h0=has_h0, has_state_out=has_state_out, n_alias=n_alias),
        grid=(nb,),
        in_specs=in_specs,
        out_specs=out_specs,
        out_shape=out_shape,
        input_output_aliases=aliases,
        scratch_shapes=[
            pltpu.VMEM((seq, SSD_INNER), F32),
            pltpu.VMEM((seq, SSD_BC), F32),
            pltpu.VMEM((seq, SSD_INNER), F32),
            pltpu.VMEM((SSD_GROUPS, SSD_STATE, SSD_INNER // SSD_GROUPS), F32),
        ],
        compiler_params=_cparams("arbitrary"),
        name="ssd_bidir",
    )(*args)
    return res if has_state_out else (res[0], None)


def _softmax_rows(parts):
    m = functools.reduce(jnp.maximum, [jnp.max(s, axis=-1, keepdims=True) for s in parts])
    es = [jnp.exp(s - m) for s in parts]
    den = functools.reduce(lambda a, b: a + b, [jnp.sum(e, axis=-1, keepdims=True) for e in es])
    inv = 1.0 / den
    return [e * inv for e in es]


def _lambda(lp, lam_init):
    a = jnp.sum(lp[0:1, :] * lp[1:2, :], axis=-1, keepdims=True)
    b = jnp.sum(lp[2:3, :] * lp[3:4, :], axis=-1, keepdims=True)
    return jnp.exp(a) - jnp.exp(b) + lam_init


def _subln(o, w, lam_init):
    ms = jnp.mean(o * o, axis=-1, keepdims=True)
    return o * lax.rsqrt(ms + EPS) * w * (1.0 - lam_init)


def _rope(x, cos, sin):
    lane = lax.broadcasted_iota(jnp.int32, x.shape, 1)
    swapped = jnp.where((lane & 16) == 0, pltpu.roll(x, LANES - 16, 1), pltpu.roll(x, 16, 1))
    return x * cos + swapped * sin


def _ctx_diff_kernel(q_ref, k_ref, v_ref, lamp_ref, sw_ref, o_ref, *, lam_init):
    scale = DIFF_HEAD_DIM ** -0.5
    lam = _lambda(lamp_ref[...], lam_init)
    sw = sw_ref[...]
    for hb in range(DIFF_HEADS):
        ps = []
        for t in (0, 1):
            sl = slice(hb * LANES + t * DIFF_HEAD_DIM, hb * LANES + (t + 1) * DIFF_HEAD_DIM)
            s = _bdot_nt(q_ref[:, sl], k_ref[:, sl].astype(BF16)) * scale
            ps.append(_softmax_rows([s])[0])
        att = (ps[0] - lam * ps[1]).astype(BF16)
        blk = slice(hb * LANES, (hb + 1) * LANES)
        o = _subln(_bdot(att, v_ref[:, blk].astype(BF16)), sw, lam_init)
        o_ref[:, blk] = o.astype(BF16)


def _ctx_softmax_kernel(q_ref, k_ref, v_ref, o_ref):
    scale = NA_HEAD_DIM ** -0.5
    for h in range(NA_HEADS):
        sl = slice(h * NA_HEAD_DIM, (h + 1) * NA_HEAD_DIM)
        s = _bdot_nt(q_ref[:, sl], k_ref[:, sl].astype(BF16)) * scale
        p = _softmax_rows([s])[0].astype(BF16)
        o_ref[:, sl] = _bdot(p, v_ref[:, sl].astype(BF16)).astype(BF16)


def _ctx_attn(q, k_all, v_all, nb, seq, layer, extra, kern, name):
    kv_spec = pl.BlockSpec((None, None, seq, HEAD_COLS), lambda b: (b, layer, 0, 0))
    extra_specs = [pl.BlockSpec(a.shape, lambda b: (0, 0)) for a in extra]
    return pl.pallas_call(
        kern,
        grid=(nb,),
        in_specs=[pl.BlockSpec((seq, HEAD_COLS), lambda b: (b, 0)), kv_spec, kv_spec] + extra_specs,
        out_specs=pl.BlockSpec((seq, HEAD_COLS), lambda b: (b, 0)),
        out_shape=jax.ShapeDtypeStruct((nb * seq, HEAD_COLS), BF16),
        compiler_params=_cparams("arbitrary"),
        name=name,
    )(q, k_all, v_all, *extra)


def _lat_diff_kernel(q_ref, k_ref, v_ref, ck_ref, cv_ref, cosq_ref, sinq_ref, cosk_ref, sink_ref,
                     lamp_ref, sw_ref, o_ref, *, lam_init):
    scale = DIFF_HEAD_DIM ** -0.5
    q = _rope(q_ref[...], cosq_ref[...], sinq_ref[...])
    k = _rope(k_ref[...], cosk_ref[...], sink_ref[...])
    ck = ck_ref[...]
    v = v_ref[...].astype(BF16)
    cv = cv_ref[...].astype(BF16)
    pl_, pc_ = [], []
    for t in (0, 1):
        sl = slice(t * DIFF_HEAD_DIM, (t + 1) * DIFF_HEAD_DIM)
        qt = q[:, sl].astype(BF16)
        s_loc = _bdot_nt(qt, k[:, sl].astype(BF16)) * scale
        s_ctx = _bdot_nt(qt, ck[:, sl].astype(BF16)) * scale
        p_loc, p_ctx = _softmax_rows([s_loc, s_ctx])
        pl_.append(p_loc)
        pc_.append(p_ctx)
    lam = _lambda(lamp_ref[...], lam_init)
    a_loc = (pl_[0] - lam * pl_[1]).astype(BF16)
    a_ctx = (pc_[0] - lam * pc_[1]).astype(BF16)
    o = _bdot(a_loc, v) + _bdot(a_ctx, cv)
    o_ref[...] = _subln(o, sw_ref[...], lam_init).astype(BF16)


def _lat_diff(q, k, v, nb, seq, cache_k, cache_v, layer, cos, sin, lamp, sw, lam_init):
    tq = 256
    nq = seq // tq
    past = cache_k.shape[2]
    cache_spec = pl.BlockSpec((None, None, past, LANES), lambda b, h, i: (b, layer, 0, h))
    loc_spec = pl.BlockSpec((seq, LANES), lambda b, h, i: (b, h))
    return pl.pallas_call(
        functools.partial(_lat_diff_kernel, lam_init=lam_init),
        grid=(nb, DIFF_HEADS, nq),
        in_specs=[
            pl.BlockSpec((tq, LANES), lambda b, h, i: (b * nq + i, h)),
            loc_spec, loc_spec, cache_spec, cache_spec,
            pl.BlockSpec((tq, LANES), lambda b, h, i: (i, 0)),
            pl.BlockSpec((tq, LANES), lambda b, h, i: (i, 0)),
            pl.BlockSpec((seq, LANES), lambda b, h, i: (0, 0)),
            pl.BlockSpec((seq, LANES), lambda b, h, i: (0, 0)),
            pl.BlockSpec((4, DIFF_HEAD_DIM), lambda b, h, i: (0, 0)),
            pl.BlockSpec((1, LANES), lambda b, h, i: (0, 0)),
        ],
        out_specs=pl.BlockSpec((tq, LANES), lambda b, h, i: (b * nq + i, h)),
        out_shape=jax.ShapeDtypeStruct((nb * seq, HEAD_COLS), BF16),
        compiler_params=_cparams("arbitrary", "arbitrary", "arbitrary"),
        name="lat_diff_attn",
    )(q, k, v, cache_k, cache_v, cos, sin, cos, sin, lamp, sw)


NA_Q_ROWS = 4


def _lat_na_kernel(q_ref, k_ref, v_ref, ck_ref, cv_ref, tb_ref, o_ref, bias_s, *, rows):
    scale = NA_HEAD_DIM ** -0.5
    kh = min(NA_KH, rows)
    win0 = lambda qr: min(max(qr - kh // 2, 0), rows - kh)

    @pl.when(pl.program_id(1) == 0)
    def _():
        bias_s[...] = jnp.full(bias_s.shape, -jnp.inf, F32)
        for t in (0, 1):
            for qr in range(rows):
                r0 = win0(qr)
                a0 = r0 - qr + NA_KH - 1
                bias_s[t, qr * GRID_W:(qr + 1) * GRID_W, r0 * GRID_W:(r0 + kh) * GRID_W] = (
                    tb_ref[t, :, a0 * GRID_W:(a0 + kh) * GRID_W])

    for t in (0, 1):
        sl = slice(t * NA_HEAD_DIM, (t + 1) * NA_HEAD_DIM)
        k_all = k_ref[:, sl].astype(BF16)
        v_all = v_ref[:, sl].astype(BF16)
        ckh = ck_ref[:, sl].astype(BF16)
        cvh = cv_ref[:, sl].astype(BF16)
        for qb in range(rows // NA_Q_ROWS):
            lo = win0(qb * NA_Q_ROWS) // 2 * 2
            hi = min(rows, (win0((qb + 1) * NA_Q_ROWS - 1) + kh + 1) // 2 * 2)
            rs = slice(qb * NA_Q_ROWS * GRID_W, (qb + 1) * NA_Q_ROWS * GRID_W)
            ks = slice(lo * GRID_W, hi * GRID_W)
            qh = q_ref[rs, sl].astype(BF16)
            s_loc = _bdot_nt(qh, k_all[ks]) * scale + bias_s[t, rs, ks]
            s_ctx = _bdot_nt(qh, ckh) * scale
            p_loc, p_ctx = _softmax_rows([s_loc, s_ctx])
            o = _bdot(p_loc.astype(BF16), v_all[ks]) + _bdot(p_ctx.astype(BF16), cvh)
            o_ref[rs, sl] = o.astype(BF16)


def _lat_na(q, k, v, nb, seq, cache_k, cache_v, layer, table):
    past = cache_k.shape[2]
    rows = seq // GRID_W
    ncol = table.shape[-1]
    loc_spec = pl.BlockSpec((seq, LANES), lambda h, b: (b, h))
    cache_spec = pl.BlockSpec((None, None, past, LANES), lambda h, b: (b, layer, 0, h))
    return pl.pallas_call(
        functools.partial(_lat_na_kernel, rows=rows),
        grid=(NA_HEADS // 2, nb),
        in_specs=[loc_spec, loc_spec, loc_spec, cache_spec, cache_spec,
                  pl.BlockSpec((2, GRID_W, ncol), lambda h, b: (h, 0, 0))],
        out_specs=pl.BlockSpec((seq, LANES), lambda h, b: (b, h)),
        out_shape=jax.ShapeDtypeStruct((nb * seq, HEAD_COLS), BF16),
        scratch_shapes=[pltpu.VMEM((2, seq, seq), F32)],
        compiler_params=_cparams("arbitrary", "arbitrary"),
        name="lat_nbr_attn",
    )(q, k, v, cache_k, cache_v, table)


def _merge_kernel(ya_ref, yb_ref, yc_ref, wa_ref, wb_ref, wc_ref, ga_ref, gb_ref, gc_ref, o_ref):
    ya, yb, yc = ya_ref[...], yb_ref[...], yc_ref[...]
    for s in range(D_MODEL // SUB):
        cols = slice(s * SUB, (s + 1) * SUB)
        m = (ga_ref[:, cols] * _bdot(ya, wa_ref[:, cols])
             + gb_ref[:, cols] * _bdot(yb, wb_ref[:, cols])
             + gc_ref[:, cols] * _bdot(yc, wc_ref[:, cols]))
        o_ref[:, cols] = m.astype(BF16)


def _merge(ya, yb, yc, wa, wb, wc, u1):
    t, kk = ya.shape
    n = wa.shape[1]
    tm = TM_MERGE
    ysp = pl.BlockSpec((tm, kk), lambda i: (i, 0))
    return pl.pallas_call(
        _merge_kernel,
        grid=(t // tm,),
        in_specs=[ysp, ysp, ysp, _resident((kk, n)), _resident((kk, n)), _resident((kk, n)),
                  pl.BlockSpec((tm, n), lambda i: (i, 0)),
                  pl.BlockSpec((tm, n), lambda i: (i, 1)),
                  pl.BlockSpec((tm, n), lambda i: (i, 2))],
        out_specs=pl.BlockSpec((tm, n), lambda i: (i, 0)),
        out_shape=jax.ShapeDtypeStruct((t, n), BF16),
        compiler_params=_cparams("arbitrary"),
        name="branch_merge",
    )(ya, yb, yc, wa, wb, wc, u1, u1, u1)


def _resid_mm_kernel(*refs, gate_row, norm_rows):
    if norm_rows is None:
        a_ref, w_ref, x_ref, mod_ref, xo_ref = refs
    else:
        a_ref, w_ref, x_ref, mod_ref, nw_ref, nmod_ref, xo_ref, h_ref = refs
    a = a_ref[...]
    for s in range(D_MODEL // SUB):
        cols = slice(s * SUB, (s + 1) * SUB)
        xo_ref[:, cols] = x_ref[:, cols] + mod_ref[0, gate_row:gate_row + 1, cols] * _bdot(a, w_ref[:, cols])
    if norm_rows is not None:
        shift_row, scale_row = norm_rows
        h = _modulated_norm(xo_ref[...], nw_ref[...], nmod_ref[0, shift_row:shift_row + 1, :],
                            nmod_ref[0, scale_row:scale_row + 1, :])
        h_ref[...] = h.astype(BF16)


def _resid_mm(a, w, x2d, mod, gate_row, tm, mod_row, norm, name):
    t, kk = a.shape
    d = w.shape[1]
    row = lambda i: (i, 0)
    mrow = lambda i: (mod_row(i, tm), 0, 0)
    in_specs = [pl.BlockSpec((tm, kk), row), _resident((kk, d)), pl.BlockSpec((tm, d), row),
                pl.BlockSpec((1, 6, d), mrow)]
    args = [a, w, x2d, mod]
    out_specs = [pl.BlockSpec((tm, d), row)]
    out_shape = [jax.ShapeDtypeStruct((t, d), F32)]
    norm_rows = None
    if norm is not None:
        nw, nmod, shift_row, scale_row = norm
        norm_rows = (shift_row, scale_row)
        in_specs += [pl.BlockSpec((1, d), lambda i: (0, 0)), pl.BlockSpec((1, 6, d), mrow)]
        args += [nw, nmod]
        out_specs.append(pl.BlockSpec((tm, d), row))
        out_shape.append(jax.ShapeDtypeStruct((t, d), BF16))
    res = pl.pallas_call(
        functools.partial(_resid_mm_kernel, gate_row=gate_row, norm_rows=norm_rows),
        grid=(t // tm,),
        in_specs=in_specs,
        out_specs=out_specs,
        out_shape=out_shape,
        compiler_params=_cparams("arbitrary"),
        name=name,
    )(*args)
    return (res[0], res[1]) if norm is not None else (res[0], None)


def _ffn_up_kernel(h_ref, wv_ref, wg_ref, cwv_ref, cwg_ref, cbv_ref, cbg_ref, o_ref, wv_s, wg_s, *, seq):
    @pl.when(pl.program_id(1) == 0)
    def _():
        wv_s[...] = wv_ref[...].astype(BF16)
        wg_s[...] = wg_ref[...].astype(BF16)

    h = h_ref[...]
    tm = h.shape[0]
    pos = lax.broadcasted_iota(jnp.int32, (tm, SUB), 0) % seq

    def conv(acc, cw, cb):
        prev = jnp.where(pos == 0, 0.0, pltpu.roll(acc, 1, 0))
        nxt = jnp.where(pos == seq - 1, 0.0, pltpu.roll(acc, tm - 1, 0))
        return cw[0:1, :] * prev + cw[1:2, :] * acc + cw[2:3, :] * nxt + cb[...]

    val = conv(_bdot(h, wv_s[...]), cwv_ref, cbv_ref)
    gt = conv(_bdot(h, wg_s[...]), cwg_ref, cbg_ref)
    o_ref[...] = (_silu(gt) * val).astype(BF16)


def _ffn_up(h, w_up, cw, cb, seq):
    t, d = h.shape
    tm = max(TM_UP, seq)
    assert tm % seq == 0
    nt = D_FF // SUB
    return pl.pallas_call(
        functools.partial(_ffn_up_kernel, seq=seq),
        grid=(nt, t // tm),
        in_specs=[
            pl.BlockSpec((tm, d), lambda j, i: (i, 0)),
            pl.BlockSpec((d, SUB), lambda j, i: (0, j)),
            pl.BlockSpec((d, SUB), lambda j, i: (0, nt + j)),
            pl.BlockSpec((3, SUB), lambda j, i: (0, j)),
            pl.BlockSpec((3, SUB), lambda j, i: (0, nt + j)),
            pl.BlockSpec((1, SUB), lambda j, i: (0, j)),
            pl.BlockSpec((1, SUB), lambda j, i: (0, nt + j)),
        ],
        out_specs=pl.BlockSpec((tm, SUB), lambda j, i: (i, j)),
        out_shape=jax.ShapeDtypeStruct((t, D_FF), BF16),
        scratch_shapes=[pltpu.VMEM((d, SUB), BF16), pltpu.VMEM((d, SUB), BF16)],
        compiler_params=_cparams("arbitrary", "arbitrary"),
        name="ffn_up_conv",
    )(h, w_up, w_up, cw, cw, cb, cb)


def _rope_tables(seq):
    t = np.arange(seq)
    row = (t // GRID_W).astype(np.float32)
    col = (t % GRID_W).astype(np.float32)
    n_freq = DIFF_HEAD_DIM // 4
    inv = jnp.asarray(ROPE_BASE, F32) ** (-jnp.arange(n_freq, dtype=F32) / n_freq)
    ang_r = jnp.asarray(row)[:, None] * inv
    ang_c = jnp.asarray(col)[:, None] * inv
    cr, sr, cc, sc = jnp.cos(ang_r), jnp.sin(ang_r), jnp.cos(ang_c), jnp.sin(ang_c)
    cos64 = jnp.concatenate([cr, cr, cc, cc], axis=-1)
    sin64 = jnp.concatenate([-sr, sr, -sc, sc], axis=-1)
    return jnp.tile(cos64, (1, 2)), jnp.tile(sin64, (1, 2))


def _na_bias_table(rpb):
    lead = rpb.shape[:-1]
    period = 2 * GRID_W
    n_side = NA_KW - 1
    r = rpb.astype(F32)
    base = jnp.concatenate([r[..., n_side:], jnp.zeros(lead + (period - 2 * n_side - 1,), F32),
                            r[..., :n_side]], axis=-1)
    flat = jnp.tile(base, (1,) * len(lead) + (GRID_W,))[..., :GRID_W * (period - 1)]
    toe = flat.reshape(lead + (GRID_W, period - 1))[..., :GRID_W]
    qc = np.arange(GRID_W)[:, None]
    kc = np.arange(GRID_W)[None, :]
    wstart = np.clip(qc - NA_KW // 2, 0, GRID_W - NA_KW)
    valid = (kc >= wstart) & (kc < wstart + NA_KW)
    tbl = jnp.where(valid, toe, -jnp.inf)
    tbl = jnp.swapaxes(tbl, -3, -2)
    return tbl.reshape(tbl.shape[:-2] + (-1,))


def _head_expand():
    e = np.zeros((2, LANES, SSD_INNER), np.float32)
    for d in range(2):
        for h in range(SSD_HEADS):
            e[d, d * SSD_HEADS + h, h * SSD_HEAD_DIM:(h + 1) * SSD_HEAD_DIM] = 1.0
    return jnp.asarray(e, BF16)


def _seg_ones():
    i = np.arange(LANES)
    return jnp.asarray((i[:, None] // 64 == i[None, :] // 64).astype(np.float32), BF16)


def _pad_lanes(v, n):
    return jnp.pad(v.reshape(1, -1), ((0, 0), (0, n - v.size)))


def _layer_params(l, norm1_w, norm2_w, w_in, ssd_conv_w, ssd_conv_b, ssd_dt_bias, ssd_a_log, ssd_d, ssd_norm_w,
                  diff_q_norm, diff_k_norm, diff_lam, diff_subln_w, na_q_norm, na_k_norm,
                  w_branch_a, w_branch_b, w_branch_c, w_out, ffn_w_up, ffn_conv_w, ffn_conv_b, ffn_w_down):
    w = w_in[l]
    o = np.cumsum([0, 1024, 1536, 32, 3 * HEAD_COLS, 3 * HEAD_COLS, 6144])
    z, xbc, dt, wd, wn, g = [w[:, o[i]:o[i + 1]] for i in range(6)]
    pad = jnp.zeros((w.shape[0], P1_COLS - COL_DT - 32), BF16)
    rep = HEAD_COLS // DIFF_HEAD_DIM
    return {
        "norm1_w": norm1_w[l].reshape(1, -1), "norm2_w": norm2_w[l].reshape(1, -1),
        "w1": jnp.concatenate([g, z, xbc, dt, pad], axis=1), "w_diff": wd, "w_na": wn,
        "diff_qn": jnp.tile(diff_q_norm[l], rep).reshape(1, -1), "diff_kn": jnp.tile(diff_k_norm[l], rep).reshape(1, -1),
        "na_qn": jnp.tile(na_q_norm[l], rep).reshape(1, -1), "na_kn": jnp.tile(na_k_norm[l], rep).reshape(1, -1),
        "cw_xs": ssd_conv_w[l][:, :SSD_INNER], "cw_bc": ssd_conv_w[l][:, SSD_INNER:],
        "cb_xs": ssd_conv_b[l][:SSD_INNER].reshape(1, -1), "cb_bc": ssd_conv_b[l][SSD_INNER:].reshape(1, -1),
        "dt_bias": _pad_lanes(ssd_dt_bias[l], LANES), "a_log": _pad_lanes(ssd_a_log[l], LANES),
        "d_exp": jnp.repeat(ssd_d[l], SSD_HEAD_DIM).reshape(1, -1), "ssd_nw": ssd_norm_w[l].reshape(1, -1),
        "e_heads": _head_expand(),
        "diff_lam": diff_lam[l], "subln_w": diff_subln_w[l].reshape(1, -1),
        "w_a": w_branch_a[l].astype(BF16), "w_b": w_branch_b[l].astype(BF16), "w_c": w_branch_c[l].astype(BF16),
        "w_out": w_out[l].astype(BF16), "w_up": ffn_w_up[l], "w_down": ffn_w_down[l].astype(BF16),
        "ffn_cw": ffn_conv_w[l], "ffn_cb": ffn_conv_b[l].reshape(1, -1),
    }


def _trunk_layer(x2d, h1, nb, seq, mod, mod_row, pw, layer, ctx_out, cache, consts, next_norm):
    u1 = _proj1(h1, pw["w1"])
    lam_init = 0.8 - 0.6 * math.exp(-0.3 * layer)
    bd = consts["bd"]
    if cache is None:
        depth, dk, dv, nk, nv, st = ctx_out
        qd, dk, dv = _qkv(h1, pw["w_diff"], pw["diff_qn"], pw["diff_kn"], bd, nb, seq, layer, (depth, dk, dv),
                          "proj_diff_qkv")
        qn, nk, nv = _qkv(h1, pw["w_na"], pw["na_qn"], pw["na_kn"], bd, nb, seq, layer, (depth, nk, nv),
                          "proj_na_qkv")
        y_a, st = _ssd(u1, nb, seq, pw, None, layer, (depth, st))
        y_b = _ctx_attn(qd, dk, dv, nb, seq, layer, [pw["diff_lam"], pw["subln_w"]],
                        functools.partial(_ctx_diff_kernel, lam_init=lam_init), "ctx_diff_attn")
        y_c = _ctx_attn(qn, nk, nv, nb, seq, layer, [], _ctx_softmax_kernel, "ctx_softmax_attn")
        ctx_out = (depth, dk, dv, nk, nv, st)
    else:
        cdk, cdv, cnk, cnv, h0 = cache
        qd, kd, vd = _qkv(h1, pw["w_diff"], pw["diff_qn"], pw["diff_kn"], bd, nb, seq, layer, None,
                          "proj_diff_qkv")
        qn, kn, vn = _qkv(h1, pw["w_na"], pw["na_qn"], pw["na_kn"], bd, nb, seq, layer, None, "proj_na_qkv")
        y_a, _ = _ssd(u1, nb, seq, pw, h0, layer, None)
        y_b = _lat_diff(qd, kd, vd, nb, seq, cdk, cdv, layer, consts["cos"], consts["sin"], pw["diff_lam"],
                        pw["subln_w"], lam_init)
        y_c = _lat_na(qn, kn, vn, nb, seq, cnk, cnv, layer, pw["na_table"])
    merged = _merge(y_a, y_b, y_c, pw["w_a"], pw["w_b"], pw["w_c"], u1)
    x2d, h2 = _resid_mm(merged, pw["w_out"], x2d, mod, 2, TM_OUT, mod_row, (pw["norm2_w"], mod, 3, 4), "out_proj")
    act = _ffn_up(h2, pw["w_up"], pw["ffn_cw"], pw["ffn_cb"], seq)
    x2d, h1_next = _resid_mm(act, pw["w_down"], x2d, mod, 5, TM_DOWN, mod_row, next_norm, "ffn_down")
    return x2d, h1_next, ctx_out


def kernel(x_prompt, x_sample, c, cache_diff_k, cache_diff_v, cache_na_k, cache_na_v, state_ssm, c_ctx, norm1_w, norm2_w, w_ada, b_ada, w_in, ssd_conv_w, ssd_conv_b, ssd_dt_bias, ssd_a_log, ssd_d, ssd_norm_w, diff_q_norm, diff_k_norm, diff_lam, diff_subln_w, na_q_norm, na_k_norm, na_rpb, w_branch_a, w_branch_b, w_branch_c, w_out, ffn_w_up, ffn_conv_w, ffn_conv_b, ffn_w_down):
    batch, seq, d = x_prompt.shape
    dec_batch, dec_seq, _ = x_sample.shape
    depth = w_in.shape[0]
    past = cache_diff_k.shape[2]
    assert d == D_MODEL and dec_batch + 1 <= 8 and seq % SSD_CHUNK == 0 and dec_seq % (NA_KH * GRID_W) == 0

    c_rows = jnp.concatenate([c_ctx.reshape(1, d), c, jnp.zeros((8 - 1 - dec_batch, d), F32)], axis=0)
    mod_all = _ada(c_rows, w_ada, b_ada).reshape(depth, 8, 6, d)

    cos, sin = _rope_tables(dec_seq)
    consts = {"bd": _seg_ones(), "cos": cos, "sin": sin}
    na_tables = _na_bias_table(na_rpb)
    cdk = cache_diff_k.reshape(dec_batch, depth, past, HEAD_COLS)
    cdv = cache_diff_v.reshape(dec_batch, depth, past, HEAD_COLS)
    cnk = cache_na_k.reshape(dec_batch, depth, past, HEAD_COLS)
    cnv = cache_na_v.reshape(dec_batch, depth, past, HEAD_COLS)
    h0 = state_ssm.reshape(dec_batch, depth, 2, SSD_INNER, SSD_STATE)

    w_in_b = w_in.astype(BF16)
    pws = []
    for l in range(depth):
        pw = _layer_params(l, norm1_w, norm2_w, w_in_b, ssd_conv_w, ssd_conv_b, ssd_dt_bias, ssd_a_log, ssd_d,
                           ssd_norm_w, diff_q_norm, diff_k_norm, diff_lam, diff_subln_w, na_q_norm, na_k_norm,
                           w_branch_a, w_branch_b, w_branch_c, w_out, ffn_w_up, ffn_conv_w, ffn_conv_b, ffn_w_down)
        pw["na_table"] = na_tables[l]
        pws.append(pw)

    ctx_row = lambda i, tm: 0
    lat_row = lambda i, tm: 1 + (i * tm) // dec_seq
    y_p = x_prompt.reshape(batch * seq, d)
    y_s = x_sample.reshape(dec_batch * dec_seq, d)
    h_p = _norm(y_p, pws[0]["norm1_w"], mod_all[0], ctx_row)
    h_s = _norm(y_s, pws[0]["norm1_w"], mod_all[0], lat_row)
    ctx_out = (depth, None, None, None, None, None)
    cache = (cdk, cdv, cnk, cnv, h0)
    for l in range(depth):
        next_norm = (pws[l + 1]["norm1_w"], mod_all[l + 1], 0, 1) if l + 1 < depth else None
        y_p, h_p, ctx_out = _trunk_layer(y_p, h_p, batch, seq, mod_all[l], ctx_row, pws[l], l, ctx_out, None,
                                         consts, next_norm)
        y_s, h_s, _ = _trunk_layer(y_s, h_s, dec_batch, dec_seq, mod_all[l], lat_row, pws[l], l, None, cache,
                                   consts, next_norm)
    _, dk, dv, nk, nv, st = ctx_out
    return (y_p.reshape(batch, seq, d), y_s.reshape(dec_batch, dec_seq, d),
            dk.reshape(batch, depth, seq, DIFF_HEADS, 2, DIFF_HEAD_DIM),
            dv.reshape(batch, depth, seq, DIFF_HEADS, 2 * DIFF_HEAD_DIM),
            nk.reshape(batch, depth, seq, NA_HEADS, NA_HEAD_DIM),
            nv.reshape(batch, depth, seq, NA_HEADS, NA_HEAD_DIM),
            st.reshape(batch, depth, 2, SSD_HEADS, SSD_HEAD_DIM, SSD_STATE))
```

```python
import functools
import math

import numpy as np
import jax
import jax.numpy as jnp
from jax import lax
from jax.experimental import pallas as pl
from jax.experimental.pallas import tpu as pltpu

F32 = jnp.float32
BF16 = jnp.bfloat16

D_MODEL = 2048
GRID_W = 64
SSD_INNER = 1024
SSD_HEAD_DIM = 64
SSD_HEADS = 16
SSD_GROUPS = 2
SSD_STATE = 128
SSD_CHUNK = 128
SSD_BC = 2 * SSD_GROUPS * SSD_STATE
DIFF_HEADS = 8
DIFF_HEAD_DIM = 64
NA_HEADS = 16
NA_HEAD_DIM = 64
NA_KH = 8
NA_KW = 16
HEAD_COLS = 1024
D_FF = 5632
ROPE_BASE = 10000.0
EPS = 1e-6

LANES = 128
VMEM_LIMIT_BYTES = 56 * 1024 * 1024
SUB = 512

COL_G = 0
COL_Z = 6144
COL_XS = 7168
COL_BC = 8192
COL_DT = 8704
P1_COLS = 9216
P1_CHUNK = 3072
N_GATE_CHUNKS = COL_Z // P1_CHUNK

TM_NORM = 1024
TM_PROJ = 512
TM_MERGE = 512
TM_OUT = 512
TM_UP = 1024
TM_DOWN = 256


def _cparams(*sem):
    return pltpu.CompilerParams(dimension_semantics=sem, vmem_limit_bytes=VMEM_LIMIT_BYTES)


def _resident(shape):
    nd = len(shape)
    return pl.BlockSpec(shape, lambda *_: (0,) * nd, pipeline_mode=pl.Buffered(1))


def _sigmoid(x):
    return 1.0 / (1.0 + jnp.exp(-x))


def _silu(x):
    return x * _sigmoid(x)


def _bdot(a, b):
    return jnp.dot(a, b, preferred_element_type=F32)


def _bdot_nt(a, b):
    return lax.dot_general(a, b, (((1,), (1,)), ((), ())), preferred_element_type=F32)


def _split3(x):
    p1 = x.astype(BF16)
    r1 = x - p1.astype(F32)
    p2 = r1.astype(BF16)
    p3 = (r1 - p2.astype(F32)).astype(BF16)
    return p1, p2, p3


def _sel_right(x, e):
    p1, p2, p3 = _split3(x)
    return _bdot(p1, e) + _bdot(p2, e) + _bdot(p3, e)


def _sel_left(t, x):
    p1, p2, p3 = _split3(x)
    return _bdot(t, p1) + _bdot(t, p2) + _bdot(t, p3)


def _seg64_rms(a, bd):
    sq = a * a
    hi = sq.astype(BF16)
    lo = (sq - hi.astype(F32)).astype(BF16)
    ss = _bdot(hi, bd) + _bdot(lo, bd)
    return a * lax.rsqrt(ss * (1.0 / 64.0) + EPS)


def _modulated_norm(x, nw, shift, scale):
    ms = jnp.mean(x * x, axis=-1, keepdims=True)
    y = x * lax.rsqrt(ms + EPS) * nw
    return y * (1.0 + scale) + shift


def _ada_kernel(c_ref, w_ref, b_ref, o_ref):
    c = c_ref[...]
    s = _silu(c).astype(BF16)
    o_ref[0] = _bdot(s, w_ref[0].astype(BF16)) + b_ref[0]


def _ada(c_rows, w_ada, b_ada):
    depth, d, n = w_ada.shape
    tn = 1024
    return pl.pallas_call(
        _ada_kernel,
        grid=(depth, n // tn),
        in_specs=[
            pl.BlockSpec((8, d), lambda l, j: (0, 0)),
            pl.BlockSpec((1, d, tn), lambda l, j: (l, 0, j)),
            pl.BlockSpec((1, 1, tn), lambda l, j: (l, 0, j)),
        ],
        out_specs=pl.BlockSpec((1, 8, tn), lambda l, j: (l, 0, j)),
        out_shape=jax.ShapeDtypeStruct((depth, 8, n), F32),
        compiler_params=_cparams("arbitrary", "arbitrary"),
        name="ada_mod",
    )(c_rows, w_ada, b_ada.reshape(depth, 1, n))


def _norm_kernel(x_ref, nw_ref, mod_ref, h_ref):
    h_ref[...] = _modulated_norm(x_ref[...], nw_ref[...], mod_ref[0, 0:1, :], mod_ref[0, 1:2, :]).astype(BF16)


def _norm(x2d, nw, mod, mod_row):
    t, d = x2d.shape
    tm = TM_NORM
    return pl.pallas_call(
        _norm_kernel,
        grid=(t // tm,),
        in_specs=[
            pl.BlockSpec((tm, d), lambda i: (i, 0)),
            pl.BlockSpec((1, d), lambda i: (0, 0)),
            pl.BlockSpec((1, 6, d), lambda i: (mod_row(i, tm), 0, 0)),
        ],
        out_specs=pl.BlockSpec((tm, d), lambda i: (i, 0)),
        out_shape=jax.ShapeDtypeStruct((t, d), BF16),
        compiler_params=_cparams("arbitrary"),
        name="norm_mod",
    )(x2d, nw, mod)


def _proj1_kernel(h_ref, w_ref, o_ref):
    c = pl.program_id(0)
    h = h_ref[...]

    def tiles(epilogue):
        for s in range(P1_CHUNK // SUB):
            cols = slice(s * SUB, (s + 1) * SUB)
            o_ref[:, cols] = epilogue(_bdot(h, w_ref[:, cols]))

    @pl.when(c < N_GATE_CHUNKS)
    def _():
        tiles(_sigmoid)

    @pl.when(c >= N_GATE_CHUNKS)
    def _():
        tiles(lambda a: a)


def _proj1(h, w_p, layer):
    t, d = h.shape
    tm = TM_PROJ
    return pl.pallas_call(
        _proj1_kernel,
        grid=(P1_COLS // P1_CHUNK, t // tm),
        in_specs=[
            pl.BlockSpec((tm, d), lambda c, i: (i, 0)),
            pl.BlockSpec((None, d, P1_CHUNK), lambda c, i: (layer, 0, c)),
        ],
        out_specs=pl.BlockSpec((tm, P1_CHUNK), lambda c, i: (i, c)),
        out_shape=jax.ShapeDtypeStruct((t, P1_COLS), F32),
        compiler_params=_cparams("arbitrary", "arbitrary"),
        name="proj_gates_ssd",
    )(h, w_p)


def _qkv_kernel(*refs, n_alias):
    h_ref, w_ref, qnw_ref, knw_ref, bd_ref = refs[:5]
    q_ref, k_ref, v_ref = refs[5 + n_alias:]
    h = h_ref[...]
    bd = bd_ref[...]
    for sec, (o_ref, nw_ref) in enumerate(((q_ref, qnw_ref), (k_ref, knw_ref), (v_ref, None))):
        for s in range(HEAD_COLS // SUB):
            acc = _bdot(h, w_ref[:, sec * HEAD_COLS + s * SUB: sec * HEAD_COLS + (s + 1) * SUB])
            if nw_ref is not None:
                parts = [_seg64_rms(acc[:, c * LANES:(c + 1) * LANES], bd) for c in range(SUB // LANES)]
                acc = jnp.concatenate(parts, axis=1) * nw_ref[:, s * SUB:(s + 1) * SUB]
            o_ref[..., s * SUB:(s + 1) * SUB] = acc.reshape(o_ref.shape[:-1] + (SUB,)).astype(o_ref.dtype)


def _qkv(h, w, w_chunk, qnw, knw, bd, nb, seq, layer, kv_prev, name):
    t, d = h.shape
    tm = TM_PROJ
    in_specs = [
        pl.BlockSpec((tm, d), lambda i: (i, 0)),
        pl.BlockSpec((None, d, P1_CHUNK), lambda i: (layer, 0, w_chunk), pipeline_mode=pl.Buffered(1)),
        pl.BlockSpec((1, HEAD_COLS), lambda i: (0, 0)),
        pl.BlockSpec((1, HEAD_COLS), lambda i: (0, 0)),
        pl.BlockSpec((LANES, LANES), lambda i: (0, 0)),
    ]
    args = [h, w, qnw, knw, bd]
    q_spec = pl.BlockSpec((tm, HEAD_COLS), lambda i: (i, 0))
    q_shape = jax.ShapeDtypeStruct((t, HEAD_COLS), F32 if kv_prev is None else BF16)
    aliases = {}
    n_alias = 0
    if kv_prev is None:
        kv_spec, kv_shape = q_spec, q_shape
    else:
        depth, k_prev, v_prev = kv_prev
        assert tm % seq == 0
        kv_spec = pl.BlockSpec((tm // seq, None, seq, HEAD_COLS), lambda i: (i, layer, 0, 0))
        kv_shape = jax.ShapeDtypeStruct((nb, depth, seq, HEAD_COLS), F32)
        if k_prev is not None:
            in_specs += [pl.BlockSpec(memory_space=pl.ANY)] * 2
            args += [k_prev, v_prev]
            aliases = {5: 1, 6: 2}
            n_alias = 2
    return pl.pallas_call(
        functools.partial(_qkv_kernel, n_alias=n_alias),
        grid=(t // tm,),
        in_specs=in_specs,
        out_specs=[q_spec, kv_spec, kv_spec],
        out_shape=[q_shape, kv_shape, kv_shape],
        input_output_aliases=aliases,
        compiler_params=_cparams("arbitrary"),
        name=name,
    )(*args)


def _ssd_kernel(*refs, seq, has_h0, has_state_out, n_alias):
    (z_ref, xs_ref, bc_ref, dt_ref, cwx_ref, cbx_ref, cwb_ref, cbb_ref, dtb_ref, alog_ref,
     dexp_ref, nw_ref, e_ref) = refs[:13]
    rest = list(refs[13:])
    h0_ref = rest.pop(0) if has_h0 else None
    rest = rest[n_alias:]
    y_ref = rest.pop(0)
    st_ref = rest.pop(0) if has_state_out else None
    xs_s, bc_s, ya_s, st_s = rest

    q = SSD_CHUNK
    nc = seq // q
    half = SSD_INNER // SSD_GROUPS

    def conv_chunk(c, carry):
        r0 = pl.multiple_of(c * q, q)
        for src, dst, cw, cb in ((xs_ref, xs_s, cwx_ref, cbx_ref), (bc_ref, bc_s, cwb_ref, cbb_ref)):
            x = src[pl.ds(r0, q), :]
            xp = src[pl.ds(jnp.maximum(r0 - 1, 0), 1), :]
            xn = src[pl.ds(jnp.minimum(r0 + q, seq - 1), 1), :]
            xp = jnp.where(c == 0, 0.0, xp)
            xn = jnp.where(c == nc - 1, 0.0, xn)
            rows = lax.broadcasted_iota(jnp.int32, x.shape, 0)
            prev = jnp.where(rows == 0, xp, pltpu.roll(x, 1, 0))
            nxt = jnp.where(rows == q - 1, xn, pltpu.roll(x, q - 1, 0))
            y = cw[0:1, :] * prev + cw[1:2, :] * x + cw[2:3, :] * nxt + cb[...]
            dst[pl.ds(r0, q), :] = _silu(y)
        return carry

    lax.fori_loop(0, nc, conv_chunk, 0)

    a_vec = -jnp.exp(alog_ref[...])
    dt_bias = dtb_ref[...]
    ri = lax.broadcasted_iota(jnp.int32, (q, q), 0)
    ci = lax.broadcasted_iota(jnp.int32, (q, q), 1)

    for d in (0, 1):
        tri = (ri >= ci) if d == 0 else (ci >= ri)
        tri_b = jnp.where(tri, 1.0, 0.0).astype(BF16)
        for g in range(SSD_GROUPS):
            if has_h0:
                st_s[g] = h0_ref[d, g * half:(g + 1) * half, :].T
            else:
                st_s[g] = jnp.zeros((SSD_STATE, half), F32)

        def chunk(i, carry, d=d, tri=tri, tri_b=tri_b):
            c = i if d == 0 else nc - 1 - i
            r0 = pl.multiple_of(c * q, q)
            x_dt = dt_ref[pl.ds(r0, q), :] + dt_bias
            dtc = jnp.maximum(x_dt, 0.0) + jnp.log1p(jnp.exp(-jnp.abs(x_dt)))
            la = dtc * a_vec
            cum = _sel_left(tri_b, la)
            cum_t = cum.T
            e = e_ref[d]
            dt_x = _sel_right(dtc, e)
            cum_x = _sel_right(cum, e)
            last = q - 1 if d == 0 else 0
            cl = cum_x[last:last + 1, :]
            xd = xs_s[pl.ds(r0, q), :] * dt_x
            xdb = xd.astype(BF16)
            xdd = (xd * jnp.exp(cl - cum_x)).astype(BF16)
            ecum = jnp.exp(cum_x)
            bcv = bc_s[pl.ds(r0, q), :]
            for g in range(SSD_GROUPS):
                bg = bcv[:, g * SSD_STATE:(g + 1) * SSD_STATE]
                cg = bcv[:, (SSD_GROUPS + g) * SSD_STATE:(SSD_GROUPS + g + 1) * SSD_STATE]
                bgb = bg.astype(BF16)
                cgb = cg.astype(BF16)
                gm = _bdot_nt(cgb, bgb)
                st = st_s[g]
                y_off = _bdot(cgb, st.astype(BF16)) * ecum[:, g * half:(g + 1) * half]
                ys = []
                for hh in range(SSD_HEADS // SSD_GROUPS):
                    h = g * (SSD_HEADS // SSD_GROUPS) + hh
                    k = d * SSD_HEADS + h
                    decay = jnp.where(tri, jnp.exp(cum[:, k:k + 1] - cum_t[k:k + 1, :]), 0.0)
                    m = (gm * decay).astype(BF16)
                    ys.append(_bdot(m, xdb[:, h * SSD_HEAD_DIM:(h + 1) * SSD_HEAD_DIM]))
                yg = jnp.concatenate(ys, axis=1) + y_off
                if d == 0:
                    ya_s[pl.ds(r0, q), g * half:(g + 1) * half] = yg
                else:
                    ya_s[pl.ds(r0, q), g * half:(g + 1) * half] += yg
                st_s[g] = (st * jnp.exp(cl[:, g * half:(g + 1) * half])
                           + _bdot(bg.T.astype(BF16), xdd[:, g * half:(g + 1) * half]))
            return carry

        lax.fori_loop(0, nc, chunk, 0)
        if has_state_out:
            for g in range(SSD_GROUPS):
                st_ref[d, g * half:(g + 1) * half, :] = st_s[g].T

    def fin_chunk(c, carry):
        r0 = pl.multiple_of(c * q, q)
        y = ya_s[pl.ds(r0, q), :] + dexp_ref[...] * xs_s[pl.ds(r0, q), :]
        y = y * _silu(z_ref[pl.ds(r0, q), :])
        for g in range(SSD_GROUPS):
            v = y[:, g * half:(g + 1) * half]
            ms = jnp.mean(v * v, axis=-1, keepdims=True)
            out = v * lax.rsqrt(ms + EPS) * nw_ref[:, g * half:(g + 1) * half]
            y_ref[pl.ds(r0, q), g * half:(g + 1) * half] = out.astype(BF16)
        return carry

    lax.fori_loop(0, nc, fin_chunk, 0)


def _ssd(u1, nb, seq, pw, h0, layer, state_prev):
    has_h0 = h0 is not None
    has_state_out = state_prev is not None
    c1 = lambda b: (0, 0)
    in_specs = [
        pl.BlockSpec((seq, SSD_INNER), lambda b: (b, COL_Z // SSD_INNER)),
        pl.BlockSpec((seq, SSD_INNER), lambda b: (b, COL_XS // SSD_INNER)),
        pl.BlockSpec((seq, SSD_BC), lambda b: (b, COL_BC // SSD_BC)),
        pl.BlockSpec((seq, LANES), lambda b: (b, COL_DT // LANES)),
        pl.BlockSpec((3, SSD_INNER), c1),
        pl.BlockSpec((1, SSD_INNER), c1),
        pl.BlockSpec((3, SSD_BC), c1),
        pl.BlockSpec((1, SSD_BC), c1),
        pl.BlockSpec((1, LANES), c1),
        pl.BlockSpec((1, LANES), c1),
        pl.BlockSpec((1, SSD_INNER), c1),
        pl.BlockSpec((1, SSD_INNER), c1),
        pl.BlockSpec((2, LANES, SSD_INNER), lambda b: (0, 0, 0)),
    ]
    args = [u1, u1, u1, u1, pw["cw_xs"], pw["cb_xs"], pw["cw_bc"], pw["cb_bc"], pw["dt_bias"], pw["a_log"],
            pw["d_exp"], pw["ssd_nw"], pw["e_heads"]]
    if has_h0:
        in_specs.append(pl.BlockSpec((None, None, 2, SSD_INNER, SSD_STATE), lambda b: (b, layer, 0, 0, 0)))
        args.append(h0)
    out_specs = [pl.BlockSpec((seq, SSD_INNER), lambda b: (b, 0))]
    out_shape = [jax.ShapeDtypeStruct((nb * seq, SSD_INNER), BF16)]
    aliases = {}
    n_alias = 0
    if has_state_out:
        depth, prev = state_prev
        out_specs.append(pl.BlockSpec((None, None, 2, SSD_INNER, SSD_STATE), lambda b: (b, layer, 0, 0, 0)))
        out_shape.append(jax.ShapeDtypeStruct((nb, depth, 2, SSD_INNER, SSD_STATE), F32))
        if prev is not None:
            aliases = {len(args): 1}
            in_specs.append(pl.BlockSpec(memory_space=pl.ANY))
            args.append(prev)
            n_alias = 1
    res = pl.pallas_call(
        functools.partial(_ssd_kernel, seq=seq, has_h0=has_h0, has_state_out=has_state_out, n_alias=n_alias),
        grid=(nb,),
        in_specs=in_specs,
        out_specs=out_specs,
        out_shape=out_shape,
        input_output_aliases=aliases,
        scratch_shapes=[
            pltpu.VMEM((seq, SSD_INNER), F32),
            pltpu.VMEM((seq, SSD_BC), F32),
            pltpu.VMEM((seq, SSD_INNER), F32),
            pltpu.VMEM((SSD_GROUPS, SSD_STATE, SSD_INNER // SSD_GROUPS), F32),
        ],
        compiler_params=_cparams("arbitrary"),
        name="ssd_bidir",
    )(*args)
    return res if has_state_out else (res[0], None)


def _softmax_rows(parts):
    m = functools.reduce(jnp.maximum, [jnp.max(s, axis=-1, keepdims=True) for s in parts])
    es = [jnp.exp(s - m) for s in parts]
    den = functools.reduce(lambda a, b: a + b, [jnp.sum(e, axis=-1, keepdims=True) for e in es])
    inv = 1.0 / den
    return [e * inv for e in es]


def _exp_rows(parts):
    m = functools.reduce(jnp.maximum, [jnp.max(s, axis=-1, keepdims=True) for s in parts])
    return [jnp.exp(s - m).astype(BF16) for s in parts]


def _lane_halves(x):
    left = lax.broadcasted_iota(jnp.int32, x.shape, 1) < LANES // 2
    return jnp.where(left, x, 0.0).astype(BF16), jnp.where(left, 0.0, x).astype(BF16)


def _lambda(lp, lam_init):
    a = jnp.sum(lp[0:1, :] * lp[1:2, :], axis=-1, keepdims=True)
    b = jnp.sum(lp[2:3, :] * lp[3:4, :], axis=-1, keepdims=True)
    return jnp.exp(a) - jnp.exp(b) + lam_init


def _subln(o, w, lam_init):
    ms = jnp.mean(o * o, axis=-1, keepdims=True)
    return o * lax.rsqrt(ms + EPS) * w * (1.0 - lam_init)


def _rope(x, cos, sin):
    lane = lax.broadcasted_iota(jnp.int32, x.shape, 1)
    swapped = jnp.where((lane & 16) == 0, pltpu.roll(x, LANES - 16, 1), pltpu.roll(x, 16, 1))
    return x * cos + swapped * sin


def _ctx_diff_kernel(q_ref, k_ref, v_ref, lamp_ref, sw_ref, o_ref, *, lam_init):
    scale = DIFF_HEAD_DIM ** -0.5
    lam = _lambda(lamp_ref[...], lam_init)
    sw = sw_ref[...]
    for hb in range(DIFF_HEADS):
        ps = []
        for t in (0, 1):
            sl = slice(hb * LANES + t * DIFF_HEAD_DIM, hb * LANES + (t + 1) * DIFF_HEAD_DIM)
            s = _bdot_nt(q_ref[:, sl], k_ref[:, sl].astype(BF16)) * scale
            ps.append(_softmax_rows([s])[0])
        att = (ps[0] - lam * ps[1]).astype(BF16)
        blk = slice(hb * LANES, (hb + 1) * LANES)
        o = _subln(_bdot(att, v_ref[:, blk].astype(BF16)), sw, lam_init)
        o_ref[:, blk] = o.astype(BF16)


def _ctx_softmax_kernel(q_ref, k_ref, v_ref, o_ref):
    scale = NA_HEAD_DIM ** -0.5
    ones_l, ones_r = _lane_halves(jnp.ones((k_ref.shape[0], LANES), F32))
    for hb in range(NA_HEADS // 2):
        blk = slice(hb * LANES, (hb + 1) * LANES)
        v_l, v_r = _lane_halves(v_ref[:, blk])
        es = []
        for t in (0, 1):
            sl = slice(hb * LANES + t * NA_HEAD_DIM, hb * LANES + (t + 1) * NA_HEAD_DIM)
            s = _bdot_nt(q_ref[:, sl], k_ref[:, sl].astype(BF16)) * scale
            es.append(_exp_rows([s])[0])
        num = _bdot(es[0], v_l) + _bdot(es[1], v_r)
        den = _bdot(es[0], ones_l) + _bdot(es[1], ones_r)
        o_ref[:, blk] = (num / den).astype(BF16)


def _ctx_attn(q, k_all, v_all, nb, seq, layer, extra, kern, name):
    kv_spec = pl.BlockSpec((None, None, seq, HEAD_COLS), lambda b: (b, layer, 0, 0))
    extra_specs = [pl.BlockSpec(a.shape, lambda b: (0, 0)) for a in extra]
    return pl.pallas_call(
        kern,
        grid=(nb,),
        in_specs=[pl.BlockSpec((seq, HEAD_COLS), lambda b: (b, 0)), kv_spec, kv_spec] + extra_specs,
        out_specs=pl.BlockSpec((seq, HEAD_COLS), lambda b: (b, 0)),
        out_shape=jax.ShapeDtypeStruct((nb * seq, HEAD_COLS), BF16),
        compiler_params=_cparams("arbitrary"),
        name=name,
    )(q, k_all, v_all, *extra)


LAT_DIFF_TQ = 256


def _lat_diff_kernel(q_ref, k_ref, v_ref, ck_ref, cv_ref, cos_ref, sin_ref, lamp_ref, sw_ref, o_ref, *, lam_init):
    scale = DIFF_HEAD_DIM ** -0.5
    k = _rope(k_ref[...], cos_ref[...], sin_ref[...])
    k_t = [k[:, t * DIFF_HEAD_DIM:(t + 1) * DIFF_HEAD_DIM].astype(BF16) for t in (0, 1)]
    ck_t = [ck_ref[:, t * DIFF_HEAD_DIM:(t + 1) * DIFF_HEAD_DIM].astype(BF16) for t in (0, 1)]
    v = v_ref[...].astype(BF16)
    cv = cv_ref[...].astype(BF16)
    lam = _lambda(lamp_ref[...], lam_init)
    sw = sw_ref[...]
    for qb in range(q_ref.shape[0] // LAT_DIFF_TQ):
        rs = slice(qb * LAT_DIFF_TQ, (qb + 1) * LAT_DIFF_TQ)
        q = _rope(q_ref[rs, :], cos_ref[rs, :], sin_ref[rs, :])
        pl_, pc_ = [], []
        for t in (0, 1):
            qt = q[:, t * DIFF_HEAD_DIM:(t + 1) * DIFF_HEAD_DIM].astype(BF16)
            p_loc, p_ctx = _softmax_rows([_bdot_nt(qt, k_t[t]) * scale, _bdot_nt(qt, ck_t[t]) * scale])
            pl_.append(p_loc)
            pc_.append(p_ctx)
        a_loc = (pl_[0] - lam * pl_[1]).astype(BF16)
        a_ctx = (pc_[0] - lam * pc_[1]).astype(BF16)
        o = _bdot(a_loc, v) + _bdot(a_ctx, cv)
        o_ref[rs, :] = _subln(o, sw, lam_init).astype(BF16)


def _lat_diff(q, k, v, nb, seq, cache_k, cache_v, layer, cos, sin, lamp, sw, lam_init):
    past = cache_k.shape[2]
    cache_spec = pl.BlockSpec((None, None, past, LANES), lambda b, h: (b, layer, 0, h))
    loc_spec = pl.BlockSpec((seq, LANES), lambda b, h: (b, h))
    tab_spec = pl.BlockSpec((seq, LANES), lambda b, h: (0, 0))
    return pl.pallas_call(
        functools.partial(_lat_diff_kernel, lam_init=lam_init),
        grid=(nb, DIFF_HEADS),
        in_specs=[
            loc_spec, loc_spec, loc_spec, cache_spec, cache_spec, tab_spec, tab_spec,
            pl.BlockSpec((4, DIFF_HEAD_DIM), lambda b, h: (0, 0)),
            pl.BlockSpec((1, LANES), lambda b, h: (0, 0)),
        ],
        out_specs=loc_spec,
        out_shape=jax.ShapeDtypeStruct((nb * seq, HEAD_COLS), BF16),
        compiler_params=_cparams("arbitrary", "arbitrary"),
        name="lat_diff_attn",
    )(q, k, v, cache_k, cache_v, cos, sin, lamp, sw)


NA_Q_ROWS = 4


def _lat_na_kernel(q_ref, k_ref, v_ref, ck_ref, cv_ref, tb_ref, o_ref, bias_s, *, rows):
    scale = NA_HEAD_DIM ** -0.5
    kh = min(NA_KH, rows)
    win0 = lambda qr: min(max(qr - kh // 2, 0), rows - kh)

    @pl.when(pl.program_id(1) == 0)
    def _():
        bias_s[...] = jnp.full(bias_s.shape, -jnp.inf, F32)
        for t in (0, 1):
            for qr in range(rows):
                r0 = win0(qr)
                a0 = r0 - qr + NA_KH - 1
                bias_s[t, qr * GRID_W:(qr + 1) * GRID_W, r0 * GRID_W:(r0 + kh) * GRID_W] = (
                    tb_ref[t, :, a0 * GRID_W:(a0 + kh) * GRID_W])

    k_h = [k_ref[:, t * NA_HEAD_DIM:(t + 1) * NA_HEAD_DIM].astype(BF16) for t in (0, 1)]
    ck_h = [ck_ref[:, t * NA_HEAD_DIM:(t + 1) * NA_HEAD_DIM].astype(BF16) for t in (0, 1)]
    v_h = _lane_halves(v_ref[...])
    cv_h = _lane_halves(cv_ref[...])
    ones_h = lambda n: _lane_halves(jnp.ones((n, LANES), F32))
    cones_h = ones_h(cv_ref.shape[0])
    for qb in range(rows // NA_Q_ROWS):
        lo = win0(qb * NA_Q_ROWS) // 2 * 2
        hi = min(rows, (win0((qb + 1) * NA_Q_ROWS - 1) + kh + 1) // 2 * 2)
        rs = slice(qb * NA_Q_ROWS * GRID_W, (qb + 1) * NA_Q_ROWS * GRID_W)
        ks = slice(lo * GRID_W, hi * GRID_W)
        num = den = None
        for t in (0, 1):
            qh = q_ref[rs, t * NA_HEAD_DIM:(t + 1) * NA_HEAD_DIM].astype(BF16)
            s_loc = _bdot_nt(qh, k_h[t][ks]) * scale + bias_s[t, rs, ks]
            s_ctx = _bdot_nt(qh, ck_h[t]) * scale
            e_loc, e_ctx = _exp_rows([s_loc, s_ctx])
            n_t = _bdot(e_loc, v_h[t][ks]) + _bdot(e_ctx, cv_h[t])
            d_t = _bdot(e_loc, ones_h((hi - lo) * GRID_W)[t]) + _bdot(e_ctx, cones_h[t])
            num = n_t if num is None else num + n_t
            den = d_t if den is None else den + d_t
        o_ref[rs, :] = (num / den).astype(BF16)


def _lat_na(q, k, v, nb, seq, cache_k, cache_v, layer, table):
    past = cache_k.shape[2]
    rows = seq // GRID_W
    ncol = table.shape[-1]
    loc_spec = pl.BlockSpec((seq, LANES), lambda h, b: (b, h))
    cache_spec = pl.BlockSpec((None, None, past, LANES), lambda h, b: (b, layer, 0, h))
    return pl.pallas_call(
        functools.partial(_lat_na_kernel, rows=rows),
        grid=(NA_HEADS // 2, nb),
        in_specs=[loc_spec, loc_spec, loc_spec, cache_spec, cache_spec,
                  pl.BlockSpec((2, GRID_W, ncol), lambda h, b: (h, 0, 0))],
        out_specs=pl.BlockSpec((seq, LANES), lambda h, b: (b, h)),
        out_shape=jax.ShapeDtypeStruct((nb * seq, HEAD_COLS), BF16),
        scratch_shapes=[pltpu.VMEM((2, seq, seq), F32)],
        compiler_params=_cparams("arbitrary", "arbitrary"),
        name="lat_nbr_attn",
    )(q, k, v, cache_k, cache_v, table)


def _merge_kernel(ya_ref, yb_ref, yc_ref, wa_ref, wb_ref, wc_ref, ga_ref, gb_ref, gc_ref, o_ref):
    ya, yb, yc = ya_ref[...], yb_ref[...], yc_ref[...]
    for s in range(D_MODEL // SUB):
        cols = slice(s * SUB, (s + 1) * SUB)
        m = (ga_ref[:, cols] * _bdot(ya, wa_ref[:, cols])
             + gb_ref[:, cols] * _bdot(yb, wb_ref[:, cols])
             + gc_ref[:, cols] * _bdot(yc, wc_ref[:, cols]))
        o_ref[:, cols] = m.astype(BF16)


def _merge(ya, yb, yc, wa, wb, wc, u1):
    t, kk = ya.shape
    n = wa.shape[1]
    tm = TM_MERGE
    ysp = pl.BlockSpec((tm, kk), lambda i: (i, 0))
    return pl.pallas_call(
        _merge_kernel,
        grid=(t // tm,),
        in_specs=[ysp, ysp, ysp, _resident((kk, n)), _resident((kk, n)), _resident((kk, n)),
                  pl.BlockSpec((tm, n), lambda i: (i, 0)),
                  pl.BlockSpec((tm, n), lambda i: (i, 1)),
                  pl.BlockSpec((tm, n), lambda i: (i, 2))],
        out_specs=pl.BlockSpec((tm, n), lambda i: (i, 0)),
        out_shape=jax.ShapeDtypeStruct((t, n), BF16),
        compiler_params=_cparams("arbitrary"),
        name="branch_merge",
    )(ya, yb, yc, wa, wb, wc, u1, u1, u1)


def _resid_mm_kernel(*refs, gate_row, norm_rows):
    if norm_rows is None:
        a_ref, w_ref, x_ref, mod_ref, xo_ref = refs
    else:
        a_ref, w_ref, x_ref, mod_ref, nw_ref, nmod_ref, xo_ref, h_ref = refs
    a = a_ref[...]
    for s in range(D_MODEL // SUB):
        cols = slice(s * SUB, (s + 1) * SUB)
        xo_ref[:, cols] = x_ref[:, cols] + mod_ref[0, gate_row:gate_row + 1, cols] * _bdot(a, w_ref[:, cols])
    if norm_rows is not None:
        shift_row, scale_row = norm_rows
        h = _modulated_norm(xo_ref[...], nw_ref[...], nmod_ref[0, shift_row:shift_row + 1, :],
                            nmod_ref[0, scale_row:scale_row + 1, :])
        h_ref[...] = h.astype(BF16)


def _resid_mm(a, w, x2d, mod, gate_row, tm, mod_row, norm, name):
    t, kk = a.shape
    d = w.shape[1]
    row = lambda i: (i, 0)
    mrow = lambda i: (mod_row(i, tm), 0, 0)
    in_specs = [pl.BlockSpec((tm, kk), row), _resident((kk, d)), pl.BlockSpec((tm, d), row),
                pl.BlockSpec((1, 6, d), mrow)]
    args = [a, w, x2d, mod]
    out_specs = [pl.BlockSpec((tm, d), row)]
    out_shape = [jax.ShapeDtypeStruct((t, d), F32)]
    norm_rows = None
    if norm is not None:
        nw, nmod, shift_row, scale_row = norm
        norm_rows = (shift_row, scale_row)
        in_specs += [pl.BlockSpec((1, d), lambda i: (0, 0)), pl.BlockSpec((1, 6, d), mrow)]
        args += [nw, nmod]
        out_specs.append(pl.BlockSpec((tm, d), row))
        out_shape.append(jax.ShapeDtypeStruct((t, d), BF16))
    res = pl.pallas_call(
        functools.partial(_resid_mm_kernel, gate_row=gate_row, norm_rows=norm_rows),
        grid=(t // tm,),
        in_specs=in_specs,
        out_specs=out_specs,
        out_shape=out_shape,
        compiler_params=_cparams("arbitrary"),
        name=name,
    )(*args)
    return (res[0], res[1]) if norm is not None else (res[0], None)


def _ffn_up_kernel(h_ref, wv_ref, wg_ref, cwv_ref, cwg_ref, cbv_ref, cbg_ref, o_ref, wv_s, wg_s, *, seq):
    @pl.when(pl.program_id(1) == 0)
    def _():
        wv_s[...] = wv_ref[...].astype(BF16)
        wg_s[...] = wg_ref[...].astype(BF16)

    h = h_ref[...]
    tm = h.shape[0]
    pos = lax.broadcasted_iota(jnp.int32, (tm, SUB), 0) % seq

    def conv(acc, cw, cb):
        prev = jnp.where(pos == 0, 0.0, pltpu.roll(acc, 1, 0))
        nxt = jnp.where(pos == seq - 1, 0.0, pltpu.roll(acc, tm - 1, 0))
        return cw[0:1, :] * prev + cw[1:2, :] * acc + cw[2:3, :] * nxt + cb[...]

    val = conv(_bdot(h, wv_s[...]), cwv_ref, cbv_ref)
    gt = conv(_bdot(h, wg_s[...]), cwg_ref, cbg_ref)
    o_ref[...] = (_silu(gt) * val).astype(BF16)


def _ffn_up(h, w_up, cw, cb, seq):
    t, d = h.shape
    tm = max(TM_UP, seq)
    assert tm % seq == 0
    nt = D_FF // SUB
    return pl.pallas_call(
        functools.partial(_ffn_up_kernel, seq=seq),
        grid=(nt, t // tm),
        in_specs=[
            pl.BlockSpec((tm, d), lambda j, i: (i, 0)),
            pl.BlockSpec((d, SUB), lambda j, i: (0, j)),
            pl.BlockSpec((d, SUB), lambda j, i: (0, nt + j)),
            pl.BlockSpec((3, SUB), lambda j, i: (0, j)),
            pl.BlockSpec((3, SUB), lambda j, i: (0, nt + j)),
            pl.BlockSpec((1, SUB), lambda j, i: (0, j)),
            pl.BlockSpec((1, SUB), lambda j, i: (0, nt + j)),
        ],
        out_specs=pl.BlockSpec((tm, SUB), lambda j, i: (i, j)),
        out_shape=jax.ShapeDtypeStruct((t, D_FF), BF16),
        scratch_shapes=[pltpu.VMEM((d, SUB), BF16), pltpu.VMEM((d, SUB), BF16)],
        compiler_params=_cparams("arbitrary", "arbitrary"),
        name="ffn_up_conv",
    )(h, w_up, w_up, cw, cw, cb, cb)


def _rope_tables(seq):
    t = np.arange(seq)
    row = (t // GRID_W).astype(np.float32)
    col = (t % GRID_W).astype(np.float32)
    n_freq = DIFF_HEAD_DIM // 4
    inv = jnp.asarray(ROPE_BASE, F32) ** (-jnp.arange(n_freq, dtype=F32) / n_freq)
    ang_r = jnp.asarray(row)[:, None] * inv
    ang_c = jnp.asarray(col)[:, None] * inv
    cr, sr, cc, sc = jnp.cos(ang_r), jnp.sin(ang_r), jnp.cos(ang_c), jnp.sin(ang_c)
    cos64 = jnp.concatenate([cr, cr, cc, cc], axis=-1)
    sin64 = jnp.concatenate([-sr, sr, -sc, sc], axis=-1)
    return jnp.tile(cos64, (1, 2)), jnp.tile(sin64, (1, 2))


def _na_bias_table(rpb):
    lead = rpb.shape[:-1]
    period = 2 * GRID_W
    n_side = NA_KW - 1
    r = rpb.astype(F32)
    base = jnp.concatenate([r[..., n_side:], jnp.zeros(lead + (period - 2 * n_side - 1,), F32),
                            r[..., :n_side]], axis=-1)
    flat = jnp.tile(base, (1,) * len(lead) + (GRID_W,))[..., :GRID_W * (period - 1)]
    toe = flat.reshape(lead + (GRID_W, period - 1))[..., :GRID_W]
    qc = np.arange(GRID_W)[:, None]
    kc = np.arange(GRID_W)[None, :]
    wstart = np.clip(qc - NA_KW // 2, 0, GRID_W - NA_KW)
    valid = (kc >= wstart) & (kc < wstart + NA_KW)
    tbl = jnp.where(valid, toe, -jnp.inf)
    tbl = jnp.swapaxes(tbl, -3, -2)
    return tbl.reshape(tbl.shape[:-2] + (-1,))


def _head_expand():
    e = np.zeros((2, LANES, SSD_INNER), np.float32)
    for d in range(2):
        for h in range(SSD_HEADS):
            e[d, d * SSD_HEADS + h, h * SSD_HEAD_DIM:(h + 1) * SSD_HEAD_DIM] = 1.0
    return jnp.asarray(e, BF16)


def _seg_ones():
    i = np.arange(LANES)
    return jnp.asarray((i[:, None] // 64 == i[None, :] // 64).astype(np.float32), BF16)


def _pad_lanes(v, n):
    return jnp.pad(v.reshape(1, -1), ((0, 0), (0, n - v.size)))


def _permute_w_in(w_in):
    o = np.cumsum([0, 1024 + 1536 + 32, 3 * HEAD_COLS, 3 * HEAD_COLS, 6144])
    ssd, wd, wn, g = [w_in[..., o[i]:o[i + 1]].astype(BF16) for i in range(4)]
    pad = jnp.zeros(w_in.shape[:-1] + (P1_COLS - COL_DT - 32,), BF16)
    return jnp.concatenate([g, ssd, pad, wd, wn], axis=-1)


W_CHUNK_DIFF = P1_COLS // P1_CHUNK
W_CHUNK_NA = W_CHUNK_DIFF + 1


def _layer_params(l, norm1_w, norm2_w, w_in, ssd_conv_w, ssd_conv_b, ssd_dt_bias, ssd_a_log, ssd_d, ssd_norm_w,
                  diff_q_norm, diff_k_norm, diff_lam, diff_subln_w, na_q_norm, na_k_norm,
                  w_branch_a, w_branch_b, w_branch_c, w_out, ffn_w_up, ffn_conv_w, ffn_conv_b, ffn_w_down):
    rep = HEAD_COLS // DIFF_HEAD_DIM
    return {
        "norm1_w": norm1_w[l].reshape(1, -1), "norm2_w": norm2_w[l].reshape(1, -1), "w_p": w_in,
        "diff_qn": jnp.tile(diff_q_norm[l], rep).reshape(1, -1), "diff_kn": jnp.tile(diff_k_norm[l], rep).reshape(1, -1),
        "na_qn": jnp.tile(na_q_norm[l], rep).reshape(1, -1), "na_kn": jnp.tile(na_k_norm[l], rep).reshape(1, -1),
        "cw_xs": ssd_conv_w[l][:, :SSD_INNER], "cw_bc": ssd_conv_w[l][:, SSD_INNER:],
        "cb_xs": ssd_conv_b[l][:SSD_INNER].reshape(1, -1), "cb_bc": ssd_conv_b[l][SSD_INNER:].reshape(1, -1),
        "dt_bias": _pad_lanes(ssd_dt_bias[l], LANES), "a_log": _pad_lanes(ssd_a_log[l], LANES),
        "d_exp": jnp.repeat(ssd_d[l], SSD_HEAD_DIM).reshape(1, -1), "ssd_nw": ssd_norm_w[l].reshape(1, -1),
        "e_heads": _head_expand(),
        "diff_lam": diff_lam[l], "subln_w": diff_subln_w[l].reshape(1, -1),
        "w_a": w_branch_a[l].astype(BF16), "w_b": w_branch_b[l].astype(BF16), "w_c": w_branch_c[l].astype(BF16),
        "w_out": w_out[l].astype(BF16), "w_up": ffn_w_up[l], "w_down": ffn_w_down[l].astype(BF16),
        "ffn_cw": ffn_conv_w[l], "ffn_cb": ffn_conv_b[l].reshape(1, -1),
    }


def _trunk_layer(x2d, h1, nb, seq, mod, mod_row, pw, layer, ctx_out, cache, consts, next_norm):
    u1 = _proj1(h1, pw["w_p"], layer)
    lam_init = 0.8 - 0.6 * math.exp(-0.3 * layer)
    bd = consts["bd"]
    if cache is None:
        depth, dk, dv, nk, nv, st = ctx_out
        qd, dk, dv = _qkv(h1, pw["w_p"], W_CHUNK_DIFF, pw["diff_qn"], pw["diff_kn"], bd, nb, seq, layer, (depth, dk, dv),
                          "proj_diff_qkv")
        qn, nk, nv = _qkv(h1, pw["w_p"], W_CHUNK_NA, pw["na_qn"], pw["na_kn"], bd, nb, seq, layer, (depth, nk, nv),
                          "proj_na_qkv")
        y_a, st = _ssd(u1, nb, seq, pw, None, layer, (depth, st))
        y_b = _ctx_attn(qd, dk, dv, nb, seq, layer, [pw["diff_lam"], pw["subln_w"]],
                        functools.partial(_ctx_diff_kernel, lam_init=lam_init), "ctx_diff_attn")
        y_c = _ctx_attn(qn, nk, nv, nb, seq, layer, [], _ctx_softmax_kernel, "ctx_softmax_attn")
        ctx_out = (depth, dk, dv, nk, nv, st)
    else:
        cdk, cdv, cnk, cnv, h0 = cache
        qd, kd, vd = _qkv(h1, pw["w_p"], W_CHUNK_DIFF, pw["diff_qn"], pw["diff_kn"], bd, nb, seq, layer, None,
                          "proj_diff_qkv")
        qn, kn, vn = _qkv(h1, pw["w_p"], W_CHUNK_NA, pw["na_qn"], pw["na_kn"], bd, nb, seq, layer, None, "proj_na_qkv")
        y_a, _ = _ssd(u1, nb, seq, pw, h0, layer, None)
        y_b = _lat_diff(qd, kd, vd, nb, seq, cdk, cdv, layer, consts["cos"], consts["sin"], pw["diff_lam"],
                        pw["subln_w"], lam_init)
        y_c = _lat_na(qn, kn, vn, nb, seq, cnk, cnv, layer, pw["na_table"])
    merged = _merge(y_a, y_b, y_c, pw["w_a"], pw["w_b"], pw["w_c"], u1)
    x2d, h2 = _resid_mm(merged, pw["w_out"], x2d, mod, 2, TM_OUT, mod_row, (pw["norm2_w"], mod, 3, 4), "out_proj")
    act = _ffn_up(h2, pw["w_up"], pw["ffn_cw"], pw["ffn_cb"], seq)
    x2d, h1_next = _resid_mm(act, pw["w_down"], x2d, mod, 5, TM_DOWN, mod_row, next_norm, "ffn_down")
    return x2d, h1_next, ctx_out


def kernel(x_prompt, x_sample, c, cache_diff_k, cache_diff_v, cache_na_k, cache_na_v, state_ssm, c_ctx, norm1_w, norm2_w, w_ada, b_ada, w_in, ssd_conv_w, ssd_conv_b, ssd_dt_bias, ssd_a_log, ssd_d, ssd_norm_w, diff_q_norm, diff_k_norm, diff_lam, diff_subln_w, na_q_norm, na_k_norm, na_rpb, w_branch_a, w_branch_b, w_branch_c, w_out, ffn_w_up, ffn_conv_w, ffn_conv_b, ffn_w_down):
    batch, seq, d = x_prompt.shape
    dec_batch, dec_seq, _ = x_sample.shape
    depth = w_in.shape[0]
    past = cache_diff_k.shape[2]
    assert d == D_MODEL and dec_batch + 1 <= 8 and seq % SSD_CHUNK == 0 and dec_seq % (NA_KH * GRID_W) == 0

    c_rows = jnp.concatenate([c_ctx.reshape(1, d), c, jnp.zeros((8 - 1 - dec_batch, d), F32)], axis=0)
    mod_all = _ada(c_rows, w_ada, b_ada).reshape(depth, 8, 6, d)

    cos, sin = _rope_tables(dec_seq)
    consts = {"bd": _seg_ones(), "cos": cos, "sin": sin}
    na_tables = _na_bias_table(na_rpb)
    cdk = cache_diff_k.reshape(dec_batch, depth, past, HEAD_COLS)
    cdv = cache_diff_v.reshape(dec_batch, depth, past, HEAD_COLS)
    cnk = cache_na_k.reshape(dec_batch, depth, past, HEAD_COLS)
    cnv = cache_na_v.reshape(dec_batch, depth, past, HEAD_COLS)
    h0 = state_ssm.reshape(dec_batch, depth, 2, SSD_INNER, SSD_STATE)

    w_in_b = _permute_w_in(w_in)
    pws = []
    for l in range(depth):
        pw = _layer_params(l, norm1_w, norm2_w, w_in_b, ssd_conv_w, ssd_conv_b, ssd_dt_bias, ssd_a_log, ssd_d,
                           ssd_norm_w, diff_q_norm, diff_k_norm, diff_lam, diff_subln_w, na_q_norm, na_k_norm,
                           w_branch_a, w_branch_b, w_branch_c, w_out, ffn_w_up, ffn_conv_w, ffn_conv_b, ffn_w_down)
        pw["na_table"] = na_tables[l]
        pws.append(pw)

    ctx_row = lambda i, tm: 0
    lat_row = lambda i, tm: 1 + (i * tm) // dec_seq
    y_p = x_prompt.reshape(batch * seq, d)
    y_s = x_sample.reshape(dec_batch * dec_seq, d)
    h_p = _norm(y_p, pws[0]["norm1_w"], mod_all[0], ctx_row)
    h_s = _norm(y_s, pws[0]["norm1_w"], mod_all[0], lat_row)
    ctx_out = (depth, None, None, None, None, None)
    cache = (cdk, cdv, cnk, cnv, h0)
    for l in range(depth):
        next_norm = (pws[l + 1]["norm1_w"], mod_all[l + 1], 0, 1) if l + 1 < depth else None
        y_p, h_p, ctx_out = _trunk_layer(y_p, h_p, batch, seq, mod_all[l], ctx_row, pws[l], l, ctx_out, None,
                                         consts, next_norm)
        y_s, h_s, _ = _trunk_layer(y_s, h_s, dec_batch, dec_seq, mod_all[l], lat_row, pws[l], l, None, cache,
                                   consts, next_norm)
    _, dk, dv, nk, nv, st = ctx_out
    return (y_p.reshape(batch, seq, d), y_s.reshape(dec_batch, dec_seq, d),
            dk.reshape(batch, depth, seq, DIFF_HEADS, 2, DIFF_HEAD_DIM),
            dv.reshape(batch, depth, seq, DIFF_HEADS, 2 * DIFF_HEAD_DIM),
            nk.reshape(batch, depth, seq, NA_HEADS, NA_HEAD_DIM),
            nv.reshape(batch, depth, seq, NA_HEADS, NA_HEAD_DIM),
            st.reshape(batch, depth, 2, SSD_HEADS, SSD_HEAD_DIM, SSD_STATE))
```

```python
import functools
import math

import numpy as np
import jax
import jax.numpy as jnp
from jax import lax
from jax.experimental import pallas as pl
from jax.experimental.pallas import tpu as pltpu

F32 = jnp.float32
BF16 = jnp.bfloat16

D_MODEL = 2048
GRID_W = 64
SSD_INNER = 1024
SSD_HEAD_DIM = 64
SSD_HEADS = 16
SSD_GROUPS = 2
SSD_STATE = 128
SSD_CHUNK = 128
SSD_BC = 2 * SSD_GROUPS * SSD_STATE
DIFF_HEADS = 8
DIFF_HEAD_DIM = 64
NA_HEADS = 16
NA_HEAD_DIM = 64
NA_KH = 8
NA_KW = 16
HEAD_COLS = 1024
D_FF = 5632
ROPE_BASE = 10000.0
EPS = 1e-6

LANES = 128
VMEM_LIMIT_BYTES = 56 * 1024 * 1024
SUB = 512

COL_G = 0
COL_Z = 6144
COL_XS = 7168
COL_BC = 8192
COL_DT = 8704
P1_COLS = 9216
P1_CHUNK = 3072
N_GATE_CHUNKS = COL_Z // P1_CHUNK

TM_NORM = 1024
TM_PROJ = 512
TM_MERGE = 512
TM_OUT = 512
TM_UP = 1024
TM_DOWN = 256


def _cparams(*sem):
    return pltpu.CompilerParams(dimension_semantics=sem, vmem_limit_bytes=VMEM_LIMIT_BYTES)


def _resident(shape, layer):
    return pl.BlockSpec((None,) + tuple(shape), lambda *_: (layer, 0, 0), pipeline_mode=pl.Buffered(1))


def _sigmoid(x):
    return 1.0 / (1.0 + jnp.exp(-x))


def _silu(x):
    return x * _sigmoid(x)


def _bdot(a, b):
    return jnp.dot(a, b, preferred_element_type=F32)


def _bdot_nt(a, b):
    return lax.dot_general(a, b, (((1,), (1,)), ((), ())), preferred_element_type=F32)


def _split3(x):
    p1 = x.astype(BF16)
    r1 = x - p1.astype(F32)
    p2 = r1.astype(BF16)
    p3 = (r1 - p2.astype(F32)).astype(BF16)
    return p1, p2, p3


def _sel_right(x, e):
    p1, p2, p3 = _split3(x)
    return _bdot(p1, e) + _bdot(p2, e) + _bdot(p3, e)


def _sel_left(t, x):
    p1, p2, p3 = _split3(x)
    return _bdot(t, p1) + _bdot(t, p2) + _bdot(t, p3)


def _seg64_rms(a, bd):
    sq = a * a
    hi = sq.astype(BF16)
    lo = (sq - hi.astype(F32)).astype(BF16)
    ss = _bdot(hi, bd) + _bdot(lo, bd)
    return a * lax.rsqrt(ss * (1.0 / 64.0) + EPS)


def _modulated_norm(x, nw, shift, scale):
    ms = jnp.mean(x * x, axis=-1, keepdims=True)
    y = x * lax.rsqrt(ms + EPS) * nw
    return y * (1.0 + scale) + shift


def _ada_kernel(c_ref, w_ref, b_ref, o_ref):
    c = c_ref[...]
    s = _silu(c).astype(BF16)
    o_ref[0] = _bdot(s, w_ref[0].astype(BF16)) + b_ref[0]


def _ada(c_rows, w_ada, b_ada):
    depth, d, n = w_ada.shape
    tn = 1024
    return pl.pallas_call(
        _ada_kernel,
        grid=(depth, n // tn),
        in_specs=[
            pl.BlockSpec((8, d), lambda l, j: (0, 0)),
            pl.BlockSpec((1, d, tn), lambda l, j: (l, 0, j)),
            pl.BlockSpec((1, 1, tn), lambda l, j: (l, 0, j)),
        ],
        out_specs=pl.BlockSpec((1, 8, tn), lambda l, j: (l, 0, j)),
        out_shape=jax.ShapeDtypeStruct((depth, 8, n), F32),
        compiler_params=_cparams("arbitrary", "arbitrary"),
        name="ada_mod",
    )(c_rows, w_ada, b_ada.reshape(depth, 1, n))


def _norm_kernel(x_ref, nw_ref, mod_ref, h_ref):
    h_ref[...] = _modulated_norm(x_ref[...], nw_ref[...], mod_ref[0, 0:1, :], mod_ref[0, 1:2, :]).astype(BF16)


def _norm(x2d, nw, mod, mod_row):
    t, d = x2d.shape
    tm = TM_NORM
    return pl.pallas_call(
        _norm_kernel,
        grid=(t // tm,),
        in_specs=[
            pl.BlockSpec((tm, d), lambda i: (i, 0)),
            pl.BlockSpec((1, d), lambda i: (0, 0)),
            pl.BlockSpec((1, 6, d), lambda i: (mod_row(i, tm), 0, 0)),
        ],
        out_specs=pl.BlockSpec((tm, d), lambda i: (i, 0)),
        out_shape=jax.ShapeDtypeStruct((t, d), BF16),
        compiler_params=_cparams("arbitrary"),
        name="norm_mod",
    )(x2d, nw, mod)


def _proj1_kernel(h_ref, wg_ref, ws_ref, o_ref):
    c = pl.program_id(0)
    h = h_ref[...]

    def tiles(w_ref, epilogue):
        for s in range(P1_CHUNK // SUB):
            cols = slice(s * SUB, (s + 1) * SUB)
            o_ref[:, cols] = epilogue(_bdot(h, w_ref[:, cols]))

    @pl.when(c < N_GATE_CHUNKS)
    def _():
        tiles(wg_ref, _sigmoid)

    @pl.when(c >= N_GATE_CHUNKS)
    def _():
        tiles(ws_ref, lambda a: a)


def _proj1(h, w_gate, w_ssd, layer):
    t, d = h.shape
    tm = TM_PROJ
    return pl.pallas_call(
        _proj1_kernel,
        grid=(P1_COLS // P1_CHUNK, t // tm),
        in_specs=[
            pl.BlockSpec((tm, d), lambda c, i: (i, 0)),
            pl.BlockSpec((None, d, P1_CHUNK), lambda c, i: (layer, 0, jnp.minimum(c, N_GATE_CHUNKS - 1)),
                         pipeline_mode=pl.Buffered(1)),
            _resident((d, P1_CHUNK), layer),
        ],
        out_specs=pl.BlockSpec((tm, P1_CHUNK), lambda c, i: (i, c)),
        out_shape=jax.ShapeDtypeStruct((t, P1_COLS), F32),
        compiler_params=_cparams("arbitrary", "arbitrary"),
        name="proj_gates_ssd",
    )(h, w_gate, w_ssd)


def _qkv_kernel(*refs, n_alias):
    h_ref, w_ref, qnw_ref, knw_ref, bd_ref = refs[:5]
    q_ref, k_ref, v_ref = refs[5 + n_alias:]
    h = h_ref[...]
    bd = bd_ref[...]
    for sec, (o_ref, nw_ref) in enumerate(((q_ref, qnw_ref), (k_ref, knw_ref), (v_ref, None))):
        for s in range(HEAD_COLS // SUB):
            acc = _bdot(h, w_ref[:, sec * HEAD_COLS + s * SUB: sec * HEAD_COLS + (s + 1) * SUB])
            if nw_ref is not None:
                parts = [_seg64_rms(acc[:, c * LANES:(c + 1) * LANES], bd) for c in range(SUB // LANES)]
                acc = jnp.concatenate(parts, axis=1) * nw_ref[:, s * SUB:(s + 1) * SUB]
            o_ref[..., s * SUB:(s + 1) * SUB] = acc.reshape(o_ref.shape[:-1] + (SUB,)).astype(o_ref.dtype)


def _qkv(h, w, qnw, knw, bd, nb, seq, layer, kv_prev, name):
    t, d = h.shape
    tm = TM_PROJ
    in_specs = [
        pl.BlockSpec((tm, d), lambda i: (i, 0)),
        _resident((d, 3 * HEAD_COLS), layer),
        pl.BlockSpec((1, HEAD_COLS), lambda i: (0, 0)),
        pl.BlockSpec((1, HEAD_COLS), lambda i: (0, 0)),
        pl.BlockSpec((LANES, LANES), lambda i: (0, 0)),
    ]
    args = [h, w, qnw, knw, bd]
    q_spec = pl.BlockSpec((tm, HEAD_COLS), lambda i: (i, 0))
    q_shape = jax.ShapeDtypeStruct((t, HEAD_COLS), F32 if kv_prev is None else BF16)
    aliases = {}
    n_alias = 0
    if kv_prev is None:
        kv_spec, kv_shape = q_spec, q_shape
    else:
        depth, k_prev, v_prev = kv_prev
        assert tm % seq == 0
        kv_spec = pl.BlockSpec((tm // seq, None, seq, HEAD_COLS), lambda i: (i, layer, 0, 0))
        kv_shape = jax.ShapeDtypeStruct((nb, depth, seq, HEAD_COLS), F32)
        if k_prev is not None:
            in_specs += [pl.BlockSpec(memory_space=pl.ANY)] * 2
            args += [k_prev, v_prev]
            aliases = {5: 1, 6: 2}
            n_alias = 2
    return pl.pallas_call(
        functools.partial(_qkv_kernel, n_alias=n_alias),
        grid=(t // tm,),
        in_specs=in_specs,
        out_specs=[q_spec, kv_spec, kv_spec],
        out_shape=[q_shape, kv_shape, kv_shape],
        input_output_aliases=aliases,
        compiler_params=_cparams("arbitrary"),
        name=name,
    )(*args)


def _ssd_kernel(*refs, seq, has_h0, has_state_out, n_alias):
    (z_ref, xs_ref, bc_ref, dt_ref, cwx_ref, cbx_ref, cwb_ref, cbb_ref, dtb_ref, alog_ref,
     dexp_ref, nw_ref, e_ref) = refs[:13]
    rest = list(refs[13:])
    h0_ref = rest.pop(0) if has_h0 else None
    rest = rest[n_alias:]
    y_ref = rest.pop(0)
    st_ref = rest.pop(0) if has_state_out else None
    xs_s, bc_s, ya_s, st_s = rest

    q = SSD_CHUNK
    nc = seq // q
    half = SSD_INNER // SSD_GROUPS

    def conv_chunk(c, carry):
        r0 = pl.multiple_of(c * q, q)
        for src, dst, cw, cb in ((xs_ref, xs_s, cwx_ref, cbx_ref), (bc_ref, bc_s, cwb_ref, cbb_ref)):
            x = src[pl.ds(r0, q), :]
            xp = src[pl.ds(jnp.maximum(r0 - 1, 0), 1), :]
            xn = src[pl.ds(jnp.minimum(r0 + q, seq - 1), 1), :]
            xp = jnp.where(c == 0, 0.0, xp)
            xn = jnp.where(c == nc - 1, 0.0, xn)
            rows = lax.broadcasted_iota(jnp.int32, x.shape, 0)
            prev = jnp.where(rows == 0, xp, pltpu.roll(x, 1, 0))
            nxt = jnp.where(rows == q - 1, xn, pltpu.roll(x, q - 1, 0))
            y = cw[0:1, :] * prev + cw[1:2, :] * x + cw[2:3, :] * nxt + cb[...]
            dst[pl.ds(r0, q), :] = _silu(y)
        return carry

    lax.fori_loop(0, nc, conv_chunk, 0)

    a_vec = -jnp.exp(alog_ref[...])
    dt_bias = dtb_ref[...]
    ri = lax.broadcasted_iota(jnp.int32, (q, q), 0)
    ci = lax.broadcasted_iota(jnp.int32, (q, q), 1)

    for d in (0, 1):
        tri = (ri >= ci) if d == 0 else (ci >= ri)
        tri_b = jnp.where(tri, 1.0, 0.0).astype(BF16)
        for g in range(SSD_GROUPS):
            if has_h0:
                st_s[g] = h0_ref[d, g * half:(g + 1) * half, :].T
            else:
                st_s[g] = jnp.zeros((SSD_STATE, half), F32)

        def chunk(i, carry, d=d, tri=tri, tri_b=tri_b):
            c = i if d == 0 else nc - 1 - i
            r0 = pl.multiple_of(c * q, q)
            x_dt = dt_ref[pl.ds(r0, q), :] + dt_bias
            dtc = jnp.maximum(x_dt, 0.0) + jnp.log1p(jnp.exp(-jnp.abs(x_dt)))
            la = dtc * a_vec
            cum = _sel_left(tri_b, la)
            cum_t = cum.T
            e = e_ref[d]
            dt_x = _sel_right(dtc, e)
            cum_x = _sel_right(cum, e)
            last = q - 1 if d == 0 else 0
            cl = cum_x[last:last + 1, :]
            xd = xs_s[pl.ds(r0, q), :] * dt_x
            xdb = xd.astype(BF16)
            xdd = (xd * jnp.exp(cl - cum_x)).astype(BF16)
            ecum = jnp.exp(cum_x)
            bcv = bc_s[pl.ds(r0, q), :]
            for g in range(SSD_GROUPS):
                bg = bcv[:, g * SSD_STATE:(g + 1) * SSD_STATE]
                cg = bcv[:, (SSD_GROUPS + g) * SSD_STATE:(SSD_GROUPS + g + 1) * SSD_STATE]
                bgb = bg.astype(BF16)
                cgb = cg.astype(BF16)
                gm = _bdot_nt(cgb, bgb)
                st = st_s[g]
                y_off = _bdot(cgb, st.astype(BF16)) * ecum[:, g * half:(g + 1) * half]
                ys = []
                for hh in range(SSD_HEADS // SSD_GROUPS):
                    h = g * (SSD_HEADS // SSD_GROUPS) + hh
                    k = d * SSD_HEADS + h
                    decay = jnp.where(tri, jnp.exp(cum[:, k:k + 1] - cum_t[k:k + 1, :]), 0.0)
                    m = (gm * decay).astype(BF16)
                    ys.append(_bdot(m, xdb[:, h * SSD_HEAD_DIM:(h + 1) * SSD_HEAD_DIM]))
                yg = jnp.concatenate(ys, axis=1) + y_off
                if d == 0:
                    ya_s[pl.ds(r0, q), g * half:(g + 1) * half] = yg
                else:
                    ya_s[pl.ds(r0, q), g * half:(g + 1) * half] += yg
                st_s[g] = (st * jnp.exp(cl[:, g * half:(g + 1) * half])
                           + _bdot(bg.T.astype(BF16), xdd[:, g * half:(g + 1) * half]))
            return carry

        lax.fori_loop(0, nc, chunk, 0)
        if has_state_out:
            for g in range(SSD_GROUPS):
                st_ref[d, g * half:(g + 1) * half, :] = st_s[g].T

    def fin_chunk(c, carry):
        r0 = pl.multiple_of(c * q, q)
        y = ya_s[pl.ds(r0, q), :] + dexp_ref[...] * xs_s[pl.ds(r0, q), :]
        y = y * _silu(z_ref[pl.ds(r0, q), :])
        for g in range(SSD_GROUPS):
            v = y[:, g * half:(g + 1) * half]
            ms = jnp.mean(v * v, axis=-1, keepdims=True)
            out = v * lax.rsqrt(ms + EPS) * nw_ref[:, g * half:(g + 1) * half]
            y_ref[pl.ds(r0, q), g * half:(g + 1) * half] = out.astype(BF16)
        return carry

    lax.fori_loop(0, nc, fin_chunk, 0)


def _ssd(u1, nb, seq, pw, h0, layer, state_prev):
    has_h0 = h0 is not None
    has_state_out = state_prev is not None
    c1 = lambda b: (0, 0)
    in_specs = [
        pl.BlockSpec((seq, SSD_INNER), lambda b: (b, COL_Z // SSD_INNER)),
        pl.BlockSpec((seq, SSD_INNER), lambda b: (b, COL_XS // SSD_INNER)),
        pl.BlockSpec((seq, SSD_BC), lambda b: (b, COL_BC // SSD_BC)),
        pl.BlockSpec((seq, LANES), lambda b: (b, COL_DT // LANES)),
        pl.BlockSpec((3, SSD_INNER), c1),
        pl.BlockSpec((1, SSD_INNER), c1),
        pl.BlockSpec((3, SSD_BC), c1),
        pl.BlockSpec((1, SSD_BC), c1),
        pl.BlockSpec((1, LANES), c1),
        pl.BlockSpec((1, LANES), c1),
        pl.BlockSpec((1, SSD_INNER), c1),
        pl.BlockSpec((1, SSD_INNER), c1),
        pl.BlockSpec((2, LANES, SSD_INNER), lambda b: (0, 0, 0)),
    ]
    args = [u1, u1, u1, u1, pw["cw_xs"], pw["cb_xs"], pw["cw_bc"], pw["cb_bc"], pw["dt_bias"], pw["a_log"],
            pw["d_exp"], pw["ssd_nw"], pw["e_heads"]]
    if has_h0:
        in_specs.append(pl.BlockSpec((None, None, 2, SSD_INNER, SSD_STATE), lambda b: (b, layer, 0, 0, 0)))
        args.append(h0)
    out_specs = [pl.BlockSpec((seq, SSD_INNER), lambda b: (b, 0))]
    out_shape = [jax.ShapeDtypeStruct((nb * seq, SSD_INNER), BF16)]
    aliases = {}
    n_alias = 0
    if has_state_out:
        depth, prev = state_prev
        out_specs.append(pl.BlockSpec((None, None, 2, SSD_INNER, SSD_STATE), lambda b: (b, layer, 0, 0, 0)))
        out_shape.append(jax.ShapeDtypeStruct((nb, depth, 2, SSD_INNER, SSD_STATE), F32))
        if prev is not None:
            aliases = {len(args): 1}
            in_specs.append(pl.BlockSpec(memory_space=pl.ANY))
            args.append(prev)
            n_alias = 1
    res = pl.pallas_call(
        functools.partial(_ssd_kernel, seq=seq, has_h0=has_h0, has_state_out=has_state_out, n_alias=n_alias),
        grid=(nb,),
        in_specs=in_specs,
        out_specs=out_specs,
        out_shape=out_shape,
        input_output_aliases=aliases,
        scratch_shapes=[
            pltpu.VMEM((seq, SSD_INNER), F32),
            pltpu.VMEM((seq, SSD_BC), F32),
            pltpu.VMEM((seq, SSD_INNER), F32),
            pltpu.VMEM((SSD_GROUPS, SSD_STATE, SSD_INNER // SSD_GROUPS), F32),
        ],
        compiler_params=_cparams("arbitrary"),
        name="ssd_bidir",
    )(*args)
    return res if has_state_out else (res[0], None)


def _softmax_rows(parts):
    m = functools.reduce(jnp.maximum, [jnp.max(s, axis=-1, keepdims=True) for s in parts])
    es = [jnp.exp(s - m) for s in parts]
    den = functools.reduce(lambda a, b: a + b, [jnp.sum(e, axis=-1, keepdims=True) for e in es])
    inv = 1.0 / den
    return [e * inv for e in es]


def _exp_rows(parts):
    m = functools.reduce(jnp.maximum, [jnp.max(s, axis=-1, keepdims=True) for s in parts])
    return [jnp.exp(s - m).astype(BF16) for s in parts]


def _lane_halves(x):
    left = lax.broadcasted_iota(jnp.int32, x.shape, 1) < LANES // 2
    return jnp.where(left, x, 0.0).astype(BF16), jnp.where(left, 0.0, x).astype(BF16)


def _lambda(lp, lam_init):
    a = jnp.sum(lp[0:1, :] * lp[1:2, :], axis=-1, keepdims=True)
    b = jnp.sum(lp[2:3, :] * lp[3:4, :], axis=-1, keepdims=True)
    return jnp.exp(a) - jnp.exp(b) + lam_init


def _subln(o, w, lam_init):
    ms = jnp.mean(o * o, axis=-1, keepdims=True)
    return o * lax.rsqrt(ms + EPS) * w * (1.0 - lam_init)


def _rope(x, cos, sin):
    lane = lax.broadcasted_iota(jnp.int32, x.shape, 1)
    swapped = jnp.where((lane & 16) == 0, pltpu.roll(x, LANES - 16, 1), pltpu.roll(x, 16, 1))
    return x * cos + swapped * sin


def _ctx_diff_kernel(q_ref, k_ref, v_ref, lamp_ref, sw_ref, o_ref, *, lam_init):
    scale = DIFF_HEAD_DIM ** -0.5
    lam = _lambda(lamp_ref[...], lam_init)
    sw = sw_ref[...]
    for hb in range(DIFF_HEADS):
        ps = []
        for t in (0, 1):
            sl = slice(hb * LANES + t * DIFF_HEAD_DIM, hb * LANES + (t + 1) * DIFF_HEAD_DIM)
            s = _bdot_nt(q_ref[:, sl], k_ref[:, sl].astype(BF16)) * scale
            ps.append(_softmax_rows([s])[0])
        att = (ps[0] - lam * ps[1]).astype(BF16)
        blk = slice(hb * LANES, (hb + 1) * LANES)
        o = _subln(_bdot(att, v_ref[:, blk].astype(BF16)), sw, lam_init)
        o_ref[:, blk] = o.astype(BF16)


def _ctx_softmax_kernel(q_ref, k_ref, v_ref, o_ref):
    scale = NA_HEAD_DIM ** -0.5
    ones_l, ones_r = _lane_halves(jnp.ones((k_ref.shape[0], LANES), F32))
    for hb in range(NA_HEADS // 2):
        blk = slice(hb * LANES, (hb + 1) * LANES)
        v_l, v_r = _lane_halves(v_ref[:, blk])
        es = []
        for t in (0, 1):
            sl = slice(hb * LANES + t * NA_HEAD_DIM, hb * LANES + (t + 1) * NA_HEAD_DIM)
            s = _bdot_nt(q_ref[:, sl], k_ref[:, sl].astype(BF16)) * scale
            es.append(_exp_rows([s])[0])
        num = _bdot(es[0], v_l) + _bdot(es[1], v_r)
        den = _bdot(es[0], ones_l) + _bdot(es[1], ones_r)
        o_ref[:, blk] = (num / den).astype(BF16)


def _ctx_attn(q, k_all, v_all, nb, seq, layer, extra, kern, name):
    kv_spec = pl.BlockSpec((None, None, seq, HEAD_COLS), lambda b: (b, layer, 0, 0))
    extra_specs = [pl.BlockSpec(a.shape, lambda b: (0, 0)) for a in extra]
    return pl.pallas_call(
        kern,
        grid=(nb,),
        in_specs=[pl.BlockSpec((seq, HEAD_COLS), lambda b: (b, 0)), kv_spec, kv_spec] + extra_specs,
        out_specs=pl.BlockSpec((seq, HEAD_COLS), lambda b: (b, 0)),
        out_shape=jax.ShapeDtypeStruct((nb * seq, HEAD_COLS), BF16),
        compiler_params=_cparams("arbitrary"),
        name=name,
    )(q, k_all, v_all, *extra)


LAT_DIFF_TQ = 256


def _lat_diff_kernel(q_ref, k_ref, v_ref, ck_ref, cv_ref, cos_ref, sin_ref, lamp_ref, sw_ref, o_ref, *, lam_init):
    scale = DIFF_HEAD_DIM ** -0.5
    k = _rope(k_ref[...], cos_ref[...], sin_ref[...])
    k_t = [k[:, t * DIFF_HEAD_DIM:(t + 1) * DIFF_HEAD_DIM].astype(BF16) for t in (0, 1)]
    ck_t = [ck_ref[:, t * DIFF_HEAD_DIM:(t + 1) * DIFF_HEAD_DIM].astype(BF16) for t in (0, 1)]
    v = v_ref[...].astype(BF16)
    cv = cv_ref[...].astype(BF16)
    lam = _lambda(lamp_ref[...], lam_init)
    sw = sw_ref[...]
    for qb in range(q_ref.shape[0] // LAT_DIFF_TQ):
        rs = slice(qb * LAT_DIFF_TQ, (qb + 1) * LAT_DIFF_TQ)
        q = _rope(q_ref[rs, :], cos_ref[rs, :], sin_ref[rs, :])
        pl_, pc_ = [], []
        for t in (0, 1):
            qt = q[:, t * DIFF_HEAD_DIM:(t + 1) * DIFF_HEAD_DIM].astype(BF16)
            p_loc, p_ctx = _softmax_rows([_bdot_nt(qt, k_t[t]) * scale, _bdot_nt(qt, ck_t[t]) * scale])
            pl_.append(p_loc)
            pc_.append(p_ctx)
        a_loc = (pl_[0] - lam * pl_[1]).astype(BF16)
        a_ctx = (pc_[0] - lam * pc_[1]).astype(BF16)
        o = _bdot(a_loc, v) + _bdot(a_ctx, cv)
        o_ref[rs, :] = _subln(o, sw, lam_init).astype(BF16)


def _lat_diff(q, k, v, nb, seq, cache_k, cache_v, layer, cos, sin, lamp, sw, lam_init):
    past = cache_k.shape[2]
    cache_spec = pl.BlockSpec((None, None, past, LANES), lambda b, h: (b, layer, 0, h))
    loc_spec = pl.BlockSpec((seq, LANES), lambda b, h: (b, h))
    tab_spec = pl.BlockSpec((seq, LANES), lambda b, h: (0, 0))
    return pl.pallas_call(
        functools.partial(_lat_diff_kernel, lam_init=lam_init),
        grid=(nb, DIFF_HEADS),
        in_specs=[
            loc_spec, loc_spec, loc_spec, cache_spec, cache_spec, tab_spec, tab_spec,
            pl.BlockSpec((4, DIFF_HEAD_DIM), lambda b, h: (0, 0)),
            pl.BlockSpec((1, LANES), lambda b, h: (0, 0)),
        ],
        out_specs=loc_spec,
        out_shape=jax.ShapeDtypeStruct((nb * seq, HEAD_COLS), BF16),
        compiler_params=_cparams("arbitrary", "arbitrary"),
        name="lat_diff_attn",
    )(q, k, v, cache_k, cache_v, cos, sin, lamp, sw)


NA_Q_ROWS = 4


def _lat_na_kernel(q_ref, k_ref, v_ref, ck_ref, cv_ref, tb_ref, o_ref, bias_s, *, rows):
    scale = NA_HEAD_DIM ** -0.5
    kh = min(NA_KH, rows)
    win0 = lambda qr: min(max(qr - kh // 2, 0), rows - kh)

    @pl.when(pl.program_id(1) == 0)
    def _():
        bias_s[...] = jnp.full(bias_s.shape, -jnp.inf, F32)
        for t in (0, 1):
            for qr in range(rows):
                r0 = win0(qr)
                a0 = r0 - qr + NA_KH - 1
                bias_s[t, qr * GRID_W:(qr + 1) * GRID_W, r0 * GRID_W:(r0 + kh) * GRID_W] = (
                    tb_ref[t, :, a0 * GRID_W:(a0 + kh) * GRID_W])

    k_h = [k_ref[:, t * NA_HEAD_DIM:(t + 1) * NA_HEAD_DIM].astype(BF16) for t in (0, 1)]
    ck_h = [ck_ref[:, t * NA_HEAD_DIM:(t + 1) * NA_HEAD_DIM].astype(BF16) for t in (0, 1)]
    v_h = _lane_halves(v_ref[...])
    cv_h = _lane_halves(cv_ref[...])
    ones_h = lambda n: _lane_halves(jnp.ones((n, LANES), F32))
    cones_h = ones_h(cv_ref.shape[0])
    for qb in range(rows // NA_Q_ROWS):
        lo = win0(qb * NA_Q_ROWS) // 2 * 2
        hi = min(rows, (win0((qb + 1) * NA_Q_ROWS - 1) + kh + 1) // 2 * 2)
        rs = slice(qb * NA_Q_ROWS * GRID_W, (qb + 1) * NA_Q_ROWS * GRID_W)
        ks = slice(lo * GRID_W, hi * GRID_W)
        num = den = None
        for t in (0, 1):
            qh = q_ref[rs, t * NA_HEAD_DIM:(t + 1) * NA_HEAD_DIM].astype(BF16)
            s_loc = _bdot_nt(qh, k_h[t][ks]) * scale + bias_s[t, rs, ks]
            s_ctx = _bdot_nt(qh, ck_h[t]) * scale
            e_loc, e_ctx = _exp_rows([s_loc, s_ctx])
            n_t = _bdot(e_loc, v_h[t][ks]) + _bdot(e_ctx, cv_h[t])
            d_t = _bdot(e_loc, ones_h((hi - lo) * GRID_W)[t]) + _bdot(e_ctx, cones_h[t])
            num = n_t if num is None else num + n_t
            den = d_t if den is None else den + d_t
        o_ref[rs, :] = (num / den).astype(BF16)


def _lat_na(q, k, v, nb, seq, cache_k, cache_v, layer, table):
    past = cache_k.shape[2]
    rows = seq // GRID_W
    ncol = table.shape[-1]
    loc_spec = pl.BlockSpec((seq, LANES), lambda h, b: (b, h))
    cache_spec = pl.BlockSpec((None, None, past, LANES), lambda h, b: (b, layer, 0, h))
    return pl.pallas_call(
        functools.partial(_lat_na_kernel, rows=rows),
        grid=(NA_HEADS // 2, nb),
        in_specs=[loc_spec, loc_spec, loc_spec, cache_spec, cache_spec,
                  pl.BlockSpec((2, GRID_W, ncol), lambda h, b: (h, 0, 0))],
        out_specs=pl.BlockSpec((seq, LANES), lambda h, b: (b, h)),
        out_shape=jax.ShapeDtypeStruct((nb * seq, HEAD_COLS), BF16),
        scratch_shapes=[pltpu.VMEM((2, seq, seq), F32)],
        compiler_params=_cparams("arbitrary", "arbitrary"),
        name="lat_nbr_attn",
    )(q, k, v, cache_k, cache_v, table)


def _merge_kernel(ya_ref, yb_ref, yc_ref, wa_ref, wb_ref, wc_ref, ga_ref, gb_ref, gc_ref, o_ref):
    ya, yb, yc = ya_ref[...], yb_ref[...], yc_ref[...]
    for s in range(D_MODEL // SUB):
        cols = slice(s * SUB, (s + 1) * SUB)
        m = (ga_ref[:, cols] * _bdot(ya, wa_ref[:, cols])
             + gb_ref[:, cols] * _bdot(yb, wb_ref[:, cols])
             + gc_ref[:, cols] * _bdot(yc, wc_ref[:, cols]))
        o_ref[:, cols] = m.astype(BF16)


def _merge(ya, yb, yc, wa, wb, wc, u1, layer):
    t, kk = ya.shape
    n = wa.shape[2]
    tm = TM_MERGE
    ysp = pl.BlockSpec((tm, kk), lambda i: (i, 0))
    wsp = _resident((kk, n), layer)
    return pl.pallas_call(
        _merge_kernel,
        grid=(t // tm,),
        in_specs=[ysp, ysp, ysp, wsp, wsp, wsp,
                  pl.BlockSpec((tm, n), lambda i: (i, 0)),
                  pl.BlockSpec((tm, n), lambda i: (i, 1)),
                  pl.BlockSpec((tm, n), lambda i: (i, 2))],
        out_specs=pl.BlockSpec((tm, n), lambda i: (i, 0)),
        out_shape=jax.ShapeDtypeStruct((t, n), BF16),
        compiler_params=_cparams("arbitrary"),
        name="branch_merge",
    )(ya, yb, yc, wa, wb, wc, u1, u1, u1)


def _resid_mm_kernel(*refs, gate_row, norm_rows):
    if norm_rows is None:
        a_ref, w_ref, x_ref, mod_ref, xo_ref = refs
    else:
        a_ref, w_ref, x_ref, mod_ref, nw_ref, nmod_ref, xo_ref, h_ref = refs
    a = a_ref[...]
    for s in range(D_MODEL // SUB):
        cols = slice(s * SUB, (s + 1) * SUB)
        xo_ref[:, cols] = x_ref[:, cols] + mod_ref[0, gate_row:gate_row + 1, cols] * _bdot(a, w_ref[:, cols])
    if norm_rows is not None:
        shift_row, scale_row = norm_rows
        h = _modulated_norm(xo_ref[...], nw_ref[...], nmod_ref[0, shift_row:shift_row + 1, :],
                            nmod_ref[0, scale_row:scale_row + 1, :])
        h_ref[...] = h.astype(BF16)


def _resid_mm(a, w, layer, x2d, mod, gate_row, tm, mod_row, norm, name):
    t, kk = a.shape
    d = w.shape[2]
    row = lambda i: (i, 0)
    mrow = lambda i: (mod_row(i, tm), 0, 0)
    in_specs = [pl.BlockSpec((tm, kk), row), _resident((kk, d), layer), pl.BlockSpec((tm, d), row),
                pl.BlockSpec((1, 6, d), mrow)]
    args = [a, w, x2d, mod]
    out_specs = [pl.BlockSpec((tm, d), row)]
    out_shape = [jax.ShapeDtypeStruct((t, d), F32)]
    norm_rows = None
    if norm is not None:
        nw, nmod, shift_row, scale_row = norm
        norm_rows = (shift_row, scale_row)
        in_specs += [pl.BlockSpec((1, d), lambda i: (0, 0)), pl.BlockSpec((1, 6, d), mrow)]
        args += [nw, nmod]
        out_specs.append(pl.BlockSpec((tm, d), row))
        out_shape.append(jax.ShapeDtypeStruct((t, d), BF16))
    res = pl.pallas_call(
        functools.partial(_resid_mm_kernel, gate_row=gate_row, norm_rows=norm_rows),
        grid=(t // tm,),
        in_specs=in_specs,
        out_specs=out_specs,
        out_shape=out_shape,
        compiler_params=_cparams("arbitrary"),
        name=name,
    )(*args)
    return (res[0], res[1]) if norm is not None else (res[0], None)


def _ffn_up_kernel(h_ref, wv_ref, wg_ref, cwv_ref, cwg_ref, cbv_ref, cbg_ref, o_ref, wv_s, wg_s, *, seq):
    @pl.when(pl.program_id(1) == 0)
    def _():
        wv_s[...] = wv_ref[...].astype(BF16)
        wg_s[...] = wg_ref[...].astype(BF16)

    h = h_ref[...]
    tm = h.shape[0]
    pos = lax.broadcasted_iota(jnp.int32, (tm, SUB), 0) % seq

    def conv(acc, cw, cb):
        prev = jnp.where(pos == 0, 0.0, pltpu.roll(acc, 1, 0))
        nxt = jnp.where(pos == seq - 1, 0.0, pltpu.roll(acc, tm - 1, 0))
        return cw[0:1, :] * prev + cw[1:2, :] * acc + cw[2:3, :] * nxt + cb[...]

    val = conv(_bdot(h, wv_s[...]), cwv_ref, cbv_ref)
    gt = conv(_bdot(h, wg_s[...]), cwg_ref, cbg_ref)
    o_ref[...] = (_silu(gt) * val).astype(BF16)


def _ffn_up(h, w_up, layer, cw, cb, seq):
    t, d = h.shape
    tm = max(TM_UP, seq)
    assert tm % seq == 0
    nt = D_FF // SUB
    return pl.pallas_call(
        functools.partial(_ffn_up_kernel, seq=seq),
        grid=(nt, t // tm),
        in_specs=[
            pl.BlockSpec((tm, d), lambda j, i: (i, 0)),
            pl.BlockSpec((None, d, SUB), lambda j, i: (layer, 0, j)),
            pl.BlockSpec((None, d, SUB), lambda j, i: (layer, 0, nt + j)),
            pl.BlockSpec((3, SUB), lambda j, i: (0, j)),
            pl.BlockSpec((3, SUB), lambda j, i: (0, nt + j)),
            pl.BlockSpec((1, SUB), lambda j, i: (0, j)),
            pl.BlockSpec((1, SUB), lambda j, i: (0, nt + j)),
        ],
        out_specs=pl.BlockSpec((tm, SUB), lambda j, i: (i, j)),
        out_shape=jax.ShapeDtypeStruct((t, D_FF), BF16),
        scratch_shapes=[pltpu.VMEM((d, SUB), BF16), pltpu.VMEM((d, SUB), BF16)],
        compiler_params=_cparams("arbitrary", "arbitrary"),
        name="ffn_up_conv",
    )(h, w_up, w_up, cw, cw, cb, cb)


def _rope_tables(seq):
    t = np.arange(seq)
    row = (t // GRID_W).astype(np.float32)
    col = (t % GRID_W).astype(np.float32)
    n_freq = DIFF_HEAD_DIM // 4
    inv = jnp.asarray(ROPE_BASE, F32) ** (-jnp.arange(n_freq, dtype=F32) / n_freq)
    ang_r = jnp.asarray(row)[:, None] * inv
    ang_c = jnp.asarray(col)[:, None] * inv
    cr, sr, cc, sc = jnp.cos(ang_r), jnp.sin(ang_r), jnp.cos(ang_c), jnp.sin(ang_c)
    cos64 = jnp.concatenate([cr, cr, cc, cc], axis=-1)
    sin64 = jnp.concatenate([-sr, sr, -sc, sc], axis=-1)
    return jnp.tile(cos64, (1, 2)), jnp.tile(sin64, (1, 2))


def _na_bias_table(rpb):
    qc = np.arange(GRID_W)[:, None]
    kc = np.arange(GRID_W)[None, :]
    onehot = (kc - qc + NA_KW - 1 == np.arange(2 * NA_KW - 1)[:, None, None]).astype(np.float32)
    wstart = np.clip(qc - NA_KW // 2, 0, GRID_W - NA_KW)
    valid = (kc >= wstart) & (kc < wstart + NA_KW)
    toe = jnp.einsum("...ab,bqk->...qak", rpb.astype(F32), jnp.asarray(onehot), precision=lax.Precision.HIGHEST)
    tbl = jnp.where(valid[:, None, :], toe, -jnp.inf)
    return tbl.reshape(tbl.shape[:-2] + (-1,))


def _head_expand():
    e = np.zeros((2, LANES, SSD_INNER), np.float32)
    for d in range(2):
        for h in range(SSD_HEADS):
            e[d, d * SSD_HEADS + h, h * SSD_HEAD_DIM:(h + 1) * SSD_HEAD_DIM] = 1.0
    return jnp.asarray(e, BF16)


def _seg_ones():
    i = np.arange(LANES)
    return jnp.asarray((i[:, None] // 64 == i[None, :] // 64).astype(np.float32), BF16)


def _pad_lanes(v, n):
    return jnp.pad(v.reshape(1, -1), ((0, 0), (0, n - v.size)))


def _split_w_in(w_in):
    o = np.cumsum([0, 1024 + 1536 + 32, 3 * HEAD_COLS, 3 * HEAD_COLS, 6144])
    ssd, wd, wn, g = [w_in[..., o[i]:o[i + 1]].astype(BF16) for i in range(4)]
    ssd = jnp.pad(ssd, ((0, 0), (0, 0), (0, P1_CHUNK - ssd.shape[-1])))
    return g, ssd, wd, wn


def _layer_params(l, norm1_w, norm2_w, ssd_conv_w, ssd_conv_b, ssd_dt_bias, ssd_a_log, ssd_d, ssd_norm_w,
                  diff_q_norm, diff_k_norm, diff_lam, diff_subln_w, na_q_norm, na_k_norm, ffn_conv_w, ffn_conv_b):
    rep = HEAD_COLS // DIFF_HEAD_DIM
    return {
        "norm1_w": norm1_w[l].reshape(1, -1), "norm2_w": norm2_w[l].reshape(1, -1),
        "diff_qn": jnp.tile(diff_q_norm[l], rep).reshape(1, -1), "diff_kn": jnp.tile(diff_k_norm[l], rep).reshape(1, -1),
        "na_qn": jnp.tile(na_q_norm[l], rep).reshape(1, -1), "na_kn": jnp.tile(na_k_norm[l], rep).reshape(1, -1),
        "cw_xs": ssd_conv_w[l][:, :SSD_INNER], "cw_bc": ssd_conv_w[l][:, SSD_INNER:],
        "cb_xs": ssd_conv_b[l][:SSD_INNER].reshape(1, -1), "cb_bc": ssd_conv_b[l][SSD_INNER:].reshape(1, -1),
        "dt_bias": _pad_lanes(ssd_dt_bias[l], LANES), "a_log": _pad_lanes(ssd_a_log[l], LANES),
        "d_exp": jnp.repeat(ssd_d[l], SSD_HEAD_DIM).reshape(1, -1), "ssd_nw": ssd_norm_w[l].reshape(1, -1),
        "e_heads": _head_expand(),
        "diff_lam": diff_lam[l], "subln_w": diff_subln_w[l].reshape(1, -1),
        "ffn_cw": ffn_conv_w[l], "ffn_cb": ffn_conv_b[l].reshape(1, -1),
    }


def _trunk_layer(x2d, h1, nb, seq, mod, mod_row, pw, layer, ctx_out, cache, consts, next_norm):
    wts = consts["weights"]
    u1 = _proj1(h1, wts["gate"], wts["ssd"], layer)
    lam_init = 0.8 - 0.6 * math.exp(-0.3 * layer)
    bd = consts["bd"]
    if cache is None:
        depth, dk, dv, nk, nv, st = ctx_out
        qd, dk, dv = _qkv(h1, wts["diff"], pw["diff_qn"], pw["diff_kn"], bd, nb, seq, layer, (depth, dk, dv),
                          "proj_diff_qkv")
        qn, nk, nv = _qkv(h1, wts["na"], pw["na_qn"], pw["na_kn"], bd, nb, seq, layer, (depth, nk, nv),
                          "proj_na_qkv")
        y_a, st = _ssd(u1, nb, seq, pw, None, layer, (depth, st))
        y_b = _ctx_attn(qd, dk, dv, nb, seq, layer, [pw["diff_lam"], pw["subln_w"]],
                        functools.partial(_ctx_diff_kernel, lam_init=lam_init), "ctx_diff_attn")
        y_c = _ctx_attn(qn, nk, nv, nb, seq, layer, [], _ctx_softmax_kernel, "ctx_softmax_attn")
        ctx_out = (depth, dk, dv, nk, nv, st)
    else:
        cdk, cdv, cnk, cnv, h0 = cache
        qd, kd, vd = _qkv(h1, wts["diff"], pw["diff_qn"], pw["diff_kn"], bd, nb, seq, layer, None, "proj_diff_qkv")
        qn, kn, vn = _qkv(h1, wts["na"], pw["na_qn"], pw["na_kn"], bd, nb, seq, layer, None, "proj_na_qkv")
        y_a, _ = _ssd(u1, nb, seq, pw, h0, layer, None)
        y_b = _lat_diff(qd, kd, vd, nb, seq, cdk, cdv, layer, consts["cos"], consts["sin"], pw["diff_lam"],
                        pw["subln_w"], lam_init)
        y_c = _lat_na(qn, kn, vn, nb, seq, cnk, cnv, layer, pw["na_table"])
    merged = _merge(y_a, y_b, y_c, wts["a"], wts["b"], wts["c"], u1, layer)
    x2d, h2 = _resid_mm(merged, wts["out"], layer, x2d, mod, 2, TM_OUT, mod_row, (pw["norm2_w"], mod, 3, 4),
                        "out_proj")
    act = _ffn_up(h2, wts["up"], layer, pw["ffn_cw"], pw["ffn_cb"], seq)
    x2d, h1_next = _resid_mm(act, wts["down"], layer, x2d, mod, 5, TM_DOWN, mod_row, next_norm, "ffn_down")
    return x2d, h1_next, ctx_out


def kernel(x_prompt, x_sample, c, cache_diff_k, cache_diff_v, cache_na_k, cache_na_v, state_ssm, c_ctx, norm1_w, norm2_w, w_ada, b_ada, w_in, ssd_conv_w, ssd_conv_b, ssd_dt_bias, ssd_a_log, ssd_d, ssd_norm_w, diff_q_norm, diff_k_norm, diff_lam, diff_subln_w, na_q_norm, na_k_norm, na_rpb, w_branch_a, w_branch_b, w_branch_c, w_out, ffn_w_up, ffn_conv_w, ffn_conv_b, ffn_w_down):
    batch, seq, d = x_prompt.shape
    dec_batch, dec_seq, _ = x_sample.shape
    depth = w_in.shape[0]
    past = cache_diff_k.shape[2]
    assert d == D_MODEL and dec_batch + 1 <= 8 and seq % SSD_CHUNK == 0 and dec_seq % (NA_KH * GRID_W) == 0

    c_rows = jnp.concatenate([c_ctx.reshape(1, d), c, jnp.zeros((8 - 1 - dec_batch, d), F32)], axis=0)
    mod_all = _ada(c_rows, w_ada, b_ada).reshape(depth, 8, 6, d)

    cos, sin = _rope_tables(dec_seq)
    consts = {"bd": _seg_ones(), "cos": cos, "sin": sin}
    na_tables = _na_bias_table(na_rpb)
    cdk = cache_diff_k.reshape(dec_batch, depth, past, HEAD_COLS)
    cdv = cache_diff_v.reshape(dec_batch, depth, past, HEAD_COLS)
    cnk = cache_na_k.reshape(dec_batch, depth, past, HEAD_COLS)
    cnv = cache_na_v.reshape(dec_batch, depth, past, HEAD_COLS)
    h0 = state_ssm.reshape(dec_batch, depth, 2, SSD_INNER, SSD_STATE)

    w_gate, w_ssd, w_diff, w_na = _split_w_in(w_in)
    consts["weights"] = {
        "gate": w_gate, "ssd": w_ssd, "diff": w_diff, "na": w_na,
        "a": w_branch_a.astype(BF16), "b": w_branch_b.astype(BF16), "c": w_branch_c.astype(BF16),
        "out": w_out.astype(BF16), "up": ffn_w_up, "down": ffn_w_down.astype(BF16),
    }
    pws = []
    for l in range(depth):
        pw = _layer_params(l, norm1_w, norm2_w, ssd_conv_w, ssd_conv_b, ssd_dt_bias, ssd_a_log, ssd_d,
                           ssd_norm_w, diff_q_norm, diff_k_norm, diff_lam, diff_subln_w, na_q_norm, na_k_norm,
                           ffn_conv_w, ffn_conv_b)
        pw["na_table"] = na_tables[l]
        pws.append(pw)

    ctx_row = lambda i, tm: 0
    lat_row = lambda i, tm: 1 + (i * tm) // dec_seq
    y_p = x_prompt.reshape(batch * seq, d)
    y_s = x_sample.reshape(dec_batch * dec_seq, d)
    h_p = _norm(y_p, pws[0]["norm1_w"], mod_all[0], ctx_row)
    h_s = _norm(y_s, pws[0]["norm1_w"], mod_all[0], lat_row)
    ctx_out = (depth, None, None, None, None, None)
    cache = (cdk, cdv, cnk, cnv, h0)
    for l in range(depth):
        next_norm = (pws[l + 1]["norm1_w"], mod_all[l + 1], 0, 1) if l + 1 < depth else None
        y_p, h_p, ctx_out = _trunk_layer(y_p, h_p, batch, seq, mod_all[l], ctx_row, pws[l], l, ctx_out, None,
                                         consts, next_norm)
        y_s, h_s, _ = _trunk_layer(y_s, h_s, dec_batch, dec_seq, mod_all[l], lat_row, pws[l], l, None, cache,
                                   consts, next_norm)
    _, dk, dv, nk, nv, st = ctx_out
    return (y_p.reshape(batch, seq, d), y_s.reshape(dec_batch, dec_seq, d),
            dk.reshape(batch, depth, seq, DIFF_HEADS, 2, DIFF_HEAD_DIM),
            dv.reshape(batch, depth, seq, DIFF_HEADS, 2 * DIFF_HEAD_DIM),
            nk.reshape(batch, depth, seq, NA_HEADS, NA_HEAD_DIM),
            nv.reshape(batch, depth, seq, NA_HEADS, NA_HEAD_DIM),
            st.reshape(batch, depth, 2, SSD_HEADS, SSD_HEAD_DIM, SSD_STATE))
```

```python
import functools
import math

import numpy as np
import jax
import jax.numpy as jnp
from jax import lax
from jax.experimental import pallas as pl
from jax.experimental.pallas import tpu as pltpu

F32 = jnp.float32
BF16 = jnp.bfloat16

D_MODEL = 2048
GRID_W = 64
SSD_INNER = 1024
SSD_HEAD_DIM = 64
SSD_HEADS = 16
SSD_GROUPS = 2
SSD_STATE = 128
SSD_CHUNK = 128
SSD_BC = 2 * SSD_GROUPS * SSD_STATE
DIFF_HEADS = 8
DIFF_HEAD_DIM = 64
NA_HEADS = 16
NA_HEAD_DIM = 64
NA_KH = 8
NA_KW = 16
HEAD_COLS = 1024
D_FF = 5632
ROPE_BASE = 10000.0
EPS = 1e-6

LANES = 128
VMEM_LIMIT_BYTES = 56 * 1024 * 1024
SUB = 512
SEG_BLOCK = 256

COL_G = 0
COL_Z = 6144
COL_XS = 7168
COL_BC = 8192
COL_DT = 8704
P1_COLS = 9216
P1_CHUNK = 3072
N_GATE_CHUNKS = COL_Z // P1_CHUNK

TM_NORM = 1024
TM_PROJ = 512
TM_MERGE = 512
TM_OUT = 512
TM_UP = 1024
TM_DOWN = 256


def _cparams(*sem):
    return pltpu.CompilerParams(dimension_semantics=sem, vmem_limit_bytes=VMEM_LIMIT_BYTES)


def _resident(shape, layer):
    return pl.BlockSpec((None,) + tuple(shape), lambda *_: (layer, 0, 0), pipeline_mode=pl.Buffered(1))


def _sigmoid(x):
    return 1.0 / (1.0 + jnp.exp(-x))


def _silu(x):
    return x * _sigmoid(x)


def _bdot(a, b):
    return jnp.dot(a, b, preferred_element_type=F32)


def _bdot_nt(a, b):
    return lax.dot_general(a, b, (((1,), (1,)), ((), ())), preferred_element_type=F32)


def _split3(x):
    p1 = x.astype(BF16)
    r1 = x - p1.astype(F32)
    p2 = r1.astype(BF16)
    p3 = (r1 - p2.astype(F32)).astype(BF16)
    return p1, p2, p3


def _sel_right(x, e):
    p1, p2, _ = _split3(x)
    return _bdot(p1, e) + _bdot(p2, e)


def _sel_left(t, x):
    p1, p2, p3 = _split3(x)
    return _bdot(t, p1) + _bdot(t, p2) + _bdot(t, p3)


def _seg64_rms(a, bd):
    ss = _bdot((a * a).astype(BF16), bd)
    return a * lax.rsqrt(ss * (1.0 / 64.0) + EPS)


def _modulated_norm(x, nw, shift, scale):
    ms = jnp.mean(x * x, axis=-1, keepdims=True)
    y = x * lax.rsqrt(ms + EPS) * nw
    return y * (1.0 + scale) + shift


def _ada_kernel(c_ref, w_ref, b_ref, o_ref):
    c = c_ref[...]
    s = _silu(c).astype(BF16)
    o_ref[0] = _bdot(s, w_ref[0].astype(BF16)) + b_ref[0]


def _ada(c_rows, w_ada, b_ada):
    depth, d, n = w_ada.shape
    tn = 1024
    return pl.pallas_call(
        _ada_kernel,
        grid=(depth, n // tn),
        in_specs=[
            pl.BlockSpec((8, d), lambda l, j: (0, 0)),
            pl.BlockSpec((1, d, tn), lambda l, j: (l, 0, j)),
            pl.BlockSpec((1, 1, tn), lambda l, j: (l, 0, j)),
        ],
        out_specs=pl.BlockSpec((1, 8, tn), lambda l, j: (l, 0, j)),
        out_shape=jax.ShapeDtypeStruct((depth, 8, n), F32),
        compiler_params=_cparams("arbitrary", "arbitrary"),
        name="ada_mod",
    )(c_rows, w_ada, b_ada.reshape(depth, 1, n))


def _norm_kernel(x_ref, nw_ref, mod_ref, h_ref):
    h_ref[...] = _modulated_norm(x_ref[...], nw_ref[...], mod_ref[0, 0:1, :], mod_ref[0, 1:2, :]).astype(BF16)


def _norm(x2d, nw, mod, mod_row):
    t, d = x2d.shape
    tm = TM_NORM
    return pl.pallas_call(
        _norm_kernel,
        grid=(t // tm,),
        in_specs=[
            pl.BlockSpec((tm, d), lambda i: (i, 0)),
            pl.BlockSpec((1, d), lambda i: (0, 0)),
            pl.BlockSpec((1, 6, d), lambda i: (mod_row(i, tm), 0, 0)),
        ],
        out_specs=pl.BlockSpec((tm, d), lambda i: (i, 0)),
        out_shape=jax.ShapeDtypeStruct((t, d), BF16),
        compiler_params=_cparams("arbitrary"),
        name="norm_mod",
    )(x2d, nw, mod)


def _proj1_kernel(h_ref, wg_ref, ws_ref, o_ref):
    c = pl.program_id(0)
    h = h_ref[...]

    def tiles(w_ref, epilogue):
        for s in range(P1_CHUNK // SUB):
            cols = slice(s * SUB, (s + 1) * SUB)
            o_ref[:, cols] = epilogue(_bdot(h, w_ref[:, cols]))

    @pl.when(c < N_GATE_CHUNKS)
    def _():
        tiles(wg_ref, _sigmoid)

    @pl.when(c >= N_GATE_CHUNKS)
    def _():
        tiles(ws_ref, lambda a: a)


def _proj1(h, w_gate, w_ssd, layer):
    t, d = h.shape
    tm = TM_PROJ
    return pl.pallas_call(
        _proj1_kernel,
        grid=(P1_COLS // P1_CHUNK, t // tm),
        in_specs=[
            pl.BlockSpec((tm, d), lambda c, i: (i, 0)),
            pl.BlockSpec((None, d, P1_CHUNK), lambda c, i: (layer, 0, jnp.minimum(c, N_GATE_CHUNKS - 1)),
                         pipeline_mode=pl.Buffered(1)),
            _resident((d, P1_CHUNK), layer),
        ],
        out_specs=pl.BlockSpec((tm, P1_CHUNK), lambda c, i: (i, c)),
        out_shape=jax.ShapeDtypeStruct((t, P1_COLS), F32),
        compiler_params=_cparams("arbitrary", "arbitrary"),
        name="proj_gates_ssd",
    )(h, w_gate, w_ssd)


def _qkv_kernel(*refs, n_alias):
    h_ref, w_ref, qnw_ref, knw_ref, bd_ref = refs[:5]
    q_ref, k_ref, v_ref = refs[5 + n_alias:]
    h = h_ref[...]
    bd = bd_ref[...]
    for sec, (o_ref, nw_ref) in enumerate(((q_ref, qnw_ref), (k_ref, knw_ref), (v_ref, None))):
        for s in range(HEAD_COLS // SUB):
            acc = _bdot(h, w_ref[:, sec * HEAD_COLS + s * SUB: sec * HEAD_COLS + (s + 1) * SUB])
            if nw_ref is not None:
                parts = [_seg64_rms(acc[:, c * SEG_BLOCK:(c + 1) * SEG_BLOCK], bd) for c in range(SUB // SEG_BLOCK)]
                acc = jnp.concatenate(parts, axis=1) * nw_ref[:, s * SUB:(s + 1) * SUB]
            o_ref[..., s * SUB:(s + 1) * SUB] = acc.reshape(o_ref.shape[:-1] + (SUB,)).astype(o_ref.dtype)


def _qkv(h, w, qnw, knw, bd, nb, seq, layer, kv_prev, name):
    t, d = h.shape
    tm = TM_PROJ
    in_specs = [
        pl.BlockSpec((tm, d), lambda i: (i, 0)),
        _resident((d, 3 * HEAD_COLS), layer),
        pl.BlockSpec((1, HEAD_COLS), lambda i: (0, 0)),
        pl.BlockSpec((1, HEAD_COLS), lambda i: (0, 0)),
        pl.BlockSpec((SEG_BLOCK, SEG_BLOCK), lambda i: (0, 0)),
    ]
    args = [h, w, qnw, knw, bd]
    q_spec = pl.BlockSpec((tm, HEAD_COLS), lambda i: (i, 0))
    q_shape = jax.ShapeDtypeStruct((t, HEAD_COLS), F32 if kv_prev is None else BF16)
    aliases = {}
    n_alias = 0
    if kv_prev is None:
        kv_spec, kv_shape = q_spec, q_shape
    else:
        depth, k_prev, v_prev = kv_prev
        assert tm % seq == 0
        kv_spec = pl.BlockSpec((tm // seq, None, seq, HEAD_COLS), lambda i: (i, layer, 0, 0))
        kv_shape = jax.ShapeDtypeStruct((nb, depth, seq, HEAD_COLS), F32)
        if k_prev is not None:
            in_specs += [pl.BlockSpec(memory_space=pl.ANY)] * 2
            args += [k_prev, v_prev]
            aliases = {5: 1, 6: 2}
            n_alias = 2
    return pl.pallas_call(
        functools.partial(_qkv_kernel, n_alias=n_alias),
        grid=(t // tm,),
        in_specs=in_specs,
        out_specs=[q_spec, kv_spec, kv_spec],
        out_shape=[q_shape, kv_shape, kv_shape],
        input_output_aliases=aliases,
        compiler_params=_cparams("arbitrary"),
        name=name,
    )(*args)


def _ssd_kernel(*refs, seq, has_h0, has_state_out, n_alias):
    (z_ref, xs_ref, bc_ref, dt_ref, cwx_ref, cbx_ref, cwb_ref, cbb_ref, dtb_ref, alog_ref,
     dexp_ref, nw_ref, e_ref) = refs[:13]
    rest = list(refs[13:])
    h0_ref = rest.pop(0) if has_h0 else None
    rest = rest[n_alias:]
    y_ref = rest.pop(0)
    st_ref = rest.pop(0) if has_state_out else None
    xs_s, bc_s, ya_s, st_s = rest

    q = SSD_CHUNK
    nc = seq // q
    half = SSD_INNER // SSD_GROUPS

    def conv_chunk(c, carry):
        r0 = pl.multiple_of(c * q, q)
        for src, dst, cw, cb in ((xs_ref, xs_s, cwx_ref, cbx_ref), (bc_ref, bc_s, cwb_ref, cbb_ref)):
            x = src[pl.ds(r0, q), :]
            xp = src[pl.ds(jnp.maximum(r0 - 1, 0), 1), :]
            xn = src[pl.ds(jnp.minimum(r0 + q, seq - 1), 1), :]
            xp = jnp.where(c == 0, 0.0, xp)
            xn = jnp.where(c == nc - 1, 0.0, xn)
            rows = lax.broadcasted_iota(jnp.int32, x.shape, 0)
            prev = jnp.where(rows == 0, xp, pltpu.roll(x, 1, 0))
            nxt = jnp.where(rows == q - 1, xn, pltpu.roll(x, q - 1, 0))
            y = cw[0:1, :] * prev + cw[1:2, :] * x + cw[2:3, :] * nxt + cb[...]
            dst[pl.ds(r0, q), :] = _silu(y)
        return carry

    lax.fori_loop(0, nc, conv_chunk, 0)

    a_vec = -jnp.exp(alog_ref[...])
    dt_bias = dtb_ref[...]
    ri = lax.broadcasted_iota(jnp.int32, (q, q), 0)
    ci = lax.broadcasted_iota(jnp.int32, (q, q), 1)

    for d in (0, 1):
        tri = (ri >= ci) if d == 0 else (ci >= ri)
        tri_b = jnp.where(tri, 1.0, 0.0).astype(BF16)
        for g in range(SSD_GROUPS):
            if has_h0:
                st_s[g] = h0_ref[d, g * half:(g + 1) * half, :].T
            else:
                st_s[g] = jnp.zeros((SSD_STATE, half), F32)

        def chunk(i, carry, d=d, tri=tri, tri_b=tri_b):
            c = i if d == 0 else nc - 1 - i
            r0 = pl.multiple_of(c * q, q)
            x_dt = dt_ref[pl.ds(r0, q), :] + dt_bias
            dtc = jnp.maximum(x_dt, 0.0) + jnp.log1p(jnp.exp(-jnp.abs(x_dt)))
            la = dtc * a_vec
            cum = _sel_left(tri_b, la)
            cum_t = cum.T
            e = e_ref[d]
            dt_x = _sel_right(dtc, e)
            cum_x = _sel_right(cum, e)
            last = q - 1 if d == 0 else 0
            cl = cum_x[last:last + 1, :]
            xd = xs_s[pl.ds(r0, q), :] * dt_x
            xdb = xd.astype(BF16)
            xdd = (xd * jnp.exp(cl - cum_x)).astype(BF16)
            ecum = jnp.exp(cum_x)
            bcv = bc_s[pl.ds(r0, q), :]
            for g in range(SSD_GROUPS):
                bg = bcv[:, g * SSD_STATE:(g + 1) * SSD_STATE]
                cg = bcv[:, (SSD_GROUPS + g) * SSD_STATE:(SSD_GROUPS + g + 1) * SSD_STATE]
                bgb = bg.astype(BF16)
                cgb = cg.astype(BF16)
                gm = _bdot_nt(cgb, bgb)
                st = st_s[g]
                y_off = _bdot(cgb, st.astype(BF16)) * ecum[:, g * half:(g + 1) * half]
                ys = []
                for hh in range(SSD_HEADS // SSD_GROUPS):
                    h = g * (SSD_HEADS // SSD_GROUPS) + hh
                    k = d * SSD_HEADS + h
                    decay = jnp.where(tri, jnp.exp(cum[:, k:k + 1] - cum_t[k:k + 1, :]), 0.0)
                    m = (gm * decay).astype(BF16)
                    ys.append(_bdot(m, xdb[:, h * SSD_HEAD_DIM:(h + 1) * SSD_HEAD_DIM]))
                yg = jnp.concatenate(ys, axis=1) + y_off
                if d == 0:
                    ya_s[pl.ds(r0, q), g * half:(g + 1) * half] = yg
                else:
                    ya_s[pl.ds(r0, q), g * half:(g + 1) * half] += yg
                st_s[g] = (st * jnp.exp(cl[:, g * half:(g + 1) * half])
                           + _bdot(bg.T.astype(BF16), xdd[:, g * half:(g + 1) * half]))
            return carry

        lax.fori_loop(0, nc, chunk, 0)
        if has_state_out:
            for g in range(SSD_GROUPS):
                st_ref[d, g * half:(g + 1) * half, :] = st_s[g].T

    def fin_chunk(c, carry):
        r0 = pl.multiple_of(c * q, q)
        y = ya_s[pl.ds(r0, q), :] + dexp_ref[...] * xs_s[pl.ds(r0, q), :]
        y = y * _silu(z_ref[pl.ds(r0, q), :])
        for g in range(SSD_GROUPS):
            v = y[:, g * half:(g + 1) * half]
            ms = jnp.mean(v * v, axis=-1, keepdims=True)
            out = v * lax.rsqrt(ms + EPS) * nw_ref[:, g * half:(g + 1) * half]
            y_ref[pl.ds(r0, q), g * half:(g + 1) * half] = out.astype(BF16)
        return carry

    lax.fori_loop(0, nc, fin_chunk, 0)


def _ssd(u1, nb, seq, pw, h0, layer, state_prev):
    has_h0 = h0 is not None
    has_state_out = state_prev is not None
    c1 = lambda b: (0, 0)
    in_specs = [
        pl.BlockSpec((seq, SSD_INNER), lambda b: (b, COL_Z // SSD_INNER)),
        pl.BlockSpec((seq, SSD_INNER), lambda b: (b, COL_XS // SSD_INNER)),
        pl.BlockSpec((seq, SSD_BC), lambda b: (b, COL_BC // SSD_BC)),
        pl.BlockSpec((seq, LANES), lambda b: (b, COL_DT // LANES)),
        pl.BlockSpec((3, SSD_INNER), c1),
        pl.BlockSpec((1, SSD_INNER), c1),
        pl.BlockSpec((3, SSD_BC), c1),
        pl.BlockSpec((1, SSD_BC), c1),
        pl.BlockSpec((1, LANES), c1),
        pl.BlockSpec((1, LANES), c1),
        pl.BlockSpec((1, SSD_INNER), c1),
        pl.BlockSpec((1, SSD_INNER), c1),
        pl.BlockSpec((2, LANES, SSD_INNER), lambda b: (0, 0, 0)),
    ]
    args = [u1, u1, u1, u1, pw["cw_xs"], pw["cb_xs"], pw["cw_bc"], pw["cb_bc"], pw["dt_bias"], pw["a_log"],
            pw["d_exp"], pw["ssd_nw"], pw["e_heads"]]
    if has_h0:
        in_specs.append(pl.BlockSpec((None, None, 2, SSD_INNER, SSD_STATE), lambda b: (b, layer, 0, 0, 0)))
        args.append(h0)
    out_specs = [pl.BlockSpec((seq, SSD_INNER), lambda b: (b, 0))]
    out_shape = [jax.ShapeDtypeStruct((nb * seq, SSD_INNER), BF16)]
    aliases = {}
    n_alias = 0
    if has_state_out:
        depth, prev = state_prev
        out_specs.append(pl.BlockSpec((None, None, 2, SSD_INNER, SSD_STATE), lambda b: (b, layer, 0, 0, 0)))
        out_shape.append(jax.ShapeDtypeStruct((nb, depth, 2, SSD_INNER, SSD_STATE), F32))
        if prev is not None:
            aliases = {len(args): 1}
            in_specs.append(pl.BlockSpec(memory_space=pl.ANY))
            args.append(prev)
            n_alias = 1
    res = pl.pallas_call(
        functools.partial(_ssd_kernel, seq=seq, has_h0=has_h0, has_state_out=has_state_out, n_alias=n_alias),
        grid=(nb,),
        in_specs=in_specs,
        out_specs=out_specs,
        out_shape=out_shape,
        input_output_aliases=aliases,
        scratch_shapes=[
            pltpu.VMEM((seq, SSD_INNER), F32),
            pltpu.VMEM((seq, SSD_BC), F32),
            pltpu.VMEM((seq, SSD_INNER), F32),
            pltpu.VMEM((SSD_GROUPS, SSD_STATE, SSD_INNER // SSD_GROUPS), F32),
        ],
        compiler_params=_cparams("arbitrary"),
        name="ssd_bidir",
    )(*args)
    return res if has_state_out else (res[0], None)


def _softmax_rows(parts, mxu_sums=False):
    m = functools.reduce(jnp.maximum, [jnp.max(s, axis=-1, keepdims=True) for s in parts])
    es = [jnp.exp(s - m) for s in parts]
    if mxu_sums:
        dens = [_bdot(e.astype(BF16), jnp.ones((e.shape[1], SEG_BLOCK), BF16)) for e in es]
        inv = 1.0 / functools.reduce(lambda a, b: a + b, dens)
        return [e * jnp.tile(inv, (1, e.shape[1] // SEG_BLOCK)) for e in es]
    den = functools.reduce(lambda a, b: a + b, [jnp.sum(e, axis=-1, keepdims=True) for e in es])
    inv = 1.0 / den
    return [e * inv for e in es]


def _exp_rows(parts):
    m = functools.reduce(jnp.maximum, [jnp.max(s, axis=-1, keepdims=True) for s in parts])
    return [jnp.exp(s - m).astype(BF16) for s in parts]


def _lane_halves(x):
    left = lax.broadcasted_iota(jnp.int32, x.shape, 1) < LANES // 2
    return jnp.where(left, x, 0.0).astype(BF16), jnp.where(left, 0.0, x).astype(BF16)


def _lambda(lp, lam_init):
    a = jnp.sum(lp[0:1, :] * lp[1:2, :], axis=-1, keepdims=True)
    b = jnp.sum(lp[2:3, :] * lp[3:4, :], axis=-1, keepdims=True)
    return jnp.exp(a) - jnp.exp(b) + lam_init


def _subln(o, w, lam_init):
    ms = jnp.mean(o * o, axis=-1, keepdims=True)
    return o * lax.rsqrt(ms + EPS) * w * (1.0 - lam_init)


def _rope(x, cos, sin):
    lane = lax.broadcasted_iota(jnp.int32, x.shape, 1)
    swapped = jnp.where((lane & 16) == 0, pltpu.roll(x, LANES - 16, 1), pltpu.roll(x, 16, 1))
    return x * cos + swapped * sin


def _ctx_diff_kernel(q_ref, k_ref, v_ref, lamp_ref, sw_ref, o_ref, *, lam_init):
    scale = DIFF_HEAD_DIM ** -0.5
    lam = _lambda(lamp_ref[...], lam_init)
    sw = sw_ref[...]
    for hb in range(DIFF_HEADS):
        ps = []
        for t in (0, 1):
            sl = slice(hb * LANES + t * DIFF_HEAD_DIM, hb * LANES + (t + 1) * DIFF_HEAD_DIM)
            s = _bdot_nt(q_ref[:, sl], k_ref[:, sl].astype(BF16)) * scale
            ps.append(_softmax_rows([s], mxu_sums=True)[0])
        att = (ps[0] - lam * ps[1]).astype(BF16)
        blk = slice(hb * LANES, (hb + 1) * LANES)
        o = _subln(_bdot(att, v_ref[:, blk].astype(BF16)), sw, lam_init)
        o_ref[:, blk] = o.astype(BF16)


def _ctx_softmax_kernel(q_ref, k_ref, v_ref, o_ref):
    scale = NA_HEAD_DIM ** -0.5
    ones_l, ones_r = _lane_halves(jnp.ones((k_ref.shape[0], LANES), F32))
    for hb in range(NA_HEADS // 2):
        blk = slice(hb * LANES, (hb + 1) * LANES)
        v_l, v_r = _lane_halves(v_ref[:, blk])
        es = []
        for t in (0, 1):
            sl = slice(hb * LANES + t * NA_HEAD_DIM, hb * LANES + (t + 1) * NA_HEAD_DIM)
            s = _bdot_nt(q_ref[:, sl], k_ref[:, sl].astype(BF16)) * scale
            es.append(_exp_rows([s])[0])
        num = _bdot(es[0], v_l) + _bdot(es[1], v_r)
        den = _bdot(es[0], ones_l) + _bdot(es[1], ones_r)
        o_ref[:, blk] = (num / den).astype(BF16)


def _ctx_attn(q, k_all, v_all, nb, seq, layer, extra, kern, name):
    kv_spec = pl.BlockSpec((None, None, seq, HEAD_COLS), lambda b: (b, layer, 0, 0))
    extra_specs = [pl.BlockSpec(a.shape, lambda b: (0, 0)) for a in extra]
    return pl.pallas_call(
        kern,
        grid=(nb,),
        in_specs=[pl.BlockSpec((seq, HEAD_COLS), lambda b: (b, 0)), kv_spec, kv_spec] + extra_specs,
        out_specs=pl.BlockSpec((seq, HEAD_COLS), lambda b: (b, 0)),
        out_shape=jax.ShapeDtypeStruct((nb * seq, HEAD_COLS), BF16),
        compiler_params=_cparams("arbitrary"),
        name=name,
    )(q, k_all, v_all, *extra)


LAT_DIFF_TQ = 256


def _lat_diff_kernel(q_ref, k_ref, v_ref, ck_ref, cv_ref, cos_ref, sin_ref, lamp_ref, sw_ref, o_ref, *, lam_init):
    scale = DIFF_HEAD_DIM ** -0.5
    k = _rope(k_ref[...], cos_ref[...], sin_ref[...])
    k_t = [k[:, t * DIFF_HEAD_DIM:(t + 1) * DIFF_HEAD_DIM].astype(BF16) for t in (0, 1)]
    ck_t = [ck_ref[:, t * DIFF_HEAD_DIM:(t + 1) * DIFF_HEAD_DIM].astype(BF16) for t in (0, 1)]
    v = v_ref[...].astype(BF16)
    cv = cv_ref[...].astype(BF16)
    lam = _lambda(lamp_ref[...], lam_init)
    sw = sw_ref[...]
    for qb in range(q_ref.shape[0] // LAT_DIFF_TQ):
        rs = slice(qb * LAT_DIFF_TQ, (qb + 1) * LAT_DIFF_TQ)
        q = _rope(q_ref[rs, :], cos_ref[rs, :], sin_ref[rs, :])
        pl_, pc_ = [], []
        for t in (0, 1):
            qt = q[:, t * DIFF_HEAD_DIM:(t + 1) * DIFF_HEAD_DIM].astype(BF16)
            p_loc, p_ctx = _softmax_rows([_bdot_nt(qt, k_t[t]) * scale, _bdot_nt(qt, ck_t[t]) * scale])
            pl_.append(p_loc)
            pc_.append(p_ctx)
        a_loc = (pl_[0] - lam * pl_[1]).astype(BF16)
        a_ctx = (pc_[0] - lam * pc_[1]).astype(BF16)
        o = _bdot(a_loc, v) + _bdot(a_ctx, cv)
        o_ref[rs, :] = _subln(o, sw, lam_init).astype(BF16)


def _lat_diff(q, k, v, nb, seq, cache_k, cache_v, layer, cos, sin, lamp, sw, lam_init):
    past = cache_k.shape[2]
    cache_spec = pl.BlockSpec((None, None, past, LANES), lambda b, h: (b, layer, 0, h))
    loc_spec = pl.BlockSpec((seq, LANES), lambda b, h: (b, h))
    tab_spec = pl.BlockSpec((seq, LANES), lambda b, h: (0, 0))
    return pl.pallas_call(
        functools.partial(_lat_diff_kernel, lam_init=lam_init),
        grid=(nb, DIFF_HEADS),
        in_specs=[
            loc_spec, loc_spec, loc_spec, cache_spec, cache_spec, tab_spec, tab_spec,
            pl.BlockSpec((4, DIFF_HEAD_DIM), lambda b, h: (0, 0)),
            pl.BlockSpec((1, LANES), lambda b, h: (0, 0)),
        ],
        out_specs=loc_spec,
        out_shape=jax.ShapeDtypeStruct((nb * seq, HEAD_COLS), BF16),
        compiler_params=_cparams("arbitrary", "arbitrary"),
        name="lat_diff_attn",
    )(q, k, v, cache_k, cache_v, cos, sin, lamp, sw)


NA_Q_ROWS = 4


def _lat_na_kernel(q_ref, k_ref, v_ref, ck_ref, cv_ref, tb_ref, o_ref, bias_s, *, rows):
    scale = NA_HEAD_DIM ** -0.5
    kh = min(NA_KH, rows)
    win0 = lambda qr: min(max(qr - kh // 2, 0), rows - kh)

    @pl.when(pl.program_id(1) == 0)
    def _():
        bias_s[...] = jnp.full(bias_s.shape, -jnp.inf, F32)
        for t in (0, 1):
            for qr in range(rows):
                r0 = win0(qr)
                a0 = r0 - qr + NA_KH - 1
                bias_s[t, qr * GRID_W:(qr + 1) * GRID_W, r0 * GRID_W:(r0 + kh) * GRID_W] = (
                    tb_ref[t, :, a0 * GRID_W:(a0 + kh) * GRID_W])

    k_h = [k_ref[:, t * NA_HEAD_DIM:(t + 1) * NA_HEAD_DIM].astype(BF16) for t in (0, 1)]
    ck_h = [ck_ref[:, t * NA_HEAD_DIM:(t + 1) * NA_HEAD_DIM].astype(BF16) for t in (0, 1)]
    v_h = _lane_halves(v_ref[...])
    cv_h = _lane_halves(cv_ref[...])
    ones_h = lambda n: _lane_halves(jnp.ones((n, LANES), F32))
    cones_h = ones_h(cv_ref.shape[0])
    for qb in range(rows // NA_Q_ROWS):
        lo = win0(qb * NA_Q_ROWS) // 2 * 2
        hi = min(rows, (win0((qb + 1) * NA_Q_ROWS - 1) + kh + 1) // 2 * 2)
        rs = slice(qb * NA_Q_ROWS * GRID_W, (qb + 1) * NA_Q_ROWS * GRID_W)
        ks = slice(lo * GRID_W, hi * GRID_W)
        num = den = None
        for t in (0, 1):
            qh = q_ref[rs, t * NA_HEAD_DIM:(t + 1) * NA_HEAD_DIM].astype(BF16)
            s_loc = _bdot_nt(qh, k_h[t][ks]) * scale + bias_s[t, rs, ks]
            s_ctx = _bdot_nt(qh, ck_h[t]) * scale
            e_loc, e_ctx = _exp_rows([s_loc, s_ctx])
            n_t = _bdot(e_loc, v_h[t][ks]) + _bdot(e_ctx, cv_h[t])
            d_t = _bdot(e_loc, ones_h((hi - lo) * GRID_W)[t]) + _bdot(e_ctx, cones_h[t])
            num = n_t if num is None else num + n_t
            den = d_t if den is None else den + d_t
        o_ref[rs, :] = (num / den).astype(BF16)


def _lat_na(q, k, v, nb, seq, cache_k, cache_v, layer, table):
    past = cache_k.shape[2]
    rows = seq // GRID_W
    ncol = table.shape[-1]
    loc_spec = pl.BlockSpec((seq, LANES), lambda h, b: (b, h))
    cache_spec = pl.BlockSpec((None, None, past, LANES), lambda h, b: (b, layer, 0, h))
    return pl.pallas_call(
        functools.partial(_lat_na_kernel, rows=rows),
        grid=(NA_HEADS // 2, nb),
        in_specs=[loc_spec, loc_spec, loc_spec, cache_spec, cache_spec,
                  pl.BlockSpec((2, GRID_W, ncol), lambda h, b: (h, 0, 0))],
        out_specs=pl.BlockSpec((seq, LANES), lambda h, b: (b, h)),
        out_shape=jax.ShapeDtypeStruct((nb * seq, HEAD_COLS), BF16),
        scratch_shapes=[pltpu.VMEM((2, seq, seq), F32)],
        compiler_params=_cparams("arbitrary", "arbitrary"),
        name="lat_nbr_attn",
    )(q, k, v, cache_k, cache_v, table)


def _merge_kernel(ya_ref, yb_ref, yc_ref, wa_ref, wb_ref, wc_ref, ga_ref, gb_ref, gc_ref, o_ref):
    ya, yb, yc = ya_ref[...], yb_ref[...], yc_ref[...]
    for s in range(D_MODEL // SUB):
        cols = slice(s * SUB, (s + 1) * SUB)
        m = (ga_ref[:, cols] * _bdot(ya, wa_ref[:, cols])
             + gb_ref[:, cols] * _bdot(yb, wb_ref[:, cols])
             + gc_ref[:, cols] * _bdot(yc, wc_ref[:, cols]))
        o_ref[:, cols] = m.astype(BF16)


def _merge(ya, yb, yc, wa, wb, wc, u1, layer):
    t, kk = ya.shape
    n = wa.shape[2]
    tm = TM_MERGE
    ysp = pl.BlockSpec((tm, kk), lambda i: (i, 0))
    wsp = _resident((kk, n), layer)
    return pl.pallas_call(
        _merge_kernel,
        grid=(t // tm,),
        in_specs=[ysp, ysp, ysp, wsp, wsp, wsp,
                  pl.BlockSpec((tm, n), lambda i: (i, 0)),
                  pl.BlockSpec((tm, n), lambda i: (i, 1)),
                  pl.BlockSpec((tm, n), lambda i: (i, 2))],
        out_specs=pl.BlockSpec((tm, n), lambda i: (i, 0)),
        out_shape=jax.ShapeDtypeStruct((t, n), BF16),
        compiler_params=_cparams("arbitrary"),
        name="branch_merge",
    )(ya, yb, yc, wa, wb, wc, u1, u1, u1)


def _resid_mm_kernel(*refs, gate_row, norm_rows):
    if norm_rows is None:
        a_ref, w_ref, x_ref, mod_ref, xo_ref = refs
    else:
        a_ref, w_ref, x_ref, mod_ref, nw_ref, nmod_ref, xo_ref, h_ref = refs
    a = a_ref[...]
    for s in range(D_MODEL // SUB):
        cols = slice(s * SUB, (s + 1) * SUB)
        xo_ref[:, cols] = x_ref[:, cols] + mod_ref[0, gate_row:gate_row + 1, cols] * _bdot(a, w_ref[:, cols])
    if norm_rows is not None:
        shift_row, scale_row = norm_rows
        h = _modulated_norm(xo_ref[...], nw_ref[...], nmod_ref[0, shift_row:shift_row + 1, :],
                            nmod_ref[0, scale_row:scale_row + 1, :])
        h_ref[...] = h.astype(BF16)


def _resid_mm(a, w, layer, x2d, mod, gate_row, tm, mod_row, norm, name):
    t, kk = a.shape
    d = w.shape[2]
    row = lambda i: (i, 0)
    mrow = lambda i: (mod_row(i, tm), 0, 0)
    in_specs = [pl.BlockSpec((tm, kk), row), _resident((kk, d), layer), pl.BlockSpec((tm, d), row),
                pl.BlockSpec((1, 6, d), mrow)]
    args = [a, w, x2d, mod]
    out_specs = [pl.BlockSpec((tm, d), row)]
    out_shape = [jax.ShapeDtypeStruct((t, d), F32)]
    norm_rows = None
    if norm is not None:
        nw, nmod, shift_row, scale_row = norm
        norm_rows = (shift_row, scale_row)
        in_specs += [pl.BlockSpec((1, d), lambda i: (0, 0)), pl.BlockSpec((1, 6, d), mrow)]
        args += [nw, nmod]
        out_specs.append(pl.BlockSpec((tm, d), row))
        out_shape.append(jax.ShapeDtypeStruct((t, d), BF16))
    res = pl.pallas_call(
        functools.partial(_resid_mm_kernel, gate_row=gate_row, norm_rows=norm_rows),
        grid=(t // tm,),
        in_specs=in_specs,
        out_specs=out_specs,
        out_shape=out_shape,
        compiler_params=_cparams("arbitrary"),
        name=name,
    )(*args)
    return (res[0], res[1]) if norm is not None else (res[0], None)


def _ffn_up_kernel(h_ref, wv_ref, wg_ref, cwv_ref, cwg_ref, cbv_ref, cbg_ref, o_ref, wv_s, wg_s, *, seq):
    @pl.when(pl.program_id(1) == 0)
    def _():
        wv_s[...] = wv_ref[...].astype(BF16)
        wg_s[...] = wg_ref[...].astype(BF16)

    h = h_ref[...]
    tm = h.shape[0]
    pos = lax.broadcasted_iota(jnp.int32, (tm, SUB), 0) % seq

    def conv(acc, cw, cb):
        prev = jnp.where(pos == 0, 0.0, pltpu.roll(acc, 1, 0))
        nxt = jnp.where(pos == seq - 1, 0.0, pltpu.roll(acc, tm - 1, 0))
        return cw[0:1, :] * prev + cw[1:2, :] * acc + cw[2:3, :] * nxt + cb[...]

    val = conv(_bdot(h, wv_s[...]), cwv_ref, cbv_ref)
    gt = conv(_bdot(h, wg_s[...]), cwg_ref, cbg_ref)
    o_ref[...] = (_silu(gt) * val).astype(BF16)


def _ffn_up(h, w_up, layer, cw, cb, seq):
    t, d = h.shape
    tm = max(TM_UP, seq)
    assert tm % seq == 0
    nt = D_FF // SUB
    return pl.pallas_call(
        functools.partial(_ffn_up_kernel, seq=seq),
        grid=(nt, t // tm),
        in_specs=[
            pl.BlockSpec((tm, d), lambda j, i: (i, 0)),
            pl.BlockSpec((None, d, SUB), lambda j, i: (layer, 0, j)),
            pl.BlockSpec((None, d, SUB), lambda j, i: (layer, 0, nt + j)),
            pl.BlockSpec((3, SUB), lambda j, i: (0, j)),
            pl.BlockSpec((3, SUB), lambda j, i: (0, nt + j)),
            pl.BlockSpec((1, SUB), lambda j, i: (0, j)),
            pl.BlockSpec((1, SUB), lambda j, i: (0, nt + j)),
        ],
        out_specs=pl.BlockSpec((tm, SUB), lambda j, i: (i, j)),
        out_shape=jax.ShapeDtypeStruct((t, D_FF), BF16),
        scratch_shapes=[pltpu.VMEM((d, SUB), BF16), pltpu.VMEM((d, SUB), BF16)],
        compiler_params=_cparams("arbitrary", "arbitrary"),
        name="ffn_up_conv",
    )(h, w_up, w_up, cw, cw, cb, cb)


def _rope_tables(seq):
    t = np.arange(seq)
    row = (t // GRID_W).astype(np.float32)
    col = (t % GRID_W).astype(np.float32)
    n_freq = DIFF_HEAD_DIM // 4
    inv = jnp.asarray(ROPE_BASE, F32) ** (-jnp.arange(n_freq, dtype=F32) / n_freq)
    ang_r = jnp.asarray(row)[:, None] * inv
    ang_c = jnp.asarray(col)[:, None] * inv
    cr, sr, cc, sc = jnp.cos(ang_r), jnp.sin(ang_r), jnp.cos(ang_c), jnp.sin(ang_c)
    cos64 = jnp.concatenate([cr, cr, cc, cc], axis=-1)
    sin64 = jnp.concatenate([-sr, sr, -sc, sc], axis=-1)
    return jnp.tile(cos64, (1, 2)), jnp.tile(sin64, (1, 2))


def _na_bias_table(rpb):
    qc = np.arange(GRID_W)[:, None]
    kc = np.arange(GRID_W)[None, :]
    onehot = (kc - qc + NA_KW - 1 == np.arange(2 * NA_KW - 1)[:, None, None]).astype(np.float32)
    wstart = np.clip(qc - NA_KW // 2, 0, GRID_W - NA_KW)
    valid = (kc >= wstart) & (kc < wstart + NA_KW)
    toe = jnp.einsum("...ab,bqk->...qak", rpb.astype(F32), jnp.asarray(onehot), precision=lax.Precision.HIGHEST)
    tbl = jnp.where(valid[:, None, :], toe, -jnp.inf)
    return tbl.reshape(tbl.shape[:-2] + (-1,))


def _head_expand():
    e = np.zeros((2, LANES, SSD_INNER), np.float32)
    for d in range(2):
        for h in range(SSD_HEADS):
            e[d, d * SSD_HEADS + h, h * SSD_HEAD_DIM:(h + 1) * SSD_HEAD_DIM] = 1.0
    return jnp.asarray(e, BF16)


def _seg_ones():
    i = np.arange(SEG_BLOCK)
    return jnp.asarray((i[:, None] // 64 == i[None, :] // 64).astype(np.float32), BF16)


def _pad_lanes(v, n):
    return jnp.pad(v.reshape(1, -1), ((0, 0), (0, n - v.size)))


def _split_w_in(w_in):
    o = np.cumsum([0, 1024 + 1536 + 32, 3 * HEAD_COLS, 3 * HEAD_COLS, 6144])
    ssd, wd, wn, g = [w_in[..., o[i]:o[i + 1]].astype(BF16) for i in range(4)]
    ssd = jnp.pad(ssd, ((0, 0), (0, 0), (0, P1_CHUNK - ssd.shape[-1])))
    return g, ssd, wd, wn


def _layer_params(l, norm1_w, norm2_w, ssd_conv_w, ssd_conv_b, ssd_dt_bias, ssd_a_log, ssd_d, ssd_norm_w,
                  diff_q_norm, diff_k_norm, diff_lam, diff_subln_w, na_q_norm, na_k_norm, ffn_conv_w, ffn_conv_b):
    rep = HEAD_COLS // DIFF_HEAD_DIM
    return {
        "norm1_w": norm1_w[l].reshape(1, -1), "norm2_w": norm2_w[l].reshape(1, -1),
        "diff_qn": jnp.tile(diff_q_norm[l], rep).reshape(1, -1), "diff_kn": jnp.tile(diff_k_norm[l], rep).reshape(1, -1),
        "na_qn": jnp.tile(na_q_norm[l], rep).reshape(1, -1), "na_kn": jnp.tile(na_k_norm[l], rep).reshape(1, -1),
        "cw_xs": ssd_conv_w[l][:, :SSD_INNER], "cw_bc": ssd_conv_w[l][:, SSD_INNER:],
        "cb_xs": ssd_conv_b[l][:SSD_INNER].reshape(1, -1), "cb_bc": ssd_conv_b[l][SSD_INNER:].reshape(1, -1),
        "dt_bias": _pad_lanes(ssd_dt_bias[l], LANES), "a_log": _pad_lanes(ssd_a_log[l], LANES),
        "d_exp": jnp.repeat(ssd_d[l], SSD_HEAD_DIM).reshape(1, -1), "ssd_nw": ssd_norm_w[l].reshape(1, -1),
        "e_heads": _head_expand(),
        "diff_lam": diff_lam[l], "subln_w": diff_subln_w[l].reshape(1, -1),
        "ffn_cw": ffn_conv_w[l], "ffn_cb": ffn_conv_b[l].reshape(1, -1),
    }


def _trunk_layer(x2d, h1, nb, seq, mod, mod_row, pw, layer, ctx_out, cache, consts, next_norm):
    wts = consts["weights"]
    u1 = _proj1(h1, wts["gate"], wts["ssd"], layer)
    lam_init = 0.8 - 0.6 * math.exp(-0.3 * layer)
    bd = consts["bd"]
    if cache is None:
        depth, dk, dv, nk, nv, st = ctx_out
        qd, dk, dv = _qkv(h1, wts["diff"], pw["diff_qn"], pw["diff_kn"], bd, nb, seq, layer, (depth, dk, dv),
                          "proj_diff_qkv")
        qn, nk, nv = _qkv(h1, wts["na"], pw["na_qn"], pw["na_kn"], bd, nb, seq, layer, (depth, nk, nv),
                          "proj_na_qkv")
        y_a, st = _ssd(u1, nb, seq, pw, None, layer, (depth, st))
        y_b = _ctx_attn(qd, dk, dv, nb, seq, layer, [pw["diff_lam"], pw["subln_w"]],
                        functools.partial(_ctx_diff_kernel, lam_init=lam_init), "ctx_diff_attn")
        y_c = _ctx_attn(qn, nk, nv, nb, seq, layer, [], _ctx_softmax_kernel, "ctx_softmax_attn")
        ctx_out = (depth, dk, dv, nk, nv, st)
    else:
        cdk, cdv, cnk, cnv, h0 = cache
        qd, kd, vd = _qkv(h1, wts["diff"], pw["diff_qn"], pw["diff_kn"], bd, nb, seq, layer, None, "proj_diff_qkv")
        qn, kn, vn = _qkv(h1, wts["na"], pw["na_qn"], pw["na_kn"], bd, nb, seq, layer, None, "proj_na_qkv")
        y_a, _ = _ssd(u1, nb, seq, pw, h0, layer, None)
        y_b = _lat_diff(qd, kd, vd, nb, seq, cdk, cdv, layer, consts["cos"], consts["sin"], pw["diff_lam"],
                        pw["subln_w"], lam_init)
        y_c = _lat_na(qn, kn, vn, nb, seq, cnk, cnv, layer, pw["na_table"])
    merged = _merge(y_a, y_b, y_c, wts["a"], wts["b"], wts["c"], u1, layer)
    x2d, h2 = _resid_mm(merged, wts["out"], layer, x2d, mod, 2, TM_OUT, mod_row, (pw["norm2_w"], mod, 3, 4),
                        "out_proj")
    act = _ffn_up(h2, wts["up"], layer, pw["ffn_cw"], pw["ffn_cb"], seq)
    x2d, h1_next = _resid_mm(act, wts["down"], layer, x2d, mod, 5, TM_DOWN, mod_row, next_norm, "ffn_down")
    return x2d, h1_next, ctx_out


def kernel(x_prompt, x_sample, c, cache_diff_k, cache_diff_v, cache_na_k, cache_na_v, state_ssm, c_ctx, norm1_w, norm2_w, w_ada, b_ada, w_in, ssd_conv_w, ssd_conv_b, ssd_dt_bias, ssd_a_log, ssd_d, ssd_norm_w, diff_q_norm, diff_k_norm, diff_lam, diff_subln_w, na_q_norm, na_k_norm, na_rpb, w_branch_a, w_branch_b, w_branch_c, w_out, ffn_w_up, ffn_conv_w, ffn_conv_b, ffn_w_down):
    batch, seq, d = x_prompt.shape
    dec_batch, dec_seq, _ = x_sample.shape
    depth = w_in.shape[0]
    past = cache_diff_k.shape[2]
    assert d == D_MODEL and dec_batch + 1 <= 8 and seq % SSD_CHUNK == 0 and dec_seq % (NA_KH * GRID_W) == 0

    c_rows = jnp.concatenate([c_ctx.reshape(1, d), c, jnp.zeros((8 - 1 - dec_batch, d), F32)], axis=0)
    mod_all = _ada(c_rows, w_ada, b_ada).reshape(depth, 8, 6, d)

    cos, sin = _rope_tables(dec_seq)
    consts = {"bd": _seg_ones(), "cos": cos, "sin": sin}
    na_tables = _na_bias_table(na_rpb)
    cdk = cache_diff_k.reshape(dec_batch, depth, past, HEAD_COLS)
    cdv = cache_diff_v.reshape(dec_batch, depth, past, HEAD_COLS)
    cnk = cache_na_k.reshape(dec_batch, depth, past, HEAD_COLS)
    cnv = cache_na_v.reshape(dec_batch, depth, past, HEAD_COLS)
    h0 = state_ssm.reshape(dec_batch, depth, 2, SSD_INNER, SSD_STATE)

    w_gate, w_ssd, w_diff, w_na = _split_w_in(w_in)
    consts["weights"] = {
        "gate": w_gate, "ssd": w_ssd, "diff": w_diff, "na": w_na,
        "a": w_branch_a.astype(BF16), "b": w_branch_b.astype(BF16), "c": w_branch_c.astype(BF16),
        "out": w_out.astype(BF16), "up": ffn_w_up, "down": ffn_w_down.astype(BF16),
    }
    pws = []
    for l in range(depth):
        pw = _layer_params(l, norm1_w, norm2_w, ssd_conv_w, ssd_conv_b, ssd_dt_bias, ssd_a_log, ssd_d,
                           ssd_norm_w, diff_q_norm, diff_k_norm, diff_lam, diff_subln_w, na_q_norm, na_k_norm,
                           ffn_conv_w, ffn_conv_b)
        pw["na_table"] = na_tables[l]
        pws.append(pw)

    ctx_row = lambda i, tm: 0
    lat_row = lambda i, tm: 1 + (i * tm) // dec_seq
    y_p = x_prompt.reshape(batch * seq, d)
    y_s = x_sample.reshape(dec_batch * dec_seq, d)
    h_p = _norm(y_p, pws[0]["norm1_w"], mod_all[0], ctx_row)
    h_s = _norm(y_s, pws[0]["norm1_w"], mod_all[0], lat_row)
    ctx_out = (depth, None, None, None, None, None)
    cache = (cdk, cdv, cnk, cnv, h0)
    for l in range(depth):
        next_norm = (pws[l + 1]["norm1_w"], mod_all[l + 1], 0, 1) if l + 1 < depth else None
        y_p, h_p, ctx_out = _trunk_layer(y_p, h_p, batch, seq, mod_all[l], ctx_row, pws[l], l, ctx_out, None,
                                         consts, next_norm)
        y_s, h_s, _ = _trunk_layer(y_s, h_s, dec_batch, dec_seq, mod_all[l], lat_row, pws[l], l, None, cache,
                                   consts, next_norm)
    _, dk, dv, nk, nv, st = ctx_out
    return (y_p.reshape(batch, seq, d), y_s.reshape(dec_batch, dec_seq, d),
            dk.reshape(batch, depth, seq, DIFF_HEADS, 2, DIFF_HEAD_DIM),
            dv.reshape(batch, depth, seq, DIFF_HEADS, 2 * DIFF_HEAD_DIM),
            nk.reshape(batch, depth, seq, NA_HEADS, NA_HEAD_DIM),
            nv.reshape(batch, depth, seq, NA_HEADS, NA_HEAD_DIM),
            st.reshape(batch, depth, 2, SSD_HEADS, SSD_HEAD_DIM, SSD_STATE))
```

```python
import functools
import math

import numpy as np
import jax
import jax.numpy as jnp
from jax import lax
from jax.experimental import pallas as pl
from jax.experimental.pallas import tpu as pltpu

F32 = jnp.float32
BF16 = jnp.bfloat16

D_MODEL = 2048
GRID_W = 64
SSD_INNER = 1024
SSD_HEAD_DIM = 64
SSD_HEADS = 16
SSD_GROUPS = 2
SSD_STATE = 128
SSD_CHUNK = 128
SSD_BC = 2 * SSD_GROUPS * SSD_STATE
DIFF_HEADS = 8
DIFF_HEAD_DIM = 64
NA_HEADS = 16
NA_HEAD_DIM = 64
NA_KH = 8
NA_KW = 16
HEAD_COLS = 1024
D_FF = 5632
ROPE_BASE = 10000.0
EPS = 1e-6

LANES = 128
VMEM_LIMIT_BYTES = 56 * 1024 * 1024
SUB = 512
SEG_BLOCK = 256

COL_G = 0
COL_Z = 6144
COL_XS = 7168
COL_BC = 8192
COL_DT = 8704
P1_COLS = 9216
P1_CHUNK = 1536
N_GATE_CHUNKS = COL_Z // P1_CHUNK

TM_NORM = 1024
TM_PROJ = 512
TM_P1 = 1024
TM_MERGE = 512
TM_OUT = 512
TM_UP = 1024
TM_DOWN = 256


def _cparams(*sem):
    return pltpu.CompilerParams(dimension_semantics=sem, vmem_limit_bytes=VMEM_LIMIT_BYTES)


def _resident(shape, layer, col_block=0):
    return pl.BlockSpec((None,) + tuple(shape), lambda *_: (layer, 0, col_block), pipeline_mode=pl.Buffered(1))


def _sigmoid(x):
    return 1.0 / (1.0 + jnp.exp(-x))


def _silu(x):
    return x * _sigmoid(x)


def _bdot(a, b):
    return jnp.dot(a, b, preferred_element_type=F32)


def _bdot_nt(a, b):
    return lax.dot_general(a, b, (((1,), (1,)), ((), ())), preferred_element_type=F32)


def _split3(x):
    p1 = x.astype(BF16)
    r1 = x - p1.astype(F32)
    p2 = r1.astype(BF16)
    p3 = (r1 - p2.astype(F32)).astype(BF16)
    return p1, p2, p3


def _sel_right(x, e):
    p1, p2, _ = _split3(x)
    return _bdot(p1, e) + _bdot(p2, e)


def _sel_left(t, x):
    p1, p2, p3 = _split3(x)
    return _bdot(t, p1) + _bdot(t, p2) + _bdot(t, p3)


def _seg64_rms(a, bd):
    ss = _bdot((a * a).astype(BF16), bd)
    return a * lax.rsqrt(ss * (1.0 / 64.0) + EPS)


def _modulated_norm(x, nw, shift, scale):
    ms = jnp.mean(x * x, axis=-1, keepdims=True)
    y = x * lax.rsqrt(ms + EPS) * nw
    return y * (1.0 + scale) + shift


def _ada_kernel(c_ref, w_ref, b_ref, o_ref):
    c = c_ref[...]
    s = _silu(c).astype(BF16)
    o_ref[0] = _bdot(s, w_ref[0].astype(BF16)) + b_ref[0]


def _ada(c_rows, w_ada, b_ada):
    depth, d, n = w_ada.shape
    tn = 1024
    return pl.pallas_call(
        _ada_kernel,
        grid=(depth, n // tn),
        in_specs=[
            pl.BlockSpec((8, d), lambda l, j: (0, 0)),
            pl.BlockSpec((1, d, tn), lambda l, j: (l, 0, j)),
            pl.BlockSpec((1, 1, tn), lambda l, j: (l, 0, j)),
        ],
        out_specs=pl.BlockSpec((1, 8, tn), lambda l, j: (l, 0, j)),
        out_shape=jax.ShapeDtypeStruct((depth, 8, n), F32),
        compiler_params=_cparams("arbitrary", "arbitrary"),
        name="ada_mod",
    )(c_rows, w_ada, b_ada.reshape(depth, 1, n))


def _norm_kernel(x_ref, nw_ref, mod_ref, h_ref):
    h_ref[...] = _modulated_norm(x_ref[...], nw_ref[...], mod_ref[0, 0:1, :], mod_ref[0, 1:2, :]).astype(BF16)


def _norm(x2d, nw, mod, mod_row):
    t, d = x2d.shape
    tm = TM_NORM
    return pl.pallas_call(
        _norm_kernel,
        grid=(t // tm,),
        in_specs=[
            pl.BlockSpec((tm, d), lambda i: (i, 0)),
            pl.BlockSpec((1, d), lambda i: (0, 0)),
            pl.BlockSpec((1, 6, d), lambda i: (mod_row(i, tm), 0, 0)),
        ],
        out_specs=pl.BlockSpec((tm, d), lambda i: (i, 0)),
        out_shape=jax.ShapeDtypeStruct((t, d), BF16),
        compiler_params=_cparams("arbitrary"),
        name="norm_mod",
    )(x2d, nw, mod)


def _proj1_kernel(h_ref, wg_ref, ws_ref, o_ref):
    c = pl.program_id(0)
    h = h_ref[...]

    def tiles(w_ref, epilogue):
        for s in range(P1_CHUNK // SUB):
            cols = slice(s * SUB, (s + 1) * SUB)
            o_ref[:, cols] = epilogue(_bdot(h, w_ref[:, cols]))

    @pl.when(c < N_GATE_CHUNKS)
    def _():
        tiles(wg_ref, _sigmoid)

    @pl.when(c >= N_GATE_CHUNKS)
    def _():
        tiles(ws_ref, lambda a: a)


def _proj1(h, w_p, layer):
    t, d = h.shape
    tm = TM_P1
    return pl.pallas_call(
        _proj1_kernel,
        grid=(P1_COLS // P1_CHUNK, t // tm),
        in_specs=[
            pl.BlockSpec((tm, d), lambda c, i: (i, 0)),
            pl.BlockSpec((None, d, P1_CHUNK), lambda c, i: (layer, 0, jnp.minimum(c, N_GATE_CHUNKS - 1))),
            pl.BlockSpec((None, d, P1_CHUNK), lambda c, i: (layer, 0, jnp.maximum(c, N_GATE_CHUNKS))),
        ],
        out_specs=pl.BlockSpec((tm, P1_CHUNK), lambda c, i: (i, c)),
        out_shape=jax.ShapeDtypeStruct((t, P1_COLS), F32),
        compiler_params=_cparams("arbitrary", "arbitrary"),
        name="proj_gates_ssd",
    )(h, w_p, w_p)


def _qkv_kernel(*refs, n_alias):
    h_ref, w_ref, qnw_ref, knw_ref, bd_ref = refs[:5]
    q_ref, k_ref, v_ref = refs[5 + n_alias:]
    h = h_ref[...]
    bd = bd_ref[...]
    for sec, (o_ref, nw_ref) in enumerate(((q_ref, qnw_ref), (k_ref, knw_ref), (v_ref, None))):
        for s in range(HEAD_COLS // SUB):
            acc = _bdot(h, w_ref[:, sec * HEAD_COLS + s * SUB: sec * HEAD_COLS + (s + 1) * SUB])
            if nw_ref is not None:
                parts = [_seg64_rms(acc[:, c * SEG_BLOCK:(c + 1) * SEG_BLOCK], bd) for c in range(SUB // SEG_BLOCK)]
                acc = jnp.concatenate(parts, axis=1) * nw_ref[:, s * SUB:(s + 1) * SUB]
            o_ref[..., s * SUB:(s + 1) * SUB] = acc.reshape(o_ref.shape[:-1] + (SUB,)).astype(o_ref.dtype)


def _qkv(h, w, w_col_block, qnw, knw, bd, nb, seq, layer, kv_prev, name):
    t, d = h.shape
    tm = TM_PROJ
    in_specs = [
        pl.BlockSpec((tm, d), lambda i: (i, 0)),
        _resident((d, 3 * HEAD_COLS), layer, w_col_block),
        pl.BlockSpec((1, HEAD_COLS), lambda i: (0, 0)),
        pl.BlockSpec((1, HEAD_COLS), lambda i: (0, 0)),
        pl.BlockSpec((SEG_BLOCK, SEG_BLOCK), lambda i: (0, 0)),
    ]
    args = [h, w, qnw, knw, bd]
    q_spec = pl.BlockSpec((tm, HEAD_COLS), lambda i: (i, 0))
    q_shape = jax.ShapeDtypeStruct((t, HEAD_COLS), F32 if kv_prev is None else BF16)
    aliases = {}
    n_alias = 0
    if kv_prev is None:
        kv_spec, kv_shape = q_spec, q_shape
    else:
        depth, k_prev, v_prev = kv_prev
        assert tm % seq == 0
        kv_spec = pl.BlockSpec((tm // seq, None, seq, HEAD_COLS), lambda i: (i, layer, 0, 0))
        kv_shape = jax.ShapeDtypeStruct((nb, depth, seq, HEAD_COLS), F32)
        if k_prev is not None:
            in_specs += [pl.BlockSpec(memory_space=pl.ANY)] * 2
            args += [k_prev, v_prev]
            aliases = {5: 1, 6: 2}
            n_alias = 2
    return pl.pallas_call(
        functools.partial(_qkv_kernel, n_alias=n_alias),
        grid=(t // tm,),
        in_specs=in_specs,
        out_specs=[q_spec, kv_spec, kv_spec],
        out_shape=[q_shape, kv_shape, kv_shape],
        input_output_aliases=aliases,
        compiler_params=_cparams("arbitrary"),
        name=name,
    )(*args)


def _ssd_kernel(*refs, seq, has_h0, has_state_out, n_alias):
    (z_ref, xs_ref, bc_ref, dt_ref, cwx_ref, cbx_ref, cwb_ref, cbb_ref, dtb_ref, alog_ref,
     dexp_ref, nw_ref, e_ref) = refs[:13]
    rest = list(refs[13:])
    h0_ref = rest.pop(0) if has_h0 else None
    rest = rest[n_alias:]
    y_ref = rest.pop(0)
    st_ref = rest.pop(0) if has_state_out else None
    xs_s, bc_s, ya_s, st_s = rest

    q = SSD_CHUNK
    nc = seq // q
    half = SSD_INNER // SSD_GROUPS

    def conv_chunk(c, carry):
        r0 = pl.multiple_of(c * q, q)
        for src, dst, cw, cb in ((xs_ref, xs_s, cwx_ref, cbx_ref), (bc_ref, bc_s, cwb_ref, cbb_ref)):
            x = src[pl.ds(r0, q), :]
            xp = src[pl.ds(jnp.maximum(r0 - 1, 0), 1), :]
            xn = src[pl.ds(jnp.minimum(r0 + q, seq - 1), 1), :]
            xp = jnp.where(c == 0, 0.0, xp)
            xn = jnp.where(c == nc - 1, 0.0, xn)
            rows = lax.broadcasted_iota(jnp.int32, x.shape, 0)
            prev = jnp.where(rows == 0, xp, pltpu.roll(x, 1, 0))
            nxt = jnp.where(rows == q - 1, xn, pltpu.roll(x, q - 1, 0))
            y = cw[0:1, :] * prev + cw[1:2, :] * x + cw[2:3, :] * nxt + cb[...]
            dst[pl.ds(r0, q), :] = _silu(y)
        return carry

    lax.fori_loop(0, nc, conv_chunk, 0)

    a_vec = -jnp.exp(alog_ref[...])
    dt_bias = dtb_ref[...]
    ri = lax.broadcasted_iota(jnp.int32, (q, q), 0)
    ci = lax.broadcasted_iota(jnp.int32, (q, q), 1)

    for d in (0, 1):
        tri = (ri >= ci) if d == 0 else (ci >= ri)
        tri_b = jnp.where(tri, 1.0, 0.0).astype(BF16)
        for g in range(SSD_GROUPS):
            if has_h0:
                st_s[g] = h0_ref[d, g * half:(g + 1) * half, :].T
            else:
                st_s[g] = jnp.zeros((SSD_STATE, half), F32)

        def chunk(i, carry, d=d, tri=tri, tri_b=tri_b):
            c = i if d == 0 else nc - 1 - i
            r0 = pl.multiple_of(c * q, q)
            x_dt = dt_ref[pl.ds(r0, q), :] + dt_bias
            dtc = jnp.maximum(x_dt, 0.0) + jnp.log1p(jnp.exp(-jnp.abs(x_dt)))
            la = dtc * a_vec
            cum = _sel_left(tri_b, la)
            cum_t = cum.T
            e = e_ref[d]
            dt_x = _sel_right(dtc, e)
            cum_x = _sel_right(cum, e)
            last = q - 1 if d == 0 else 0
            cl = cum_x[last:last + 1, :]
            xd = xs_s[pl.ds(r0, q), :] * dt_x
            xdb = xd.astype(BF16)
            xdd = (xd * jnp.exp(cl - cum_x)).astype(BF16)
            ecum = jnp.exp(cum_x)
            bcv = bc_s[pl.ds(r0, q), :]
            for g in range(SSD_GROUPS):
                bg = bcv[:, g * SSD_STATE:(g + 1) * SSD_STATE]
                cg = bcv[:, (SSD_GROUPS + g) * SSD_STATE:(SSD_GROUPS + g + 1) * SSD_STATE]
                bgb = bg.astype(BF16)
                cgb = cg.astype(BF16)
                gm = _bdot_nt(cgb, bgb)
                st = st_s[g]
                y_off = _bdot(cgb, st.astype(BF16)) * ecum[:, g * half:(g + 1) * half]
                ys = []
                for hh in range(SSD_HEADS // SSD_GROUPS):
                    h = g * (SSD_HEADS // SSD_GROUPS) + hh
                    k = d * SSD_HEADS + h
                    decay = jnp.where(tri, jnp.exp(cum[:, k:k + 1] - cum_t[k:k + 1, :]), 0.0)
                    m = (gm * decay).astype(BF16)
                    ys.append(_bdot(m, xdb[:, h * SSD_HEAD_DIM:(h + 1) * SSD_HEAD_DIM]))
                yg = jnp.concatenate(ys, axis=1) + y_off
                if d == 0:
                    ya_s[pl.ds(r0, q), g * half:(g + 1) * half] = yg
                else:
                    ya_s[pl.ds(r0, q), g * half:(g + 1) * half] += yg
                st_s[g] = (st * jnp.exp(cl[:, g * half:(g + 1) * half])
                           + _bdot(bg.T.astype(BF16), xdd[:, g * half:(g + 1) * half]))
            return carry

        lax.fori_loop(0, nc, chunk, 0)
        if has_state_out:
            for g in range(SSD_GROUPS):
                st_ref[d, g * half:(g + 1) * half, :] = st_s[g].T

    def fin_chunk(c, carry):
        r0 = pl.multiple_of(c * q, q)
        y = ya_s[pl.ds(r0, q), :] + dexp_ref[...] * xs_s[pl.ds(r0, q), :]
        y = y * _silu(z_ref[pl.ds(r0, q), :])
        for g in range(SSD_GROUPS):
            v = y[:, g * half:(g + 1) * half]
            ms = jnp.mean(v * v, axis=-1, keepdims=True)
            out = v * lax.rsqrt(ms + EPS) * nw_ref[:, g * half:(g + 1) * half]
            y_ref[pl.ds(r0, q), g * half:(g + 1) * half] = out.astype(BF16)
        return carry

    lax.fori_loop(0, nc, fin_chunk, 0)


def _ssd(u1, nb, seq, pw, h0, layer, state_prev):
    has_h0 = h0 is not None
    has_state_out = state_prev is not None
    c1 = lambda b: (0, 0)
    in_specs = [
        pl.BlockSpec((seq, SSD_INNER), lambda b: (b, COL_Z // SSD_INNER)),
        pl.BlockSpec((seq, SSD_INNER), lambda b: (b, COL_XS // SSD_INNER)),
        pl.BlockSpec((seq, SSD_BC), lambda b: (b, COL_BC // SSD_BC)),
        pl.BlockSpec((seq, LANES), lambda b: (b, COL_DT // LANES)),
        pl.BlockSpec((3, SSD_INNER), c1),
        pl.BlockSpec((1, SSD_INNER), c1),
        pl.BlockSpec((3, SSD_BC), c1),
        pl.BlockSpec((1, SSD_BC), c1),
        pl.BlockSpec((1, LANES), c1),
        pl.BlockSpec((1, LANES), c1),
        pl.BlockSpec((1, SSD_INNER), c1),
        pl.BlockSpec((1, SSD_INNER), c1),
        pl.BlockSpec((2, LANES, SSD_INNER), lambda b: (0, 0, 0)),
    ]
    args = [u1, u1, u1, u1, pw["cw_xs"], pw["cb_xs"], pw["cw_bc"], pw["cb_bc"], pw["dt_bias"], pw["a_log"],
            pw["d_exp"], pw["ssd_nw"], pw["e_heads"]]
    if has_h0:
        in_specs.append(pl.BlockSpec((None, None, 2, SSD_INNER, SSD_STATE), lambda b: (b, layer, 0, 0, 0)))
        args.append(h0)
    out_specs = [pl.BlockSpec((seq, SSD_INNER), lambda b: (b, 0))]
    out_shape = [jax.ShapeDtypeStruct((nb * seq, SSD_INNER), BF16)]
    aliases = {}
    n_alias = 0
    if has_state_out:
        depth, prev = state_prev
        out_specs.append(pl.BlockSpec((None, None, 2, SSD_INNER, SSD_STATE), lambda b: (b, layer, 0, 0, 0)))
        out_shape.append(jax.ShapeDtypeStruct((nb, depth, 2, SSD_INNER, SSD_STATE), F32))
        if prev is not None:
            aliases = {len(args): 1}
            in_specs.append(pl.BlockSpec(memory_space=pl.ANY))
            args.append(prev)
            n_alias = 1
    res = pl.pallas_call(
        functools.partial(_ssd_kernel, seq=seq, has_h0=has_h0, has_state_out=has_state_out, n_alias=n_alias),
        grid=(nb,),
        in_specs=in_specs,
        out_specs=out_specs,
        out_shape=out_shape,
        input_output_aliases=aliases,
        scratch_shapes=[
            pltpu.VMEM((seq, SSD_INNER), F32),
            pltpu.VMEM((seq, SSD_BC), F32),
            pltpu.VMEM((seq, SSD_INNER), F32),
            pltpu.VMEM((SSD_GROUPS, SSD_STATE, SSD_INNER // SSD_GROUPS), F32),
        ],
        compiler_params=_cparams("arbitrary"),
        name="ssd_bidir",
    )(*args)
    return res if has_state_out else (res[0], None)


def _softmax_rows(parts, mxu_sums=False):
    m = functools.reduce(jnp.maximum, [jnp.max(s, axis=-1, keepdims=True) for s in parts])
    es = [jnp.exp(s - m) for s in parts]
    if mxu_sums:
        dens = [_bdot(e.astype(BF16), jnp.ones((e.shape[1], SEG_BLOCK), BF16)) for e in es]
        inv = 1.0 / functools.reduce(lambda a, b: a + b, dens)
        return [e * jnp.tile(inv, (1, e.shape[1] // SEG_BLOCK)) for e in es]
    den = functools.reduce(lambda a, b: a + b, [jnp.sum(e, axis=-1, keepdims=True) for e in es])
    inv = 1.0 / den
    return [e * inv for e in es]


def _exp_rows(parts):
    m = functools.reduce(jnp.maximum, [jnp.max(s, axis=-1, keepdims=True) for s in parts])
    return [jnp.exp(s - m).astype(BF16) for s in parts]


def _lane_halves(x):
    left = lax.broadcasted_iota(jnp.int32, x.shape, 1) < LANES // 2
    return jnp.where(left, x, 0.0).astype(BF16), jnp.where(left, 0.0, x).astype(BF16)


def _lambda(lp, lam_init):
    a = jnp.sum(lp[0:1, :] * lp[1:2, :], axis=-1, keepdims=True)
    b = jnp.sum(lp[2:3, :] * lp[3:4, :], axis=-1, keepdims=True)
    return jnp.exp(a) - jnp.exp(b) + lam_init


def _subln(o, w, lam_init):
    ms = jnp.mean(o * o, axis=-1, keepdims=True)
    return o * lax.rsqrt(ms + EPS) * w * (1.0 - lam_init)


def _rope(x, cos, sin):
    lane = lax.broadcasted_iota(jnp.int32, x.shape, 1)
    swapped = jnp.where((lane & 16) == 0, pltpu.roll(x, LANES - 16, 1), pltpu.roll(x, 16, 1))
    return x * cos + swapped * sin


def _ctx_diff_kernel(q_ref, k_ref, v_ref, lamp_ref, sw_ref, o_ref, *, lam_init):
    scale = DIFF_HEAD_DIM ** -0.5
    lam = _lambda(lamp_ref[...], lam_init)
    sw = sw_ref[...]
    for hb in range(DIFF_HEADS):
        ps = []
        for t in (0, 1):
            sl = slice(hb * LANES + t * DIFF_HEAD_DIM, hb * LANES + (t + 1) * DIFF_HEAD_DIM)
            s = _bdot_nt(q_ref[:, sl], k_ref[:, sl].astype(BF16)) * scale
            ps.append(_softmax_rows([s], mxu_sums=True)[0])
        att = (ps[0] - lam * ps[1]).astype(BF16)
        blk = slice(hb * LANES, (hb + 1) * LANES)
        o = _subln(_bdot(att, v_ref[:, blk].astype(BF16)), sw, lam_init)
        o_ref[:, blk] = o.astype(BF16)


def _ctx_softmax_kernel(q_ref, k_ref, v_ref, o_ref):
    scale = NA_HEAD_DIM ** -0.5
    ones_l, ones_r = _lane_halves(jnp.ones((k_ref.shape[0], LANES), F32))
    for hb in range(NA_HEADS // 2):
        blk = slice(hb * LANES, (hb + 1) * LANES)
        v_l, v_r = _lane_halves(v_ref[:, blk])
        es = []
        for t in (0, 1):
            sl = slice(hb * LANES + t * NA_HEAD_DIM, hb * LANES + (t + 1) * NA_HEAD_DIM)
            s = _bdot_nt(q_ref[:, sl], k_ref[:, sl].astype(BF16)) * scale
            es.append(_exp_rows([s])[0])
        num = _bdot(es[0], v_l) + _bdot(es[1], v_r)
        den = _bdot(es[0], ones_l) + _bdot(es[1], ones_r)
        o_ref[:, blk] = (num / den).astype(BF16)


def _ctx_attn(q, k_all, v_all, nb, seq, layer, extra, kern, name):
    kv_spec = pl.BlockSpec((None, None, seq, HEAD_COLS), lambda b: (b, layer, 0, 0))
    extra_specs = [pl.BlockSpec(a.shape, lambda b: (0, 0)) for a in extra]
    return pl.pallas_call(
        kern,
        grid=(nb,),
        in_specs=[pl.BlockSpec((seq, HEAD_COLS), lambda b: (b, 0)), kv_spec, kv_spec] + extra_specs,
        out_specs=pl.BlockSpec((seq, HEAD_COLS), lambda b: (b, 0)),
        out_shape=jax.ShapeDtypeStruct((nb * seq, HEAD_COLS), BF16),
        compiler_params=_cparams("arbitrary"),
        name=name,
    )(q, k_all, v_all, *extra)


LAT_DIFF_TQ = 256


def _lat_diff_kernel(q_ref, k_ref, v_ref, ck_ref, cv_ref, cos_ref, sin_ref, lamp_ref, sw_ref, o_ref, *, lam_init):
    scale = DIFF_HEAD_DIM ** -0.5
    k = _rope(k_ref[...], cos_ref[...], sin_ref[...])
    k_t = [k[:, t * DIFF_HEAD_DIM:(t + 1) * DIFF_HEAD_DIM].astype(BF16) for t in (0, 1)]
    ck_t = [ck_ref[:, t * DIFF_HEAD_DIM:(t + 1) * DIFF_HEAD_DIM].astype(BF16) for t in (0, 1)]
    v = v_ref[...].astype(BF16)
    cv = cv_ref[...].astype(BF16)
    lam = _lambda(lamp_ref[...], lam_init)
    sw = sw_ref[...]
    for qb in range(q_ref.shape[0] // LAT_DIFF_TQ):
        rs = slice(qb * LAT_DIFF_TQ, (qb + 1) * LAT_DIFF_TQ)
        q = _rope(q_ref[rs, :], cos_ref[rs, :], sin_ref[rs, :])
        pl_, pc_ = [], []
        for t in (0, 1):
            qt = q[:, t * DIFF_HEAD_DIM:(t + 1) * DIFF_HEAD_DIM].astype(BF16)
            p_loc, p_ctx = _softmax_rows([_bdot_nt(qt, k_t[t]) * scale, _bdot_nt(qt, ck_t[t]) * scale])
            pl_.append(p_loc)
            pc_.append(p_ctx)
        a_loc = (pl_[0] - lam * pl_[1]).astype(BF16)
        a_ctx = (pc_[0] - lam * pc_[1]).astype(BF16)
        o = _bdot(a_loc, v) + _bdot(a_ctx, cv)
        o_ref[rs, :] = _subln(o, sw, lam_init).astype(BF16)


def _lat_diff(q, k, v, nb, seq, cache_k, cache_v, layer, cos, sin, lamp, sw, lam_init):
    past = cache_k.shape[2]
    cache_spec = pl.BlockSpec((None, None, past, LANES), lambda b, h: (b, layer, 0, h))
    loc_spec = pl.BlockSpec((seq, LANES), lambda b, h: (b, h))
    tab_spec = pl.BlockSpec((seq, LANES), lambda b, h: (0, 0))
    return pl.pallas_call(
        functools.partial(_lat_diff_kernel, lam_init=lam_init),
        grid=(nb, DIFF_HEADS),
        in_specs=[
            loc_spec, loc_spec, loc_spec, cache_spec, cache_spec, tab_spec, tab_spec,
            pl.BlockSpec((4, DIFF_HEAD_DIM), lambda b, h: (0, 0)),
            pl.BlockSpec((1, LANES), lambda b, h: (0, 0)),
        ],
        out_specs=loc_spec,
        out_shape=jax.ShapeDtypeStruct((nb * seq, HEAD_COLS), BF16),
        compiler_params=_cparams("arbitrary", "arbitrary"),
        name="lat_diff_attn",
    )(q, k, v, cache_k, cache_v, cos, sin, lamp, sw)


NA_Q_ROWS = 4


def _lat_na_kernel(q_ref, k_ref, v_ref, ck_ref, cv_ref, tb_ref, o_ref, bias_s, *, rows):
    scale = NA_HEAD_DIM ** -0.5
    kh = min(NA_KH, rows)
    win0 = lambda qr: min(max(qr - kh // 2, 0), rows - kh)

    @pl.when(pl.program_id(1) == 0)
    def _():
        bias_s[...] = jnp.full(bias_s.shape, -jnp.inf, F32)
        for t in (0, 1):
            for qr in range(rows):
                r0 = win0(qr)
                a0 = r0 - qr + NA_KH - 1
                bias_s[t, qr * GRID_W:(qr + 1) * GRID_W, r0 * GRID_W:(r0 + kh) * GRID_W] = (
                    tb_ref[t, :, a0 * GRID_W:(a0 + kh) * GRID_W])

    k_h = [k_ref[:, t * NA_HEAD_DIM:(t + 1) * NA_HEAD_DIM].astype(BF16) for t in (0, 1)]
    ck_h = [ck_ref[:, t * NA_HEAD_DIM:(t + 1) * NA_HEAD_DIM].astype(BF16) for t in (0, 1)]
    v_h = _lane_halves(v_ref[...])
    cv_h = _lane_halves(cv_ref[...])
    ones_h = lambda n: _lane_halves(jnp.ones((n, LANES), F32))
    cones_h = ones_h(cv_ref.shape[0])
    for qb in range(rows // NA_Q_ROWS):
        lo = win0(qb * NA_Q_ROWS) // 2 * 2
        hi = min(rows, (win0((qb + 1) * NA_Q_ROWS - 1) + kh + 1) // 2 * 2)
        rs = slice(qb * NA_Q_ROWS * GRID_W, (qb + 1) * NA_Q_ROWS * GRID_W)
        ks = slice(lo * GRID_W, hi * GRID_W)
        num = den = None
        for t in (0, 1):
            qh = q_ref[rs, t * NA_HEAD_DIM:(t + 1) * NA_HEAD_DIM].astype(BF16)
            s_loc = _bdot_nt(qh, k_h[t][ks]) * scale + bias_s[t, rs, ks]
            s_ctx = _bdot_nt(qh, ck_h[t]) * scale
            e_loc, e_ctx = _exp_rows([s_loc, s_ctx])
            n_t = _bdot(e_loc, v_h[t][ks]) + _bdot(e_ctx, cv_h[t])
            d_t = _bdot(e_loc, ones_h((hi - lo) * GRID_W)[t]) + _bdot(e_ctx, cones_h[t])
            num = n_t if num is None else num + n_t
            den = d_t if den is None else den + d_t
        o_ref[rs, :] = (num / den).astype(BF16)


def _lat_na(q, k, v, nb, seq, cache_k, cache_v, layer, table):
    past = cache_k.shape[2]
    rows = seq // GRID_W
    ncol = table.shape[-1]
    loc_spec = pl.BlockSpec((seq, LANES), lambda h, b: (b, h))
    cache_spec = pl.BlockSpec((None, None, past, LANES), lambda h, b: (b, layer, 0, h))
    return pl.pallas_call(
        functools.partial(_lat_na_kernel, rows=rows),
        grid=(NA_HEADS // 2, nb),
        in_specs=[loc_spec, loc_spec, loc_spec, cache_spec, cache_spec,
                  pl.BlockSpec((2, GRID_W, ncol), lambda h, b: (h, 0, 0))],
        out_specs=pl.BlockSpec((seq, LANES), lambda h, b: (b, h)),
        out_shape=jax.ShapeDtypeStruct((nb * seq, HEAD_COLS), BF16),
        scratch_shapes=[pltpu.VMEM((2, seq, seq), F32)],
        compiler_params=_cparams("arbitrary", "arbitrary"),
        name="lat_nbr_attn",
    )(q, k, v, cache_k, cache_v, table)


def _merge_kernel(ya_ref, yb_ref, yc_ref, wa_ref, wb_ref, wc_ref, ga_ref, gb_ref, gc_ref, o_ref):
    ya, yb, yc = ya_ref[...], yb_ref[...], yc_ref[...]
    for s in range(D_MODEL // SUB):
        cols = slice(s * SUB, (s + 1) * SUB)
        m = (ga_ref[:, cols] * _bdot(ya, wa_ref[:, cols])
             + gb_ref[:, cols] * _bdot(yb, wb_ref[:, cols])
             + gc_ref[:, cols] * _bdot(yc, wc_ref[:, cols]))
        o_ref[:, cols] = m.astype(BF16)


def _merge(ya, yb, yc, wa, wb, wc, u1, layer):
    t, kk = ya.shape
    n = wa.shape[2]
    tm = TM_MERGE
    ysp = pl.BlockSpec((tm, kk), lambda i: (i, 0))
    wsp = _resident((kk, n), layer)
    return pl.pallas_call(
        _merge_kernel,
        grid=(t // tm,),
        in_specs=[ysp, ysp, ysp, wsp, wsp, wsp,
                  pl.BlockSpec((tm, n), lambda i: (i, 0)),
                  pl.BlockSpec((tm, n), lambda i: (i, 1)),
                  pl.BlockSpec((tm, n), lambda i: (i, 2))],
        out_specs=pl.BlockSpec((tm, n), lambda i: (i, 0)),
        out_shape=jax.ShapeDtypeStruct((t, n), BF16),
        compiler_params=_cparams("arbitrary"),
        name="branch_merge",
    )(ya, yb, yc, wa, wb, wc, u1, u1, u1)


def _resid_mm_kernel(*refs, gate_row, norm_rows):
    if norm_rows is None:
        a_ref, w_ref, x_ref, mod_ref, xo_ref = refs
    else:
        a_ref, w_ref, x_ref, mod_ref, nw_ref, nmod_ref, xo_ref, h_ref = refs
    a = a_ref[...]
    for s in range(D_MODEL // SUB):
        cols = slice(s * SUB, (s + 1) * SUB)
        xo_ref[:, cols] = x_ref[:, cols] + mod_ref[0, gate_row:gate_row + 1, cols] * _bdot(a, w_ref[:, cols])
    if norm_rows is not None:
        shift_row, scale_row = norm_rows
        h = _modulated_norm(xo_ref[...], nw_ref[...], nmod_ref[0, shift_row:shift_row + 1, :],
                            nmod_ref[0, scale_row:scale_row + 1, :])
        h_ref[...] = h.astype(BF16)


def _resid_mm(a, w, layer, x2d, mod, gate_row, tm, mod_row, norm, name):
    t, kk = a.shape
    d = w.shape[2]
    row = lambda i: (i, 0)
    mrow = lambda i: (mod_row(i, tm), 0, 0)
    in_specs = [pl.BlockSpec((tm, kk), row), _resident((kk, d), layer), pl.BlockSpec((tm, d), row),
                pl.BlockSpec((1, 6, d), mrow)]
    args = [a, w, x2d, mod]
    out_specs = [pl.BlockSpec((tm, d), row)]
    out_shape = [jax.ShapeDtypeStruct((t, d), F32)]
    norm_rows = None
    if norm is not None:
        nw, nmod, shift_row, scale_row = norm
        norm_rows = (shift_row, scale_row)
        in_specs += [pl.BlockSpec((1, d), lambda i: (0, 0)), pl.BlockSpec((1, 6, d), mrow)]
        args += [nw, nmod]
        out_specs.append(pl.BlockSpec((tm, d), row))
        out_shape.append(jax.ShapeDtypeStruct((t, d), BF16))
    res = pl.pallas_call(
        functools.partial(_resid_mm_kernel, gate_row=gate_row, norm_rows=norm_rows),
        grid=(t // tm,),
        in_specs=in_specs,
        out_specs=out_specs,
        out_shape=out_shape,
        compiler_params=_cparams("arbitrary"),
        name=name,
    )(*args)
    return (res[0], res[1]) if norm is not None else (res[0], None)


def _ffn_up_kernel(h_ref, wv_ref, wg_ref, cwv_ref, cwg_ref, cbv_ref, cbg_ref, o_ref, wv_s, wg_s, *, seq):
    @pl.when(pl.program_id(1) == 0)
    def _():
        wv_s[...] = wv_ref[...].astype(BF16)
        wg_s[...] = wg_ref[...].astype(BF16)

    h = h_ref[...]
    tm = h.shape[0]
    pos = lax.broadcasted_iota(jnp.int32, (tm, SUB), 0) % seq

    def conv(acc, cw, cb):
        prev = jnp.where(pos == 0, 0.0, pltpu.roll(acc, 1, 0))
        nxt = jnp.where(pos == seq - 1, 0.0, pltpu.roll(acc, tm - 1, 0))
        return cw[0:1, :] * prev + cw[1:2, :] * acc + cw[2:3, :] * nxt + cb[...]

    val = conv(_bdot(h, wv_s[...]), cwv_ref, cbv_ref)
    gt = conv(_bdot(h, wg_s[...]), cwg_ref, cbg_ref)
    o_ref[...] = (_silu(gt) * val).astype(BF16)


def _ffn_up(h, w_up, layer, cw, cb, seq):
    t, d = h.shape
    tm = max(TM_UP, seq)
    assert tm % seq == 0
    nt = D_FF // SUB
    return pl.pallas_call(
        functools.partial(_ffn_up_kernel, seq=seq),
        grid=(nt, t // tm),
        in_specs=[
            pl.BlockSpec((tm, d), lambda j, i: (i, 0)),
            pl.BlockSpec((None, d, SUB), lambda j, i: (layer, 0, j)),
            pl.BlockSpec((None, d, SUB), lambda j, i: (layer, 0, nt + j)),
            pl.BlockSpec((3, SUB), lambda j, i: (0, j)),
            pl.BlockSpec((3, SUB), lambda j, i: (0, nt + j)),
            pl.BlockSpec((1, SUB), lambda j, i: (0, j)),
            pl.BlockSpec((1, SUB), lambda j, i: (0, nt + j)),
        ],
        out_specs=pl.BlockSpec((tm, SUB), lambda j, i: (i, j)),
        out_shape=jax.ShapeDtypeStruct((t, D_FF), BF16),
        scratch_shapes=[pltpu.VMEM((d, SUB), BF16), pltpu.VMEM((d, SUB), BF16)],
        compiler_params=_cparams("arbitrary", "arbitrary"),
        name="ffn_up_conv",
    )(h, w_up, w_up, cw, cw, cb, cb)


def _rope_tables(seq):
    t = np.arange(seq)
    row = (t // GRID_W).astype(np.float32)
    col = (t % GRID_W).astype(np.float32)
    n_freq = DIFF_HEAD_DIM // 4
    inv = jnp.asarray(ROPE_BASE, F32) ** (-jnp.arange(n_freq, dtype=F32) / n_freq)
    ang_r = jnp.asarray(row)[:, None] * inv
    ang_c = jnp.asarray(col)[:, None] * inv
    cr, sr, cc, sc = jnp.cos(ang_r), jnp.sin(ang_r), jnp.cos(ang_c), jnp.sin(ang_c)
    cos64 = jnp.concatenate([cr, cr, cc, cc], axis=-1)
    sin64 = jnp.concatenate([-sr, sr, -sc, sc], axis=-1)
    return jnp.tile(cos64, (1, 2)), jnp.tile(sin64, (1, 2))


def _na_bias_table(rpb):
    qc = np.arange(GRID_W)[:, None]
    kc = np.arange(GRID_W)[None, :]
    onehot = (kc - qc + NA_KW - 1 == np.arange(2 * NA_KW - 1)[:, None, None]).astype(np.float32)
    wstart = np.clip(qc - NA_KW // 2, 0, GRID_W - NA_KW)
    valid = (kc >= wstart) & (kc < wstart + NA_KW)
    toe = jnp.einsum("...ab,bqk->...qak", rpb.astype(F32), jnp.asarray(onehot), precision=lax.Precision.HIGHEST)
    tbl = jnp.where(valid[:, None, :], toe, -jnp.inf)
    return tbl.reshape(tbl.shape[:-2] + (-1,))


def _head_expand():
    e = np.zeros((2, LANES, SSD_INNER), np.float32)
    for d in range(2):
        for h in range(SSD_HEADS):
            e[d, d * SSD_HEADS + h, h * SSD_HEAD_DIM:(h + 1) * SSD_HEAD_DIM] = 1.0
    return jnp.asarray(e, BF16)


def _seg_ones():
    i = np.arange(SEG_BLOCK)
    return jnp.asarray((i[:, None] // 64 == i[None, :] // 64).astype(np.float32), BF16)


def _pad_lanes(v, n):
    return jnp.pad(v.reshape(1, -1), ((0, 0), (0, n - v.size)))


W_TILE = 512
W_COLS = 30 * W_TILE
W_SRC_GATES = (1024 + 1536 + 6 * HEAD_COLS) // W_TILE
W_SRC_QKV = (1024 + 1536) // W_TILE
W_N_GATE_TILES = COL_Z // W_TILE
W_N_SSD_TILES = (P1_COLS - COL_Z) // W_TILE
W_SHIFT = 32
W_BLOCK_DIFF = P1_COLS // (3 * HEAD_COLS)
W_BLOCK_NA = W_BLOCK_DIFF + 1


def _w_src_tile(j):
    return jnp.where(j < W_N_GATE_TILES, W_SRC_GATES + j,
                     jnp.where(j < W_N_GATE_TILES + W_N_SSD_TILES, j - W_N_GATE_TILES,
                               W_SRC_QKV + j - W_N_GATE_TILES - W_N_SSD_TILES))


def _w_prep_kernel(a_ref, b_ref, o_ref):
    j = pl.program_id(2)
    aligned = (j >= W_N_GATE_TILES) & (j < W_N_GATE_TILES + W_N_SSD_TILES)

    @pl.when(aligned)
    def _():
        o_ref[...] = a_ref[...].astype(BF16)

    @pl.when(jnp.logical_not(aligned))
    def _():
        x = jnp.concatenate([a_ref[...], b_ref[...]], axis=1)
        width = x.shape[1]
        o_ref[...] = pltpu.roll(x, width - W_SHIFT, 1)[:, :W_TILE].astype(BF16)


def _prep_w_in(w_in):
    depth, d, n = w_in.shape
    assert n == 1024 + 1536 + 32 + 6 * HEAD_COLS + COL_Z
    rt = 1024
    return pl.pallas_call(
        _w_prep_kernel,
        grid=(depth, d // rt, W_COLS // W_TILE),
        in_specs=[
            pl.BlockSpec((None, rt, W_TILE), lambda l, r, j: (l, r, _w_src_tile(j))),
            pl.BlockSpec((None, rt, LANES), lambda l, r, j: (l, r, (W_TILE // LANES) * (_w_src_tile(j) + 1))),
        ],
        out_specs=pl.BlockSpec((None, rt, W_TILE), lambda l, r, j: (l, r, j)),
        out_shape=jax.ShapeDtypeStruct((depth, d, W_COLS), BF16),
        compiler_params=_cparams("arbitrary", "arbitrary", "arbitrary"),
        name="w_in_prep",
    )(w_in, w_in)


def _layer_params(l, norm1_w, norm2_w, ssd_conv_w, ssd_conv_b, ssd_dt_bias, ssd_a_log, ssd_d, ssd_norm_w,
                  diff_q_norm, diff_k_norm, diff_lam, diff_subln_w, na_q_norm, na_k_norm, ffn_conv_w, ffn_conv_b):
    rep = HEAD_COLS // DIFF_HEAD_DIM
    return {
        "norm1_w": norm1_w[l].reshape(1, -1), "norm2_w": norm2_w[l].reshape(1, -1),
        "diff_qn": jnp.tile(diff_q_norm[l], rep).reshape(1, -1), "diff_kn": jnp.tile(diff_k_norm[l], rep).reshape(1, -1),
        "na_qn": jnp.tile(na_q_norm[l], rep).reshape(1, -1), "na_kn": jnp.tile(na_k_norm[l], rep).reshape(1, -1),
        "cw_xs": ssd_conv_w[l][:, :SSD_INNER], "cw_bc": ssd_conv_w[l][:, SSD_INNER:],
        "cb_xs": ssd_conv_b[l][:SSD_INNER].reshape(1, -1), "cb_bc": ssd_conv_b[l][SSD_INNER:].reshape(1, -1),
        "dt_bias": _pad_lanes(ssd_dt_bias[l], LANES), "a_log": _pad_lanes(ssd_a_log[l], LANES),
        "d_exp": jnp.repeat(ssd_d[l], SSD_HEAD_DIM).reshape(1, -1), "ssd_nw": ssd_norm_w[l].reshape(1, -1),
        "e_heads": _head_expand(),
        "diff_lam": diff_lam[l], "subln_w": diff_subln_w[l].reshape(1, -1),
        "ffn_cw": ffn_conv_w[l], "ffn_cb": ffn_conv_b[l].reshape(1, -1),
    }


def _trunk_layer(x2d, h1, nb, seq, mod, mod_row, pw, layer, ctx_out, cache, consts, next_norm):
    wts = consts["weights"]
    u1 = _proj1(h1, wts["in"], layer)
    lam_init = 0.8 - 0.6 * math.exp(-0.3 * layer)
    bd = consts["bd"]
    if cache is None:
        depth, dk, dv, nk, nv, st = ctx_out
        qd, dk, dv = _qkv(h1, wts["in"], W_BLOCK_DIFF, pw["diff_qn"], pw["diff_kn"], bd, nb, seq, layer, (depth, dk, dv),
                          "proj_diff_qkv")
        qn, nk, nv = _qkv(h1, wts["in"], W_BLOCK_NA, pw["na_qn"], pw["na_kn"], bd, nb, seq, layer, (depth, nk, nv),
                          "proj_na_qkv")
        y_a, st = _ssd(u1, nb, seq, pw, None, layer, (depth, st))
        y_b = _ctx_attn(qd, dk, dv, nb, seq, layer, [pw["diff_lam"], pw["subln_w"]],
                        functools.partial(_ctx_diff_kernel, lam_init=lam_init), "ctx_diff_attn")
        y_c = _ctx_attn(qn, nk, nv, nb, seq, layer, [], _ctx_softmax_kernel, "ctx_softmax_attn")
        ctx_out = (depth, dk, dv, nk, nv, st)
    else:
        cdk, cdv, cnk, cnv, h0 = cache
        qd, kd, vd = _qkv(h1, wts["in"], W_BLOCK_DIFF, pw["diff_qn"], pw["diff_kn"], bd, nb, seq, layer, None, "proj_diff_qkv")
        qn, kn, vn = _qkv(h1, wts["in"], W_BLOCK_NA, pw["na_qn"], pw["na_kn"], bd, nb, seq, layer, None, "proj_na_qkv")
        y_a, _ = _ssd(u1, nb, seq, pw, h0, layer, None)
        y_b = _lat_diff(qd, kd, vd, nb, seq, cdk, cdv, layer, consts["cos"], consts["sin"], pw["diff_lam"],
                        pw["subln_w"], lam_init)
        y_c = _lat_na(qn, kn, vn, nb, seq, cnk, cnv, layer, pw["na_table"])
    merged = _merge(y_a, y_b, y_c, wts["a"], wts["b"], wts["c"], u1, layer)
    x2d, h2 = _resid_mm(merged, wts["out"], layer, x2d, mod, 2, TM_OUT, mod_row, (pw["norm2_w"], mod, 3, 4),
                        "out_proj")
    act = _ffn_up(h2, wts["up"], layer, pw["ffn_cw"], pw["ffn_cb"], seq)
    x2d, h1_next = _resid_mm(act, wts["down"], layer, x2d, mod, 5, TM_DOWN, mod_row, next_norm, "ffn_down")
    return x2d, h1_next, ctx_out


def kernel(x_prompt, x_sample, c, cache_diff_k, cache_diff_v, cache_na_k, cache_na_v, state_ssm, c_ctx, norm1_w, norm2_w, w_ada, b_ada, w_in, ssd_conv_w, ssd_conv_b, ssd_dt_bias, ssd_a_log, ssd_d, ssd_norm_w, diff_q_norm, diff_k_norm, diff_lam, diff_subln_w, na_q_norm, na_k_norm, na_rpb, w_branch_a, w_branch_b, w_branch_c, w_out, ffn_w_up, ffn_conv_w, ffn_conv_b, ffn_w_down):
    batch, seq, d = x_prompt.shape
    dec_batch, dec_seq, _ = x_sample.shape
    depth = w_in.shape[0]
    past = cache_diff_k.shape[2]
    assert d == D_MODEL and dec_batch + 1 <= 8 and seq % SSD_CHUNK == 0 and dec_seq % (NA_KH * GRID_W) == 0

    c_rows = jnp.concatenate([c_ctx.reshape(1, d), c, jnp.zeros((8 - 1 - dec_batch, d), F32)], axis=0)
    mod_all = _ada(c_rows, w_ada, b_ada).reshape(depth, 8, 6, d)

    cos, sin = _rope_tables(dec_seq)
    consts = {"bd": _seg_ones(), "cos": cos, "sin": sin}
    na_tables = _na_bias_table(na_rpb)
    cdk = cache_diff_k.reshape(dec_batch, depth, past, HEAD_COLS)
    cdv = cache_diff_v.reshape(dec_batch, depth, past, HEAD_COLS)
    cnk = cache_na_k.reshape(dec_batch, depth, past, HEAD_COLS)
    cnv = cache_na_v.reshape(dec_batch, depth, past, HEAD_COLS)
    h0 = state_ssm.reshape(dec_batch, depth, 2, SSD_INNER, SSD_STATE)

    consts["weights"] = {
        "in": _prep_w_in(w_in),
        "a": w_branch_a.astype(BF16), "b": w_branch_b.astype(BF16), "c": w_branch_c.astype(BF16),
        "out": w_out.astype(BF16), "up": ffn_w_up, "down": ffn_w_down.astype(BF16),
    }
    pws = []
    for l in range(depth):
        pw = _layer_params(l, norm1_w, norm2_w, ssd_conv_w, ssd_conv_b, ssd_dt_bias, ssd_a_log, ssd_d,
                           ssd_norm_w, diff_q_norm, diff_k_norm, diff_lam, diff_subln_w, na_q_norm, na_k_norm,
                           ffn_conv_w, ffn_conv_b)
        pw["na_table"] = na_tables[l]
        pws.append(pw)

    ctx_row = lambda i, tm: 0
    lat_row = lambda i, tm: 1 + (i * tm) // dec_seq
    y_p = x_prompt.reshape(batch * seq, d)
    y_s = x_sample.reshape(dec_batch * dec_seq, d)
    h_p = _norm(y_p, pws[0]["norm1_w"], mod_all[0], ctx_row)
    h_s = _norm(y_s, pws[0]["norm1_w"], mod_all[0], lat_row)
    ctx_out = (depth, None, None, None, None, None)
    cache = (cdk, cdv, cnk, cnv, h0)
    for l in range(depth):
        next_norm = (pws[l + 1]["norm1_w"], mod_all[l + 1], 0, 1) if l + 1 < depth else None
        y_p, h_p, ctx_out = _trunk_layer(y_p, h_p, batch, seq, mod_all[l], ctx_row, pws[l], l, ctx_out, None,
                                         consts, next_norm)
        y_s, h_s, _ = _trunk_layer(y_s, h_s, dec_batch, dec_seq, mod_all[l], lat_row, pws[l], l, None, cache,
                                   consts, next_norm)
    _, dk, dv, nk, nv, st = ctx_out
    return (y_p.reshape(batch, seq, d), y_s.reshape(dec_batch, dec_seq, d),
            dk.reshape(batch, depth, seq, DIFF_HEADS, 2, DIFF_HEAD_DIM),
            dv.reshape(batch, depth, seq, DIFF_HEADS, 2 * DIFF_HEAD_DIM),
            nk.reshape(batch, depth, seq, NA_HEADS, NA_HEAD_DIM),
            nv.reshape(batch, depth, seq, NA_HEADS, NA_HEAD_DIM),
            st.reshape(batch, depth, 2, SSD_HEADS, SSD_HEAD_DIM, SSD_STATE))
```

```python
import functools
import math

import numpy as np
import jax
import jax.numpy as jnp
from jax import lax
from jax.experimental import pallas as pl
from jax.experimental.pallas import tpu as pltpu

F32 = jnp.float32
BF16 = jnp.bfloat16

D_MODEL = 2048
GRID_W = 64
SSD_INNER = 1024
SSD_HEAD_DIM = 64
SSD_HEADS = 16
SSD_GROUPS = 2
SSD_STATE = 128
SSD_CHUNK = 128
SSD_BC = 2 * SSD_GROUPS * SSD_STATE
DIFF_HEADS = 8
DIFF_HEAD_DIM = 64
NA_HEADS = 16
NA_HEAD_DIM = 64
NA_KH = 8
NA_KW = 16
HEAD_COLS = 1024
D_FF = 5632
ROPE_BASE = 10000.0
EPS = 1e-6

LANES = 128
VMEM_LIMIT_BYTES = 56 * 1024 * 1024
SUB = 512
SEG_BLOCK = 256

COL_G = 0
COL_Z = 6144
COL_XS = 7168
COL_BC = 8192
COL_DT = 8704
P1_COLS = 9216
P1_CHUNK = 1536
N_GATE_CHUNKS = COL_Z // P1_CHUNK

TM_NORM = 1024
TM_PROJ = 512
TM_P1 = 1024
TM_MERGE = 512
TM_OUT = 512
TM_UP = 1024
TM_DOWN = 256


def _cparams(*sem):
    return pltpu.CompilerParams(dimension_semantics=sem, vmem_limit_bytes=VMEM_LIMIT_BYTES)


def _resident(shape, layer):
    return pl.BlockSpec((None,) + tuple(shape), lambda *_: (layer, 0, 0), pipeline_mode=pl.Buffered(1))


def _sigmoid(x):
    return 1.0 / (1.0 + jnp.exp(-x))


def _silu(x):
    return x * _sigmoid(x)


def _bdot(a, b):
    return jnp.dot(a, b, preferred_element_type=F32)


def _bdot_nt(a, b):
    return lax.dot_general(a, b, (((1,), (1,)), ((), ())), preferred_element_type=F32)


def _split3(x):
    p1 = x.astype(BF16)
    r1 = x - p1.astype(F32)
    p2 = r1.astype(BF16)
    p3 = (r1 - p2.astype(F32)).astype(BF16)
    return p1, p2, p3


def _sel_right(x, e):
    p1, p2, _ = _split3(x)
    return _bdot(p1, e) + _bdot(p2, e)


def _sel_left(t, x):
    p1, p2, p3 = _split3(x)
    return _bdot(t, p1) + _bdot(t, p2) + _bdot(t, p3)


def _seg64_rms(a, bd):
    ss = _bdot((a * a).astype(BF16), bd)
    return a * lax.rsqrt(ss * (1.0 / 64.0) + EPS)


def _modulated_norm(x, nw, shift, scale):
    ms = jnp.mean(x * x, axis=-1, keepdims=True)
    y = x * lax.rsqrt(ms + EPS) * nw
    return y * (1.0 + scale) + shift


def _ada_kernel(c_ref, w_ref, b_ref, o_ref):
    c = c_ref[...]
    s = _silu(c).astype(BF16)
    o_ref[0] = _bdot(s, w_ref[0].astype(BF16)) + b_ref[0]


def _ada(c_rows, w_ada, b_ada):
    depth, d, n = w_ada.shape
    tn = 1024
    return pl.pallas_call(
        _ada_kernel,
        grid=(depth, n // tn),
        in_specs=[
            pl.BlockSpec((8, d), lambda l, j: (0, 0)),
            pl.BlockSpec((1, d, tn), lambda l, j: (l, 0, j)),
            pl.BlockSpec((1, 1, tn), lambda l, j: (l, 0, j)),
        ],
        out_specs=pl.BlockSpec((1, 8, tn), lambda l, j: (l, 0, j)),
        out_shape=jax.ShapeDtypeStruct((depth, 8, n), F32),
        compiler_params=_cparams("arbitrary", "arbitrary"),
        name="ada_mod",
    )(c_rows, w_ada, b_ada.reshape(depth, 1, n))


def _norm_kernel(x_ref, nw_ref, mod_ref, h_ref):
    h_ref[...] = _modulated_norm(x_ref[...], nw_ref[...], mod_ref[0, 0:1, :], mod_ref[0, 1:2, :]).astype(BF16)


def _norm(x2d, nw, mod, mod_row):
    t, d = x2d.shape
    tm = TM_NORM
    return pl.pallas_call(
        _norm_kernel,
        grid=(t // tm,),
        in_specs=[
            pl.BlockSpec((tm, d), lambda i: (i, 0)),
            pl.BlockSpec((1, d), lambda i: (0, 0)),
            pl.BlockSpec((1, 6, d), lambda i: (mod_row(i, tm), 0, 0)),
        ],
        out_specs=pl.BlockSpec((tm, d), lambda i: (i, 0)),
        out_shape=jax.ShapeDtypeStruct((t, d), BF16),
        compiler_params=_cparams("arbitrary"),
        name="norm_mod",
    )(x2d, nw, mod)


def _proj1_kernel(h_ref, wg_ref, ws_ref, o_ref):
    c = pl.program_id(0)
    h = h_ref[...]

    def tiles(w_ref, epilogue):
        for s in range(P1_CHUNK // SUB):
            cols = slice(s * SUB, (s + 1) * SUB)
            o_ref[:, cols] = epilogue(_bdot(h, w_ref[:, cols]))

    @pl.when(c < N_GATE_CHUNKS)
    def _():
        tiles(wg_ref, _sigmoid)

    @pl.when(c >= N_GATE_CHUNKS)
    def _():
        tiles(ws_ref, lambda a: a)


def _proj1(h, w_gate, w_ssd, layer):
    t, d = h.shape
    tm = TM_P1
    return pl.pallas_call(
        _proj1_kernel,
        grid=(P1_COLS // P1_CHUNK, t // tm),
        in_specs=[
            pl.BlockSpec((tm, d), lambda c, i: (i, 0)),
            pl.BlockSpec((None, d, P1_CHUNK), lambda c, i: (layer, 0, jnp.minimum(c, N_GATE_CHUNKS - 1))),
            pl.BlockSpec((None, d, P1_CHUNK), lambda c, i: (layer, 0, jnp.maximum(c - N_GATE_CHUNKS, 0))),
        ],
        out_specs=pl.BlockSpec((tm, P1_CHUNK), lambda c, i: (i, c)),
        out_shape=jax.ShapeDtypeStruct((t, P1_COLS), F32),
        compiler_params=_cparams("arbitrary", "arbitrary"),
        name="proj_gates_ssd",
    )(h, w_gate, w_ssd)


def _qkv_kernel(*refs, n_alias):
    h_ref, w_ref, qnw_ref, knw_ref, bd_ref = refs[:5]
    q_ref, k_ref, v_ref = refs[5 + n_alias:]
    h = h_ref[...]
    bd = bd_ref[...]
    for sec, (o_ref, nw_ref) in enumerate(((q_ref, qnw_ref), (k_ref, knw_ref), (v_ref, None))):
        for s in range(HEAD_COLS // SUB):
            acc = _bdot(h, w_ref[:, sec * HEAD_COLS + s * SUB: sec * HEAD_COLS + (s + 1) * SUB])
            if nw_ref is not None:
                parts = [_seg64_rms(acc[:, c * SEG_BLOCK:(c + 1) * SEG_BLOCK], bd) for c in range(SUB // SEG_BLOCK)]
                acc = jnp.concatenate(parts, axis=1) * nw_ref[:, s * SUB:(s + 1) * SUB]
            o_ref[..., s * SUB:(s + 1) * SUB] = acc.reshape(o_ref.shape[:-1] + (SUB,)).astype(o_ref.dtype)


def _qkv(h, w, qnw, knw, bd, nb, seq, layer, kv_prev, name):
    t, d = h.shape
    tm = TM_PROJ
    in_specs = [
        pl.BlockSpec((tm, d), lambda i: (i, 0)),
        _resident((d, 3 * HEAD_COLS), layer),
        pl.BlockSpec((1, HEAD_COLS), lambda i: (0, 0)),
        pl.BlockSpec((1, HEAD_COLS), lambda i: (0, 0)),
        pl.BlockSpec((SEG_BLOCK, SEG_BLOCK), lambda i: (0, 0)),
    ]
    args = [h, w, qnw, knw, bd]
    q_spec = pl.BlockSpec((tm, HEAD_COLS), lambda i: (i, 0))
    q_shape = jax.ShapeDtypeStruct((t, HEAD_COLS), F32 if kv_prev is None else BF16)
    aliases = {}
    n_alias = 0
    if kv_prev is None:
        kv_spec, kv_shape = q_spec, q_shape
    else:
        depth, k_prev, v_prev = kv_prev
        assert tm % seq == 0
        kv_spec = pl.BlockSpec((tm // seq, None, seq, HEAD_COLS), lambda i: (i, layer, 0, 0))
        kv_shape = jax.ShapeDtypeStruct((nb, depth, seq, HEAD_COLS), F32)
        if k_prev is not None:
            in_specs += [pl.BlockSpec(memory_space=pl.ANY)] * 2
            args += [k_prev, v_prev]
            aliases = {5: 1, 6: 2}
            n_alias = 2
    return pl.pallas_call(
        functools.partial(_qkv_kernel, n_alias=n_alias),
        grid=(t // tm,),
        in_specs=in_specs,
        out_specs=[q_spec, kv_spec, kv_spec],
        out_shape=[q_shape, kv_shape, kv_shape],
        input_output_aliases=aliases,
        compiler_params=_cparams("arbitrary"),
        name=name,
    )(*args)


def _ssd_kernel(*refs, seq, has_h0, has_state_out, n_alias):
    (z_ref, xs_ref, bc_ref, dt_ref, cwx_ref, cbx_ref, cwb_ref, cbb_ref, dtb_ref, alog_ref,
     dexp_ref, nw_ref, e_ref) = refs[:13]
    rest = list(refs[13:])
    h0_ref = rest.pop(0) if has_h0 else None
    rest = rest[n_alias:]
    y_ref = rest.pop(0)
    st_ref = rest.pop(0) if has_state_out else None
    xs_s, bc_s, ya_s, st_s = rest

    q = SSD_CHUNK
    nc = seq // q
    half = SSD_INNER // SSD_GROUPS

    def conv_chunk(c, carry):
        r0 = pl.multiple_of(c * q, q)
        for src, dst, cw, cb in ((xs_ref, xs_s, cwx_ref, cbx_ref), (bc_ref, bc_s, cwb_ref, cbb_ref)):
            x = src[pl.ds(r0, q), :]
            xp = src[pl.ds(jnp.maximum(r0 - 1, 0), 1), :]
            xn = src[pl.ds(jnp.minimum(r0 + q, seq - 1), 1), :]
            xp = jnp.where(c == 0, 0.0, xp)
            xn = jnp.where(c == nc - 1, 0.0, xn)
            rows = lax.broadcasted_iota(jnp.int32, x.shape, 0)
            prev = jnp.where(rows == 0, xp, pltpu.roll(x, 1, 0))
            nxt = jnp.where(rows == q - 1, xn, pltpu.roll(x, q - 1, 0))
            y = cw[0:1, :] * prev + cw[1:2, :] * x + cw[2:3, :] * nxt + cb[...]
            dst[pl.ds(r0, q), :] = _silu(y)
        return carry

    lax.fori_loop(0, nc, conv_chunk, 0)

    a_vec = -jnp.exp(alog_ref[...])
    dt_bias = dtb_ref[...]
    ri = lax.broadcasted_iota(jnp.int32, (q, q), 0)
    ci = lax.broadcasted_iota(jnp.int32, (q, q), 1)

    for d in (0, 1):
        tri = (ri >= ci) if d == 0 else (ci >= ri)
        tri_b = jnp.where(tri, 1.0, 0.0).astype(BF16)
        for g in range(SSD_GROUPS):
            if has_h0:
                st_s[g] = h0_ref[d, g * half:(g + 1) * half, :].T
            else:
                st_s[g] = jnp.zeros((SSD_STATE, half), F32)

        def chunk(i, carry, d=d, tri=tri, tri_b=tri_b):
            c = i if d == 0 else nc - 1 - i
            r0 = pl.multiple_of(c * q, q)
            x_dt = dt_ref[pl.ds(r0, q), :] + dt_bias
            dtc = jnp.maximum(x_dt, 0.0) + jnp.log1p(jnp.exp(-jnp.abs(x_dt)))
            la = dtc * a_vec
            cum = _sel_left(tri_b, la)
            cum_t = cum.T
            e = e_ref[d]
            dt_x = _sel_right(dtc, e)
            cum_x = _sel_right(cum, e)
            last = q - 1 if d == 0 else 0
            cl = cum_x[last:last + 1, :]
            xd = xs_s[pl.ds(r0, q), :] * dt_x
            xdb = xd.astype(BF16)
            xdd = (xd * jnp.exp(cl - cum_x)).astype(BF16)
            ecum = jnp.exp(cum_x)
            bcv = bc_s[pl.ds(r0, q), :]
            for g in range(SSD_GROUPS):
                bg = bcv[:, g * SSD_STATE:(g + 1) * SSD_STATE]
                cg = bcv[:, (SSD_GROUPS + g) * SSD_STATE:(SSD_GROUPS + g + 1) * SSD_STATE]
                bgb = bg.astype(BF16)
                cgb = cg.astype(BF16)
                gm = _bdot_nt(cgb, bgb)
                st = st_s[g]
                y_off = _bdot(cgb, st.astype(BF16)) * ecum[:, g * half:(g + 1) * half]
                ys = []
                for hh in range(SSD_HEADS // SSD_GROUPS):
                    h = g * (SSD_HEADS // SSD_GROUPS) + hh
                    k = d * SSD_HEADS + h
                    decay = jnp.where(tri, jnp.exp(cum[:, k:k + 1] - cum_t[k:k + 1, :]), 0.0)
                    m = (gm * decay).astype(BF16)
                    ys.append(_bdot(m, xdb[:, h * SSD_HEAD_DIM:(h + 1) * SSD_HEAD_DIM]))
                yg = jnp.concatenate(ys, axis=1) + y_off
                if d == 0:
                    ya_s[pl.ds(r0, q), g * half:(g + 1) * half] = yg
                else:
                    ya_s[pl.ds(r0, q), g * half:(g + 1) * half] += yg
                st_s[g] = (st * jnp.exp(cl[:, g * half:(g + 1) * half])
                           + _bdot(bg.T.astype(BF16), xdd[:, g * half:(g + 1) * half]))
            return carry

        lax.fori_loop(0, nc, chunk, 0)
        if has_state_out:
            for g in range(SSD_GROUPS):
                st_ref[d, g * half:(g + 1) * half, :] = st_s[g].T

    def fin_chunk(c, carry):
        r0 = pl.multiple_of(c * q, q)
        y = ya_s[pl.ds(r0, q), :] + dexp_ref[...] * xs_s[pl.ds(r0, q), :]
        y = y * _silu(z_ref[pl.ds(r0, q), :])
        for g in range(SSD_GROUPS):
            v = y[:, g * half:(g + 1) * half]
            ms = jnp.mean(v * v, axis=-1, keepdims=True)
            out = v * lax.rsqrt(ms + EPS) * nw_ref[:, g * half:(g + 1) * half]
            y_ref[pl.ds(r0, q), g * half:(g + 1) * half] = out.astype(BF16)
        return carry

    lax.fori_loop(0, nc, fin_chunk, 0)


def _ssd(u1, nb, seq, pw, h0, layer, state_prev):
    has_h0 = h0 is not None
    has_state_out = state_prev is not None
    c1 = lambda b: (0, 0)
    in_specs = [
        pl.BlockSpec((seq, SSD_INNER), lambda b: (b, COL_Z // SSD_INNER)),
        pl.BlockSpec((seq, SSD_INNER), lambda b: (b, COL_XS // SSD_INNER)),
        pl.BlockSpec((seq, SSD_BC), lambda b: (b, COL_BC // SSD_BC)),
        pl.BlockSpec((seq, LANES), lambda b: (b, COL_DT // LANES)),
        pl.BlockSpec((3, SSD_INNER), c1),
        pl.BlockSpec((1, SSD_INNER), c1),
        pl.BlockSpec((3, SSD_BC), c1),
        pl.BlockSpec((1, SSD_BC), c1),
        pl.BlockSpec((1, LANES), c1),
        pl.BlockSpec((1, LANES), c1),
        pl.BlockSpec((1, SSD_INNER), c1),
        pl.BlockSpec((1, SSD_INNER), c1),
        pl.BlockSpec((2, LANES, SSD_INNER), lambda b: (0, 0, 0)),
    ]
    args = [u1, u1, u1, u1, pw["cw_xs"], pw["cb_xs"], pw["cw_bc"], pw["cb_bc"], pw["dt_bias"], pw["a_log"],
            pw["d_exp"], pw["ssd_nw"], pw["e_heads"]]
    if has_h0:
        in_specs.append(pl.BlockSpec((None, None, 2, SSD_INNER, SSD_STATE), lambda b: (b, layer, 0, 0, 0)))
        args.append(h0)
    out_specs = [pl.BlockSpec((seq, SSD_INNER), lambda b: (b, 0))]
    out_shape = [jax.ShapeDtypeStruct((nb * seq, SSD_INNER), BF16)]
    aliases = {}
    n_alias = 0
    if has_state_out:
        depth, prev = state_prev
        out_specs.append(pl.BlockSpec((None, None, 2, SSD_INNER, SSD_STATE), lambda b: (b, layer, 0, 0, 0)))
        out_shape.append(jax.ShapeDtypeStruct((nb, depth, 2, SSD_INNER, SSD_STATE), F32))
        if prev is not None:
            aliases = {len(args): 1}
            in_specs.append(pl.BlockSpec(memory_space=pl.ANY))
            args.append(prev)
            n_alias = 1
    res = pl.pallas_call(
        functools.partial(_ssd_kernel, seq=seq, has_h0=has_h0, has_state_out=has_state_out, n_alias=n_alias),
        grid=(nb,),
        in_specs=in_specs,
        out_specs=out_specs,
        out_shape=out_shape,
        input_output_aliases=aliases,
        scratch_shapes=[
            pltpu.VMEM((seq, SSD_INNER), F32),
            pltpu.VMEM((seq, SSD_BC), F32),
            pltpu.VMEM((seq, SSD_INNER), F32),
            pltpu.VMEM((SSD_GROUPS, SSD_STATE, SSD_INNER // SSD_GROUPS), F32),
        ],
        compiler_params=_cparams("arbitrary"),
        name="ssd_bidir",
    )(*args)
    return res if has_state_out else (res[0], None)


def _softmax_rows(parts, mxu_sums=False):
    m = functools.reduce(jnp.maximum, [jnp.max(s, axis=-1, keepdims=True) for s in parts])
    es = [jnp.exp(s - m) for s in parts]
    if mxu_sums:
        dens = [_bdot(e.astype(BF16), jnp.ones((e.shape[1], SEG_BLOCK), BF16)) for e in es]
        inv = 1.0 / functools.reduce(lambda a, b: a + b, dens)
        return [e * jnp.tile(inv, (1, e.shape[1] // SEG_BLOCK)) for e in es]
    den = functools.reduce(lambda a, b: a + b, [jnp.sum(e, axis=-1, keepdims=True) for e in es])
    inv = 1.0 / den
    return [e * inv for e in es]


def _exp_rows(parts):
    m = functools.reduce(jnp.maximum, [jnp.max(s, axis=-1, keepdims=True) for s in parts])
    return [jnp.exp(s - m).astype(BF16) for s in parts]


def _lane_halves(x):
    left = lax.broadcasted_iota(jnp.int32, x.shape, 1) < LANES // 2
    return jnp.where(left, x, 0.0).astype(BF16), jnp.where(left, 0.0, x).astype(BF16)


def _lambda(lp, lam_init):
    a = jnp.sum(lp[0:1, :] * lp[1:2, :], axis=-1, keepdims=True)
    b = jnp.sum(lp[2:3, :] * lp[3:4, :], axis=-1, keepdims=True)
    return jnp.exp(a) - jnp.exp(b) + lam_init


def _subln(o, w, lam_init):
    ms = jnp.mean(o * o, axis=-1, keepdims=True)
    return o * lax.rsqrt(ms + EPS) * w * (1.0 - lam_init)


def _rope(x, cos, sin):
    lane = lax.broadcasted_iota(jnp.int32, x.shape, 1)
    swapped = jnp.where((lane & 16) == 0, pltpu.roll(x, LANES - 16, 1), pltpu.roll(x, 16, 1))
    return x * cos + swapped * sin


def _ctx_diff_kernel(q_ref, k_ref, v_ref, lamp_ref, sw_ref, o_ref, *, lam_init):
    scale = DIFF_HEAD_DIM ** -0.5
    lam = _lambda(lamp_ref[...], lam_init)
    sw = sw_ref[...]
    for hb in range(DIFF_HEADS):
        ps = []
        for t in (0, 1):
            sl = slice(hb * LANES + t * DIFF_HEAD_DIM, hb * LANES + (t + 1) * DIFF_HEAD_DIM)
            s = _bdot_nt(q_ref[:, sl], k_ref[:, sl].astype(BF16)) * scale
            ps.append(_softmax_rows([s], mxu_sums=True)[0])
        att = (ps[0] - lam * ps[1]).astype(BF16)
        blk = slice(hb * LANES, (hb + 1) * LANES)
        o = _subln(_bdot(att, v_ref[:, blk].astype(BF16)), sw, lam_init)
        o_ref[:, blk] = o.astype(BF16)


def _ctx_softmax_kernel(q_ref, k_ref, v_ref, o_ref):
    scale = NA_HEAD_DIM ** -0.5
    ones_l, ones_r = _lane_halves(jnp.ones((k_ref.shape[0], LANES), F32))
    for hb in range(NA_HEADS // 2):
        blk = slice(hb * LANES, (hb + 1) * LANES)
        v_l, v_r = _lane_halves(v_ref[:, blk])
        es = []
        for t in (0, 1):
            sl = slice(hb * LANES + t * NA_HEAD_DIM, hb * LANES + (t + 1) * NA_HEAD_DIM)
            s = _bdot_nt(q_ref[:, sl], k_ref[:, sl].astype(BF16)) * scale
            es.append(_exp_rows([s])[0])
        num = _bdot(es[0], v_l) + _bdot(es[1], v_r)
        den = _bdot(es[0], ones_l) + _bdot(es[1], ones_r)
        o_ref[:, blk] = (num / den).astype(BF16)


def _ctx_attn(q, k_all, v_all, nb, seq, layer, extra, kern, name):
    kv_spec = pl.BlockSpec((None, None, seq, HEAD_COLS), lambda b: (b, layer, 0, 0))
    extra_specs = [pl.BlockSpec(a.shape, lambda b: (0, 0)) for a in extra]
    return pl.pallas_call(
        kern,
        grid=(nb,),
        in_specs=[pl.BlockSpec((seq, HEAD_COLS), lambda b: (b, 0)), kv_spec, kv_spec] + extra_specs,
        out_specs=pl.BlockSpec((seq, HEAD_COLS), lambda b: (b, 0)),
        out_shape=jax.ShapeDtypeStruct((nb * seq, HEAD_COLS), BF16),
        compiler_params=_cparams("arbitrary"),
        name=name,
    )(q, k_all, v_all, *extra)


LAT_DIFF_TQ = 256


def _lat_diff_kernel(q_ref, k_ref, v_ref, ck_ref, cv_ref, cos_ref, sin_ref, lamp_ref, sw_ref, o_ref, *, lam_init):
    scale = DIFF_HEAD_DIM ** -0.5
    k = _rope(k_ref[...], cos_ref[...], sin_ref[...])
    k_t = [k[:, t * DIFF_HEAD_DIM:(t + 1) * DIFF_HEAD_DIM].astype(BF16) for t in (0, 1)]
    ck_t = [ck_ref[:, t * DIFF_HEAD_DIM:(t + 1) * DIFF_HEAD_DIM].astype(BF16) for t in (0, 1)]
    v = v_ref[...].astype(BF16)
    cv = cv_ref[...].astype(BF16)
    lam = _lambda(lamp_ref[...], lam_init)
    sw = sw_ref[...]
    for qb in range(q_ref.shape[0] // LAT_DIFF_TQ):
        rs = slice(qb * LAT_DIFF_TQ, (qb + 1) * LAT_DIFF_TQ)
        q = _rope(q_ref[rs, :], cos_ref[rs, :], sin_ref[rs, :])
        pl_, pc_ = [], []
        for t in (0, 1):
            qt = q[:, t * DIFF_HEAD_DIM:(t + 1) * DIFF_HEAD_DIM].astype(BF16)
            p_loc, p_ctx = _softmax_rows([_bdot_nt(qt, k_t[t]) * scale, _bdot_nt(qt, ck_t[t]) * scale])
            pl_.append(p_loc)
            pc_.append(p_ctx)
        a_loc = (pl_[0] - lam * pl_[1]).astype(BF16)
        a_ctx = (pc_[0] - lam * pc_[1]).astype(BF16)
        o = _bdot(a_loc, v) + _bdot(a_ctx, cv)
        o_ref[rs, :] = _subln(o, sw, lam_init).astype(BF16)


def _lat_diff(q, k, v, nb, seq, cache_k, cache_v, layer, cos, sin, lamp, sw, lam_init):
    past = cache_k.shape[2]
    cache_spec = pl.BlockSpec((None, None, past, LANES), lambda b, h: (b, layer, 0, h))
    loc_spec = pl.BlockSpec((seq, LANES), lambda b, h: (b, h))
    tab_spec = pl.BlockSpec((seq, LANES), lambda b, h: (0, 0))
    return pl.pallas_call(
        functools.partial(_lat_diff_kernel, lam_init=lam_init),
        grid=(nb, DIFF_HEADS),
        in_specs=[
            loc_spec, loc_spec, loc_spec, cache_spec, cache_spec, tab_spec, tab_spec,
            pl.BlockSpec((4, DIFF_HEAD_DIM), lambda b, h: (0, 0)),
            pl.BlockSpec((1, LANES), lambda b, h: (0, 0)),
        ],
        out_specs=loc_spec,
        out_shape=jax.ShapeDtypeStruct((nb * seq, HEAD_COLS), BF16),
        compiler_params=_cparams("arbitrary", "arbitrary"),
        name="lat_diff_attn",
    )(q, k, v, cache_k, cache_v, cos, sin, lamp, sw)


NA_Q_ROWS = 4


def _lat_na_kernel(q_ref, k_ref, v_ref, ck_ref, cv_ref, tb_ref, o_ref, bias_s, *, rows):
    scale = NA_HEAD_DIM ** -0.5
    kh = min(NA_KH, rows)
    win0 = lambda qr: min(max(qr - kh // 2, 0), rows - kh)

    @pl.when(pl.program_id(1) == 0)
    def _():
        bias_s[...] = jnp.full(bias_s.shape, -jnp.inf, F32)
        for t in (0, 1):
            for qr in range(rows):
                r0 = win0(qr)
                a0 = r0 - qr + NA_KH - 1
                bias_s[t, qr * GRID_W:(qr + 1) * GRID_W, r0 * GRID_W:(r0 + kh) * GRID_W] = (
                    tb_ref[t, :, a0 * GRID_W:(a0 + kh) * GRID_W])

    k_h = [k_ref[:, t * NA_HEAD_DIM:(t + 1) * NA_HEAD_DIM].astype(BF16) for t in (0, 1)]
    ck_h = [ck_ref[:, t * NA_HEAD_DIM:(t + 1) * NA_HEAD_DIM].astype(BF16) for t in (0, 1)]
    v_h = _lane_halves(v_ref[...])
    cv_h = _lane_halves(cv_ref[...])
    ones_h = lambda n: _lane_halves(jnp.ones((n, LANES), F32))
    cones_h = ones_h(cv_ref.shape[0])
    for qb in range(rows // NA_Q_ROWS):
        lo = win0(qb * NA_Q_ROWS) // 2 * 2
        hi = min(rows, (win0((qb + 1) * NA_Q_ROWS - 1) + kh + 1) // 2 * 2)
        rs = slice(qb * NA_Q_ROWS * GRID_W, (qb + 1) * NA_Q_ROWS * GRID_W)
        ks = slice(lo * GRID_W, hi * GRID_W)
        num = den = None
        for t in (0, 1):
            qh = q_ref[rs, t * NA_HEAD_DIM:(t + 1) * NA_HEAD_DIM].astype(BF16)
            s_loc = _bdot_nt(qh, k_h[t][ks]) * scale + bias_s[t, rs, ks]
            s_ctx = _bdot_nt(qh, ck_h[t]) * scale
            e_loc, e_ctx = _exp_rows([s_loc, s_ctx])
            n_t = _bdot(e_loc, v_h[t][ks]) + _bdot(e_ctx, cv_h[t])
            d_t = _bdot(e_loc, ones_h((hi - lo) * GRID_W)[t]) + _bdot(e_ctx, cones_h[t])
            num = n_t if num is None else num + n_t
            den = d_t if den is None else den + d_t
        o_ref[rs, :] = (num / den).astype(BF16)


def _lat_na(q, k, v, nb, seq, cache_k, cache_v, layer, table):
    past = cache_k.shape[2]
    rows = seq // GRID_W
    ncol = table.shape[-1]
    loc_spec = pl.BlockSpec((seq, LANES), lambda h, b: (b, h))
    cache_spec = pl.BlockSpec((None, None, past, LANES), lambda h, b: (b, layer, 0, h))
    return pl.pallas_call(
        functools.partial(_lat_na_kernel, rows=rows),
        grid=(NA_HEADS // 2, nb),
        in_specs=[loc_spec, loc_spec, loc_spec, cache_spec, cache_spec,
                  pl.BlockSpec((2, GRID_W, ncol), lambda h, b: (h, 0, 0))],
        out_specs=pl.BlockSpec((seq, LANES), lambda h, b: (b, h)),
        out_shape=jax.ShapeDtypeStruct((nb * seq, HEAD_COLS), BF16),
        scratch_shapes=[pltpu.VMEM((2, seq, seq), F32)],
        compiler_params=_cparams("arbitrary", "arbitrary"),
        name="lat_nbr_attn",
    )(q, k, v, cache_k, cache_v, table)


def _merge_kernel(ya_ref, yb_ref, yc_ref, wa_ref, wb_ref, wc_ref, ga_ref, gb_ref, gc_ref, o_ref):
    ya, yb, yc = ya_ref[...], yb_ref[...], yc_ref[...]
    for s in range(D_MODEL // SUB):
        cols = slice(s * SUB, (s + 1) * SUB)
        m = (ga_ref[:, cols] * _bdot(ya, wa_ref[:, cols])
             + gb_ref[:, cols] * _bdot(yb, wb_ref[:, cols])
             + gc_ref[:, cols] * _bdot(yc, wc_ref[:, cols]))
        o_ref[:, cols] = m.astype(BF16)


def _merge(ya, yb, yc, wa, wb, wc, u1, layer):
    t, kk = ya.shape
    n = wa.shape[2]
    tm = TM_MERGE
    ysp = pl.BlockSpec((tm, kk), lambda i: (i, 0))
    wsp = _resident((kk, n), layer)
    return pl.pallas_call(
        _merge_kernel,
        grid=(t // tm,),
        in_specs=[ysp, ysp, ysp, wsp, wsp, wsp,
                  pl.BlockSpec((tm, n), lambda i: (i, 0)),
                  pl.BlockSpec((tm, n), lambda i: (i, 1)),
                  pl.BlockSpec((tm, n), lambda i: (i, 2))],
        out_specs=pl.BlockSpec((tm, n), lambda i: (i, 0)),
        out_shape=jax.ShapeDtypeStruct((t, n), BF16),
        compiler_params=_cparams("arbitrary"),
        name="branch_merge",
    )(ya, yb, yc, wa, wb, wc, u1, u1, u1)


def _resid_mm_kernel(*refs, gate_row, norm_rows):
    if norm_rows is None:
        a_ref, w_ref, x_ref, mod_ref, xo_ref = refs
    else:
        a_ref, w_ref, x_ref, mod_ref, nw_ref, nmod_ref, xo_ref, h_ref = refs
    a = a_ref[...]
    for s in range(D_MODEL // SUB):
        cols = slice(s * SUB, (s + 1) * SUB)
        xo_ref[:, cols] = x_ref[:, cols] + mod_ref[0, gate_row:gate_row + 1, cols] * _bdot(a, w_ref[:, cols])
    if norm_rows is not None:
        shift_row, scale_row = norm_rows
        h = _modulated_norm(xo_ref[...], nw_ref[...], nmod_ref[0, shift_row:shift_row + 1, :],
                            nmod_ref[0, scale_row:scale_row + 1, :])
        h_ref[...] = h.astype(BF16)


def _resid_mm(a, w, layer, x2d, mod, gate_row, tm, mod_row, norm, name):
    t, kk = a.shape
    d = w.shape[2]
    row = lambda i: (i, 0)
    mrow = lambda i: (mod_row(i, tm), 0, 0)
    in_specs = [pl.BlockSpec((tm, kk), row), _resident((kk, d), layer), pl.BlockSpec((tm, d), row),
                pl.BlockSpec((1, 6, d), mrow)]
    args = [a, w, x2d, mod]
    out_specs = [pl.BlockSpec((tm, d), row)]
    out_shape = [jax.ShapeDtypeStruct((t, d), F32)]
    norm_rows = None
    if norm is not None:
        nw, nmod, shift_row, scale_row = norm
        norm_rows = (shift_row, scale_row)
        in_specs += [pl.BlockSpec((1, d), lambda i: (0, 0)), pl.BlockSpec((1, 6, d), mrow)]
        args += [nw, nmod]
        out_specs.append(pl.BlockSpec((tm, d), row))
        out_shape.append(jax.ShapeDtypeStruct((t, d), BF16))
    res = pl.pallas_call(
        functools.partial(_resid_mm_kernel, gate_row=gate_row, norm_rows=norm_rows),
        grid=(t // tm,),
        in_specs=in_specs,
        out_specs=out_specs,
        out_shape=out_shape,
        compiler_params=_cparams("arbitrary"),
        name=name,
    )(*args)
    return (res[0], res[1]) if norm is not None else (res[0], None)


def _ffn_up_kernel(h_ref, wv_ref, wg_ref, cwv_ref, cwg_ref, cbv_ref, cbg_ref, o_ref, wv_s, wg_s, *, seq):
    @pl.when(pl.program_id(1) == 0)
    def _():
        wv_s[...] = wv_ref[...].astype(BF16)
        wg_s[...] = wg_ref[...].astype(BF16)

    h = h_ref[...]
    tm = h.shape[0]
    pos = lax.broadcasted_iota(jnp.int32, (tm, SUB), 0) % seq

    def conv(acc, cw, cb):
        prev = jnp.where(pos == 0, 0.0, pltpu.roll(acc, 1, 0))
        nxt = jnp.where(pos == seq - 1, 0.0, pltpu.roll(acc, tm - 1, 0))
        return cw[0:1, :] * prev + cw[1:2, :] * acc + cw[2:3, :] * nxt + cb[...]

    val = conv(_bdot(h, wv_s[...]), cwv_ref, cbv_ref)
    gt = conv(_bdot(h, wg_s[...]), cwg_ref, cbg_ref)
    o_ref[...] = (_silu(gt) * val).astype(BF16)


def _ffn_up(h, w_up, layer, cw, cb, seq):
    t, d = h.shape
    tm = max(TM_UP, seq)
    assert tm % seq == 0
    nt = D_FF // SUB
    return pl.pallas_call(
        functools.partial(_ffn_up_kernel, seq=seq),
        grid=(nt, t // tm),
        in_specs=[
            pl.BlockSpec((tm, d), lambda j, i: (i, 0)),
            pl.BlockSpec((None, d, SUB), lambda j, i: (layer, 0, j)),
            pl.BlockSpec((None, d, SUB), lambda j, i: (layer, 0, nt + j)),
            pl.BlockSpec((3, SUB), lambda j, i: (0, j)),
            pl.BlockSpec((3, SUB), lambda j, i: (0, nt + j)),
            pl.BlockSpec((1, SUB), lambda j, i: (0, j)),
            pl.BlockSpec((1, SUB), lambda j, i: (0, nt + j)),
        ],
        out_specs=pl.BlockSpec((tm, SUB), lambda j, i: (i, j)),
        out_shape=jax.ShapeDtypeStruct((t, D_FF), BF16),
        scratch_shapes=[pltpu.VMEM((d, SUB), BF16), pltpu.VMEM((d, SUB), BF16)],
        compiler_params=_cparams("arbitrary", "arbitrary"),
        name="ffn_up_conv",
    )(h, w_up, w_up, cw, cw, cb, cb)


def _rope_tables(seq):
    t = np.arange(seq)
    row = (t // GRID_W).astype(np.float32)
    col = (t % GRID_W).astype(np.float32)
    n_freq = DIFF_HEAD_DIM // 4
    inv = jnp.asarray(ROPE_BASE, F32) ** (-jnp.arange(n_freq, dtype=F32) / n_freq)
    ang_r = jnp.asarray(row)[:, None] * inv
    ang_c = jnp.asarray(col)[:, None] * inv
    cr, sr, cc, sc = jnp.cos(ang_r), jnp.sin(ang_r), jnp.cos(ang_c), jnp.sin(ang_c)
    cos64 = jnp.concatenate([cr, cr, cc, cc], axis=-1)
    sin64 = jnp.concatenate([-sr, sr, -sc, sc], axis=-1)
    return jnp.tile(cos64, (1, 2)), jnp.tile(sin64, (1, 2))


def _na_bias_table(rpb):
    qc = np.arange(GRID_W)[:, None]
    kc = np.arange(GRID_W)[None, :]
    onehot = (kc - qc + NA_KW - 1 == np.arange(2 * NA_KW - 1)[:, None, None]).astype(np.float32)
    wstart = np.clip(qc - NA_KW // 2, 0, GRID_W - NA_KW)
    valid = (kc >= wstart) & (kc < wstart + NA_KW)
    toe = jnp.einsum("...ab,bqk->...qak", rpb.astype(F32), jnp.asarray(onehot), precision=lax.Precision.HIGHEST)
    tbl = jnp.where(valid[:, None, :], toe, -jnp.inf)
    return tbl.reshape(tbl.shape[:-2] + (-1,))


def _head_expand():
    e = np.zeros((2, LANES, SSD_INNER), np.float32)
    for d in range(2):
        for h in range(SSD_HEADS):
            e[d, d * SSD_HEADS + h, h * SSD_HEAD_DIM:(h + 1) * SSD_HEAD_DIM] = 1.0
    return jnp.asarray(e, BF16)


def _seg_ones():
    i = np.arange(SEG_BLOCK)
    return jnp.asarray((i[:, None] // 64 == i[None, :] // 64).astype(np.float32), BF16)


def _pad_lanes(v, n):
    return jnp.pad(v.reshape(1, -1), ((0, 0), (0, n - v.size)))


def _split_w_in(w_in):
    o = np.cumsum([0, 1024 + 1536 + 32, 3 * HEAD_COLS, 3 * HEAD_COLS, 6144])
    ssd, wd, wn, g = [w_in[..., o[i]:o[i + 1]].astype(BF16) for i in range(4)]
    ssd = jnp.pad(ssd, ((0, 0), (0, 0), (0, P1_COLS - COL_Z - ssd.shape[-1])))
    return g, ssd, wd, wn


def _layer_params(l, norm1_w, norm2_w, ssd_conv_w, ssd_conv_b, ssd_dt_bias, ssd_a_log, ssd_d, ssd_norm_w,
                  diff_q_norm, diff_k_norm, diff_lam, diff_subln_w, na_q_norm, na_k_norm, ffn_conv_w, ffn_conv_b):
    rep = HEAD_COLS // DIFF_HEAD_DIM
    return {
        "norm1_w": norm1_w[l].reshape(1, -1), "norm2_w": norm2_w[l].reshape(1, -1),
        "diff_qn": jnp.tile(diff_q_norm[l], rep).reshape(1, -1), "diff_kn": jnp.tile(diff_k_norm[l], rep).reshape(1, -1),
        "na_qn": jnp.tile(na_q_norm[l], rep).reshape(1, -1), "na_kn": jnp.tile(na_k_norm[l], rep).reshape(1, -1),
        "cw_xs": ssd_conv_w[l][:, :SSD_INNER], "cw_bc": ssd_conv_w[l][:, SSD_INNER:],
        "cb_xs": ssd_conv_b[l][:SSD_INNER].reshape(1, -1), "cb_bc": ssd_conv_b[l][SSD_INNER:].reshape(1, -1),
        "dt_bias": _pad_lanes(ssd_dt_bias[l], LANES), "a_log": _pad_lanes(ssd_a_log[l], LANES),
        "d_exp": jnp.repeat(ssd_d[l], SSD_HEAD_DIM).reshape(1, -1), "ssd_nw": ssd_norm_w[l].reshape(1, -1),
        "e_heads": _head_expand(),
        "diff_lam": diff_lam[l], "subln_w": diff_subln_w[l].reshape(1, -1),
        "ffn_cw": ffn_conv_w[l], "ffn_cb": ffn_conv_b[l].reshape(1, -1),
    }


def _trunk_layer(x2d, h1, nb, seq, mod, mod_row, pw, layer, ctx_out, cache, consts, next_norm):
    wts = consts["weights"]
    u1 = _proj1(h1, wts["gate"], wts["ssd"], layer)
    lam_init = 0.8 - 0.6 * math.exp(-0.3 * layer)
    bd = consts["bd"]
    if cache is None:
        depth, dk, dv, nk, nv, st = ctx_out
        qd, dk, dv = _qkv(h1, wts["diff"], pw["diff_qn"], pw["diff_kn"], bd, nb, seq, layer, (depth, dk, dv),
                          "proj_diff_qkv")
        qn, nk, nv = _qkv(h1, wts["na"], pw["na_qn"], pw["na_kn"], bd, nb, seq, layer, (depth, nk, nv),
                          "proj_na_qkv")
        y_a, st = _ssd(u1, nb, seq, pw, None, layer, (depth, st))
        y_b = _ctx_attn(qd, dk, dv, nb, seq, layer, [pw["diff_lam"], pw["subln_w"]],
                        functools.partial(_ctx_diff_kernel, lam_init=lam_init), "ctx_diff_attn")
        y_c = _ctx_attn(qn, nk, nv, nb, seq, layer, [], _ctx_softmax_kernel, "ctx_softmax_attn")
        ctx_out = (depth, dk, dv, nk, nv, st)
    else:
        cdk, cdv, cnk, cnv, h0 = cache
        qd, kd, vd = _qkv(h1, wts["diff"], pw["diff_qn"], pw["diff_kn"], bd, nb, seq, layer, None, "proj_diff_qkv")
        qn, kn, vn = _qkv(h1, wts["na"], pw["na_qn"], pw["na_kn"], bd, nb, seq, layer, None, "proj_na_qkv")
        y_a, _ = _ssd(u1, nb, seq, pw, h0, layer, None)
        y_b = _lat_diff(qd, kd, vd, nb, seq, cdk, cdv, layer, consts["cos"], consts["sin"], pw["diff_lam"],
                        pw["subln_w"], lam_init)
        y_c = _lat_na(qn, kn, vn, nb, seq, cnk, cnv, layer, pw["na_table"])
    merged = _merge(y_a, y_b, y_c, wts["a"], wts["b"], wts["c"], u1, layer)
    x2d, h2 = _resid_mm(merged, wts["out"], layer, x2d, mod, 2, TM_OUT, mod_row, (pw["norm2_w"], mod, 3, 4),
                        "out_proj")
    act = _ffn_up(h2, wts["up"], layer, pw["ffn_cw"], pw["ffn_cb"], seq)
    x2d, h1_next = _resid_mm(act, wts["down"], layer, x2d, mod, 5, TM_DOWN, mod_row, next_norm, "ffn_down")
    return x2d, h1_next, ctx_out


def kernel(x_prompt, x_sample, c, cache_diff_k, cache_diff_v, cache_na_k, cache_na_v, state_ssm, c_ctx, norm1_w, norm2_w, w_ada, b_ada, w_in, ssd_conv_w, ssd_conv_b, ssd_dt_bias, ssd_a_log, ssd_d, ssd_norm_w, diff_q_norm, diff_k_norm, diff_lam, diff_subln_w, na_q_norm, na_k_norm, na_rpb, w_branch_a, w_branch_b, w_branch_c, w_out, ffn_w_up, ffn_conv_w, ffn_conv_b, ffn_w_down):
    batch, seq, d = x_prompt.shape
    dec_batch, dec_seq, _ = x_sample.shape
    depth = w_in.shape[0]
    past = cache_diff_k.shape[2]
    assert d == D_MODEL and dec_batch + 1 <= 8 and seq % SSD_CHUNK == 0 and dec_seq % (NA_KH * GRID_W) == 0

    c_rows = jnp.concatenate([c_ctx.reshape(1, d), c, jnp.zeros((8 - 1 - dec_batch, d), F32)], axis=0)
    mod_all = _ada(c_rows, w_ada, b_ada).reshape(depth, 8, 6, d)

    cos, sin = _rope_tables(dec_seq)
    consts = {"bd": _seg_ones(), "cos": cos, "sin": sin}
    na_tables = _na_bias_table(na_rpb)
    cdk = cache_diff_k.reshape(dec_batch, depth, past, HEAD_COLS)
    cdv = cache_diff_v.reshape(dec_batch, depth, past, HEAD_COLS)
    cnk = cache_na_k.reshape(dec_batch, depth, past, HEAD_COLS)
    cnv = cache_na_v.reshape(dec_batch, depth, past, HEAD_COLS)
    h0 = state_ssm.reshape(dec_batch, depth, 2, SSD_INNER, SSD_STATE)

    w_gate, w_ssd, w_diff, w_na = _split_w_in(w_in)
    consts["weights"] = {
        "gate": w_gate, "ssd": w_ssd, "diff": w_diff, "na": w_na,
        "a": w_branch_a.astype(BF16), "b": w_branch_b.astype(BF16), "c": w_branch_c.astype(BF16),
        "out": w_out.astype(BF16), "up": ffn_w_up, "down": ffn_w_down.astype(BF16),
    }
    pws = []
    for l in range(depth):
        pw = _layer_params(l, norm1_w, norm2_w, ssd_conv_w, ssd_conv_b, ssd_dt_bias, ssd_a_log, ssd_d,
                           ssd_norm_w, diff_q_norm, diff_k_norm, diff_lam, diff_subln_w, na_q_norm, na_k_norm,
                           ffn_conv_w, ffn_conv_b)
        pw["na_table"] = na_tables[l]
        pws.append(pw)

    ctx_row = lambda i, tm: 0
    lat_row = lambda i, tm: 1 + (i * tm) // dec_seq
    y_p = x_prompt.reshape(batch * seq, d)
    y_s = x_sample.reshape(dec_batch * dec_seq, d)
    h_p = _norm(y_p, pws[0]["norm1_w"], mod_all[0], ctx_row)
    h_s = _norm(y_s, pws[0]["norm1_w"], mod_all[0], lat_row)
    ctx_out = (depth, None, None, None, None, None)
    cache = (cdk, cdv, cnk, cnv, h0)
    for l in range(depth):
        next_norm = (pws[l + 1]["norm1_w"], mod_all[l + 1], 0, 1) if l + 1 < depth else None
        y_p, h_p, ctx_out = _trunk_layer(y_p, h_p, batch, seq, mod_all[l], ctx_row, pws[l], l, ctx_out, None,
                                         consts, next_norm)
        y_s, h_s, _ = _trunk_layer(y_s, h_s, dec_batch, dec_seq, mod_all[l], lat_row, pws[l], l, None, cache,
                                   consts, next_norm)
    _, dk, dv, nk, nv, st = ctx_out
    return (y_p.reshape(batch, seq, d), y_s.reshape(dec_batch, dec_seq, d),
            dk.reshape(batch, depth, seq, DIFF_HEADS, 2, DIFF_HEAD_DIM),
            dv.reshape(batch, depth, seq, DIFF_HEADS, 2 * DIFF_HEAD_DIM),
            nk.reshape(batch, depth, seq, NA_HEADS, NA_HEAD_DIM),
            nv.reshape(batch, depth, seq, NA_HEADS, NA_HEAD_DIM),
            st.reshape(batch, depth, 2, SSD_HEADS, SSD_HEAD_DIM, SSD_STATE))
```

```python
import functools
import math

import numpy as np
import jax
import jax.numpy as jnp
from jax import lax
from jax.experimental import pallas as pl
from jax.experimental.pallas import tpu as pltpu

F32 = jnp.float32
BF16 = jnp.bfloat16

D_MODEL = 2048
GRID_W = 64
SSD_INNER = 1024
SSD_HEAD_DIM = 64
SSD_HEADS = 16
SSD_GROUPS = 2
SSD_STATE = 128
SSD_CHUNK = 128
SSD_BC = 2 * SSD_GROUPS * SSD_STATE
DIFF_HEADS = 8
DIFF_HEAD_DIM = 64
NA_HEADS = 16
NA_HEAD_DIM = 64
NA_KH = 8
NA_KW = 16
HEAD_COLS = 1024
D_FF = 5632
ROPE_BASE = 10000.0
EPS = 1e-6

LANES = 128
VMEM_LIMIT_BYTES = 56 * 1024 * 1024
SUB = 512
SEG_BLOCK = 256

COL_G = 0
COL_Z = 6144
COL_XS = 7168
COL_BC = 8192
COL_DT = 8704
P1_COLS = 9216
P1_CHUNK = 1536
N_GATE_CHUNKS = COL_Z // P1_CHUNK

TM_NORM = 1024
TM_PROJ = 512
TM_P1 = 1024
TM_MERGE = 512
TM_OUT = 512
TM_UP = 1024
TM_DOWN = 256


def _cparams(*sem):
    return pltpu.CompilerParams(dimension_semantics=sem, vmem_limit_bytes=VMEM_LIMIT_BYTES)


def _resident(shape, layer):
    return pl.BlockSpec((None,) + tuple(shape), lambda *_: (layer, 0, 0), pipeline_mode=pl.Buffered(1))


def _sigmoid(x):
    return 1.0 / (1.0 + jnp.exp(-x))


def _silu(x):
    return x * _sigmoid(x)


def _bdot(a, b):
    return jnp.dot(a, b, preferred_element_type=F32)


def _bdot_nt(a, b):
    return lax.dot_general(a, b, (((1,), (1,)), ((), ())), preferred_element_type=F32)


def _split3(x):
    p1 = x.astype(BF16)
    r1 = x - p1.astype(F32)
    p2 = r1.astype(BF16)
    p3 = (r1 - p2.astype(F32)).astype(BF16)
    return p1, p2, p3


def _sel_right(x, e):
    p1, p2, _ = _split3(x)
    return _bdot(p1, e) + _bdot(p2, e)


def _sel_left(t, x):
    p1, p2, p3 = _split3(x)
    return _bdot(t, p1) + _bdot(t, p2) + _bdot(t, p3)


def _seg64_rms(a, bd):
    ss = _bdot((a * a).astype(BF16), bd)
    return a * lax.rsqrt(ss * (1.0 / 64.0) + EPS)


def _modulated_norm(x, nw, shift, scale):
    ms = jnp.mean(x * x, axis=-1, keepdims=True)
    y = x * lax.rsqrt(ms + EPS) * nw
    return y * (1.0 + scale) + shift


def _ada_kernel(c_ref, w_ref, b_ref, o_ref):
    c = c_ref[...]
    s = _silu(c).astype(BF16)
    o_ref[0] = _bdot(s, w_ref[0].astype(BF16)) + b_ref[0]


def _ada(c_rows, w_ada, b_ada):
    depth, d, n = w_ada.shape
    tn = 1024
    return pl.pallas_call(
        _ada_kernel,
        grid=(depth, n // tn),
        in_specs=[
            pl.BlockSpec((8, d), lambda l, j: (0, 0)),
            pl.BlockSpec((1, d, tn), lambda l, j: (l, 0, j)),
            pl.BlockSpec((1, 1, tn), lambda l, j: (l, 0, j)),
        ],
        out_specs=pl.BlockSpec((1, 8, tn), lambda l, j: (l, 0, j)),
        out_shape=jax.ShapeDtypeStruct((depth, 8, n), F32),
        compiler_params=_cparams("arbitrary", "arbitrary"),
        name="ada_mod",
    )(c_rows, w_ada, b_ada.reshape(depth, 1, n))


def _norm_kernel(x_ref, nw_ref, mod_ref, h_ref):
    h_ref[...] = _modulated_norm(x_ref[...], nw_ref[...], mod_ref[0, 0:1, :], mod_ref[0, 1:2, :]).astype(BF16)


def _norm(x2d, nw, mod, mod_row):
    t, d = x2d.shape
    tm = TM_NORM
    return pl.pallas_call(
        _norm_kernel,
        grid=(t // tm,),
        in_specs=[
            pl.BlockSpec((tm, d), lambda i: (i, 0)),
            pl.BlockSpec((1, d), lambda i: (0, 0)),
            pl.BlockSpec((1, 6, d), lambda i: (mod_row(i, tm), 0, 0)),
        ],
        out_specs=pl.BlockSpec((tm, d), lambda i: (i, 0)),
        out_shape=jax.ShapeDtypeStruct((t, d), BF16),
        compiler_params=_cparams("arbitrary"),
        name="norm_mod",
    )(x2d, nw, mod)


def _proj1_kernel(h_ref, wg_ref, ws_ref, o_ref):
    c = pl.program_id(0)
    h = h_ref[...]

    def tiles(w_ref, epilogue):
        for s in range(P1_CHUNK // SUB):
            cols = slice(s * SUB, (s + 1) * SUB)
            o_ref[:, cols] = epilogue(_bdot(h, w_ref[:, cols]))

    @pl.when(c < N_GATE_CHUNKS)
    def _():
        tiles(wg_ref, _sigmoid)

    @pl.when(c >= N_GATE_CHUNKS)
    def _():
        tiles(ws_ref, lambda a: a)


def _proj1(h, w_gate, w_ssd, layer):
    t, d = h.shape
    tm = TM_P1
    return pl.pallas_call(
        _proj1_kernel,
        grid=(P1_COLS // P1_CHUNK, t // tm),
        in_specs=[
            pl.BlockSpec((tm, d), lambda c, i: (i, 0)),
            pl.BlockSpec((None, d, P1_CHUNK), lambda c, i: (layer, 0, jnp.minimum(c, N_GATE_CHUNKS - 1))),
            pl.BlockSpec((None, d, P1_CHUNK), lambda c, i: (layer, 0, jnp.maximum(c - N_GATE_CHUNKS, 0))),
        ],
        out_specs=pl.BlockSpec((tm, P1_CHUNK), lambda c, i: (i, c)),
        out_shape=jax.ShapeDtypeStruct((t, P1_COLS), F32),
        compiler_params=_cparams("arbitrary", "arbitrary"),
        name="proj_gates_ssd",
    )(h, w_gate, w_ssd)


def _qkv_kernel(*refs, n_alias):
    h_ref, w_ref, qnw_ref, knw_ref, bd_ref = refs[:5]
    q_ref, k_ref, v_ref = refs[5 + n_alias:]
    h = h_ref[...]
    bd = bd_ref[...]
    for sec, (o_ref, nw_ref) in enumerate(((q_ref, qnw_ref), (k_ref, knw_ref), (v_ref, None))):
        for s in range(HEAD_COLS // SUB):
            acc = _bdot(h, w_ref[:, sec * HEAD_COLS + s * SUB: sec * HEAD_COLS + (s + 1) * SUB])
            if nw_ref is not None:
                parts = [_seg64_rms(acc[:, c * SEG_BLOCK:(c + 1) * SEG_BLOCK], bd) for c in range(SUB // SEG_BLOCK)]
                acc = jnp.concatenate(parts, axis=1) * nw_ref[:, s * SUB:(s + 1) * SUB]
            o_ref[..., s * SUB:(s + 1) * SUB] = acc.reshape(o_ref.shape[:-1] + (SUB,)).astype(o_ref.dtype)


def _qkv(h, w, qnw, knw, bd, nb, seq, layer, kv_prev, name):
    t, d = h.shape
    tm = TM_PROJ
    in_specs = [
        pl.BlockSpec((tm, d), lambda i: (i, 0)),
        _resident((d, 3 * HEAD_COLS), layer),
        pl.BlockSpec((1, HEAD_COLS), lambda i: (0, 0)),
        pl.BlockSpec((1, HEAD_COLS), lambda i: (0, 0)),
        pl.BlockSpec((SEG_BLOCK, SEG_BLOCK), lambda i: (0, 0)),
    ]
    args = [h, w, qnw, knw, bd]
    q_spec = pl.BlockSpec((tm, HEAD_COLS), lambda i: (i, 0))
    q_shape = jax.ShapeDtypeStruct((t, HEAD_COLS), F32 if kv_prev is None else BF16)
    aliases = {}
    n_alias = 0
    if kv_prev is None:
        kv_spec, kv_shape = q_spec, q_shape
    else:
        depth, k_prev, v_prev = kv_prev
        assert tm % seq == 0
        kv_spec = pl.BlockSpec((tm // seq, None, seq, HEAD_COLS), lambda i: (i, layer, 0, 0))
        kv_shape = jax.ShapeDtypeStruct((nb, depth, seq, HEAD_COLS), F32)
        if k_prev is not None:
            in_specs += [pl.BlockSpec(memory_space=pl.ANY)] * 2
            args += [k_prev, v_prev]
            aliases = {5: 1, 6: 2}
            n_alias = 2
    return pl.pallas_call(
        functools.partial(_qkv_kernel, n_alias=n_alias),
        grid=(t // tm,),
        in_specs=in_specs,
        out_specs=[q_spec, kv_spec, kv_spec],
        out_shape=[q_shape, kv_shape, kv_shape],
        input_output_aliases=aliases,
        compiler_params=_cparams("arbitrary"),
        name=name,
    )(*args)


def _ssd_kernel(*refs, seq, has_h0, has_state_out, n_alias):
    (z_ref, xs_ref, bc_ref, dt_ref, cwx_ref, cbx_ref, cwb_ref, cbb_ref, dtb_ref, alog_ref,
     dexp_ref, nw_ref, e_ref) = refs[:13]
    rest = list(refs[13:])
    h0_ref = rest.pop(0) if has_h0 else None
    rest = rest[n_alias:]
    y_ref = rest.pop(0)
    st_ref = rest.pop(0) if has_state_out else None
    xs_s, bc_s, ya_s, st_s = rest

    q = SSD_CHUNK
    nc = seq // q
    half = SSD_INNER // SSD_GROUPS

    def conv_chunk(c, carry):
        r0 = pl.multiple_of(c * q, q)
        for src, dst, cw, cb in ((xs_ref, xs_s, cwx_ref, cbx_ref), (bc_ref, bc_s, cwb_ref, cbb_ref)):
            x = src[pl.ds(r0, q), :]
            xp = src[pl.ds(jnp.maximum(r0 - 1, 0), 1), :]
            xn = src[pl.ds(jnp.minimum(r0 + q, seq - 1), 1), :]
            xp = jnp.where(c == 0, 0.0, xp)
            xn = jnp.where(c == nc - 1, 0.0, xn)
            rows = lax.broadcasted_iota(jnp.int32, x.shape, 0)
            prev = jnp.where(rows == 0, xp, pltpu.roll(x, 1, 0))
            nxt = jnp.where(rows == q - 1, xn, pltpu.roll(x, q - 1, 0))
            y = cw[0:1, :] * prev + cw[1:2, :] * x + cw[2:3, :] * nxt + cb[...]
            dst[pl.ds(r0, q), :] = _silu(y)
        return carry

    lax.fori_loop(0, nc, conv_chunk, 0)

    a_vec = -jnp.exp(alog_ref[...])
    dt_bias = dtb_ref[...]
    ri = lax.broadcasted_iota(jnp.int32, (q, q), 0)
    ci = lax.broadcasted_iota(jnp.int32, (q, q), 1)

    for d in (0, 1):
        tri = (ri >= ci) if d == 0 else (ci >= ri)
        tri_b = jnp.where(tri, 1.0, 0.0).astype(BF16)
        for g in range(SSD_GROUPS):
            if has_h0:
                st_s[g] = h0_ref[d, g * half:(g + 1) * half, :].T
            else:
                st_s[g] = jnp.zeros((SSD_STATE, half), F32)

        def chunk(i, carry, d=d, tri=tri, tri_b=tri_b):
            c = i if d == 0 else nc - 1 - i
            r0 = pl.multiple_of(c * q, q)
            x_dt = dt_ref[pl.ds(r0, q), :] + dt_bias
            dtc = jnp.maximum(x_dt, 0.0) + jnp.log1p(jnp.exp(-jnp.abs(x_dt)))
            la = dtc * a_vec
            cum = _sel_left(tri_b, la)
            cum_t = cum.T
            e = e_ref[d]
            dt_x = _sel_right(dtc, e)
            cum_x = _sel_right(cum, e)
            last = q - 1 if d == 0 else 0
            cl = cum_x[last:last + 1, :]
            xd = xs_s[pl.ds(r0, q), :] * dt_x
            xdb = xd.astype(BF16)
            xdd = (xd * jnp.exp(cl - cum_x)).astype(BF16)
            ecum = jnp.exp(cum_x)
            bcv = bc_s[pl.ds(r0, q), :]
            for g in range(SSD_GROUPS):
                bg = bcv[:, g * SSD_STATE:(g + 1) * SSD_STATE]
                cg = bcv[:, (SSD_GROUPS + g) * SSD_STATE:(SSD_GROUPS + g + 1) * SSD_STATE]
                bgb = bg.astype(BF16)
                cgb = cg.astype(BF16)
                gm = _bdot_nt(cgb, bgb)
                st = st_s[g]
                y_off = _bdot(cgb, st.astype(BF16)) * ecum[:, g * half:(g + 1) * half]
                ys = []
                for hh in range(SSD_HEADS // SSD_GROUPS):
                    h = g * (SSD_HEADS // SSD_GROUPS) + hh
                    k = d * SSD_HEADS + h
                    decay = jnp.where(tri, jnp.exp(cum[:, k:k + 1] - cum_t[k:k + 1, :]), 0.0)
                    m = (gm * decay).astype(BF16)
                    ys.append(_bdot(m, xdb[:, h * SSD_HEAD_DIM:(h + 1) * SSD_HEAD_DIM]))
                yg = jnp.concatenate(ys, axis=1) + y_off
                if d == 0:
                    ya_s[pl.ds(r0, q), g * half:(g + 1) * half] = yg
                else:
                    ya_s[pl.ds(r0, q), g * half:(g + 1) * half] += yg
                st_s[g] = (st * jnp.exp(cl[:, g * half:(g + 1) * half])
                           + _bdot(bg.T.astype(BF16), xdd[:, g * half:(g + 1) * half]))
            return carry

        lax.fori_loop(0, nc, chunk, 0)
        if has_state_out:
            for g in range(SSD_GROUPS):
                st_ref[d, g * half:(g + 1) * half, :] = st_s[g].T

    def fin_chunk(c, carry):
        r0 = pl.multiple_of(c * q, q)
        y = ya_s[pl.ds(r0, q), :] + dexp_ref[...] * xs_s[pl.ds(r0, q), :]
        y = y * _silu(z_ref[pl.ds(r0, q), :])
        for g in range(SSD_GROUPS):
            v = y[:, g * half:(g + 1) * half]
            ms = jnp.mean(v * v, axis=-1, keepdims=True)
            out = v * lax.rsqrt(ms + EPS) * nw_ref[:, g * half:(g + 1) * half]
            y_ref[pl.ds(r0, q), g * half:(g + 1) * half] = out.astype(BF16)
        return carry

    lax.fori_loop(0, nc, fin_chunk, 0)


def _ssd(u1, nb, seq, pw, h0, layer, state_prev):
    has_h0 = h0 is not None
    has_state_out = state_prev is not None
    c1 = lambda b: (0, 0)
    in_specs = [
        pl.BlockSpec((seq, SSD_INNER), lambda b: (b, COL_Z // SSD_INNER)),
        pl.BlockSpec((seq, SSD_INNER), lambda b: (b, COL_XS // SSD_INNER)),
        pl.BlockSpec((seq, SSD_BC), lambda b: (b, COL_BC // SSD_BC)),
        pl.BlockSpec((seq, LANES), lambda b: (b, COL_DT // LANES)),
        pl.BlockSpec((3, SSD_INNER), c1),
        pl.BlockSpec((1, SSD_INNER), c1),
        pl.BlockSpec((3, SSD_BC), c1),
        pl.BlockSpec((1, SSD_BC), c1),
        pl.BlockSpec((1, LANES), c1),
        pl.BlockSpec((1, LANES), c1),
        pl.BlockSpec((1, SSD_INNER), c1),
        pl.BlockSpec((1, SSD_INNER), c1),
        pl.BlockSpec((2, LANES, SSD_INNER), lambda b: (0, 0, 0)),
    ]
    args = [u1, u1, u1, u1, pw["cw_xs"], pw["cb_xs"], pw["cw_bc"], pw["cb_bc"], pw["dt_bias"], pw["a_log"],
            pw["d_exp"], pw["ssd_nw"], pw["e_heads"]]
    if has_h0:
        in_specs.append(pl.BlockSpec((None, None, 2, SSD_INNER, SSD_STATE), lambda b: (b, layer, 0, 0, 0)))
        args.append(h0)
    out_specs = [pl.BlockSpec((seq, SSD_INNER), lambda b: (b, 0))]
    out_shape = [jax.ShapeDtypeStruct((nb * seq, SSD_INNER), BF16)]
    aliases = {}
    n_alias = 0
    if has_state_out:
        depth, prev = state_prev
        out_specs.append(pl.BlockSpec((None, None, 2, SSD_INNER, SSD_STATE), lambda b: (b, layer, 0, 0, 0)))
        out_shape.append(jax.ShapeDtypeStruct((nb, depth, 2, SSD_INNER, SSD_STATE), F32))
        if prev is not None:
            aliases = {len(args): 1}
            in_specs.append(pl.BlockSpec(memory_space=pl.ANY))
            args.append(prev)
            n_alias = 1
    res = pl.pallas_call(
        functools.partial(_ssd_kernel, seq=seq, has_h0=has_h0, has_state_out=has_state_out, n_alias=n_alias),
        grid=(nb,),
        in_specs=in_specs,
        out_specs=out_specs,
        out_shape=out_shape,
        input_output_aliases=aliases,
        scratch_shapes=[
            pltpu.VMEM((seq, SSD_INNER), F32),
            pltpu.VMEM((seq, SSD_BC), F32),
            pltpu.VMEM((seq, SSD_INNER), F32),
            pltpu.VMEM((SSD_GROUPS, SSD_STATE, SSD_INNER // SSD_GROUPS), F32),
        ],
        compiler_params=_cparams("arbitrary"),
        name="ssd_bidir",
    )(*args)
    return res if has_state_out else (res[0], None)


def _softmax_rows(parts, mxu_sums=False):
    m = functools.reduce(jnp.maximum, [jnp.max(s, axis=-1, keepdims=True) for s in parts])
    es = [jnp.exp(s - m) for s in parts]
    if mxu_sums:
        dens = [_bdot(e.astype(BF16), jnp.ones((e.shape[1], SEG_BLOCK), BF16)) for e in es]
        inv = 1.0 / functools.reduce(lambda a, b: a + b, dens)
        return [e * jnp.tile(inv, (1, e.shape[1] // SEG_BLOCK)) for e in es]
    den = functools.reduce(lambda a, b: a + b, [jnp.sum(e, axis=-1, keepdims=True) for e in es])
    inv = 1.0 / den
    return [e * inv for e in es]


def _exp_rows(parts):
    m = functools.reduce(jnp.maximum, [jnp.max(s, axis=-1, keepdims=True) for s in parts])
    return [jnp.exp(s - m).astype(BF16) for s in parts]


def _lane_halves(x):
    left = lax.broadcasted_iota(jnp.int32, x.shape, 1) < LANES // 2
    return jnp.where(left, x, 0.0).astype(BF16), jnp.where(left, 0.0, x).astype(BF16)


def _lambda(lp, lam_init):
    a = jnp.sum(lp[0:1, :] * lp[1:2, :], axis=-1, keepdims=True)
    b = jnp.sum(lp[2:3, :] * lp[3:4, :], axis=-1, keepdims=True)
    return jnp.exp(a) - jnp.exp(b) + lam_init


def _subln(o, w, lam_init):
    ms = jnp.mean(o * o, axis=-1, keepdims=True)
    return o * lax.rsqrt(ms + EPS) * w * (1.0 - lam_init)


def _rope(x, cos, sin):
    lane = lax.broadcasted_iota(jnp.int32, x.shape, 1)
    swapped = jnp.where((lane & 16) == 0, pltpu.roll(x, LANES - 16, 1), pltpu.roll(x, 16, 1))
    return x * cos + swapped * sin


def _ctx_diff_kernel(q_ref, k_ref, v_ref, lamp_ref, sw_ref, o_ref, *, lam_init):
    scale = DIFF_HEAD_DIM ** -0.5
    lam = _lambda(lamp_ref[...], lam_init)
    sw = sw_ref[...]
    for hb in range(DIFF_HEADS):
        ps = []
        for t in (0, 1):
            sl = slice(hb * LANES + t * DIFF_HEAD_DIM, hb * LANES + (t + 1) * DIFF_HEAD_DIM)
            s = _bdot_nt(q_ref[:, sl], k_ref[:, sl].astype(BF16)) * scale
            ps.append(_softmax_rows([s], mxu_sums=True)[0])
        att = (ps[0] - lam * ps[1]).astype(BF16)
        blk = slice(hb * LANES, (hb + 1) * LANES)
        o = _subln(_bdot(att, v_ref[:, blk].astype(BF16)), sw, lam_init)
        o_ref[:, blk] = o.astype(BF16)


def _ctx_softmax_kernel(q_ref, k_ref, v_ref, o_ref):
    scale = NA_HEAD_DIM ** -0.5
    ones_l, ones_r = _lane_halves(jnp.ones((k_ref.shape[0], LANES), F32))
    for hb in range(NA_HEADS // 2):
        blk = slice(hb * LANES, (hb + 1) * LANES)
        v_l, v_r = _lane_halves(v_ref[:, blk])
        es = []
        for t in (0, 1):
            sl = slice(hb * LANES + t * NA_HEAD_DIM, hb * LANES + (t + 1) * NA_HEAD_DIM)
            s = _bdot_nt(q_ref[:, sl], k_ref[:, sl].astype(BF16)) * scale
            es.append(_exp_rows([s])[0])
        num = _bdot(es[0], v_l) + _bdot(es[1], v_r)
        den = _bdot(es[0], ones_l) + _bdot(es[1], ones_r)
        o_ref[:, blk] = (num / den).astype(BF16)


def _ctx_attn(q, k_all, v_all, nb, seq, layer, extra, kern, name):
    kv_spec = pl.BlockSpec((None, None, seq, HEAD_COLS), lambda b: (b, layer, 0, 0))
    extra_specs = [pl.BlockSpec(a.shape, lambda b: (0, 0)) for a in extra]
    return pl.pallas_call(
        kern,
        grid=(nb,),
        in_specs=[pl.BlockSpec((seq, HEAD_COLS), lambda b: (b, 0)), kv_spec, kv_spec] + extra_specs,
        out_specs=pl.BlockSpec((seq, HEAD_COLS), lambda b: (b, 0)),
        out_shape=jax.ShapeDtypeStruct((nb * seq, HEAD_COLS), BF16),
        compiler_params=_cparams("arbitrary"),
        name=name,
    )(q, k_all, v_all, *extra)


LAT_DIFF_TQ = 256


def _lat_diff_kernel(q_ref, k_ref, v_ref, ck_ref, cv_ref, cos_ref, sin_ref, lamp_ref, sw_ref, o_ref, *, lam_init):
    scale = DIFF_HEAD_DIM ** -0.5
    k = _rope(k_ref[...], cos_ref[...], sin_ref[...])
    k_t = [k[:, t * DIFF_HEAD_DIM:(t + 1) * DIFF_HEAD_DIM].astype(BF16) for t in (0, 1)]
    ck_t = [ck_ref[:, t * DIFF_HEAD_DIM:(t + 1) * DIFF_HEAD_DIM].astype(BF16) for t in (0, 1)]
    v = v_ref[...].astype(BF16)
    cv = cv_ref[...].astype(BF16)
    lam = _lambda(lamp_ref[...], lam_init)
    sw = sw_ref[...]
    for qb in range(q_ref.shape[0] // LAT_DIFF_TQ):
        rs = slice(qb * LAT_DIFF_TQ, (qb + 1) * LAT_DIFF_TQ)
        q = _rope(q_ref[rs, :], cos_ref[rs, :], sin_ref[rs, :])
        pl_, pc_ = [], []
        for t in (0, 1):
            qt = q[:, t * DIFF_HEAD_DIM:(t + 1) * DIFF_HEAD_DIM].astype(BF16)
            p_loc, p_ctx = _softmax_rows([_bdot_nt(qt, k_t[t]) * scale, _bdot_nt(qt, ck_t[t]) * scale])
            pl_.append(p_loc)
            pc_.append(p_ctx)
        a_loc = (pl_[0] - lam * pl_[1]).astype(BF16)
        a_ctx = (pc_[0] - lam * pc_[1]).astype(BF16)
        o = _bdot(a_loc, v) + _bdot(a_ctx, cv)
        o_ref[rs, :] = _subln(o, sw, lam_init).astype(BF16)


def _lat_diff(q, k, v, nb, seq, cache_k, cache_v, layer, cos, sin, lamp, sw, lam_init):
    past = cache_k.shape[2]
    cache_spec = pl.BlockSpec((None, None, past, LANES), lambda b, h: (b, layer, 0, h))
    loc_spec = pl.BlockSpec((seq, LANES), lambda b, h: (b, h))
    tab_spec = pl.BlockSpec((seq, LANES), lambda b, h: (0, 0))
    return pl.pallas_call(
        functools.partial(_lat_diff_kernel, lam_init=lam_init),
        grid=(nb, DIFF_HEADS),
        in_specs=[
            loc_spec, loc_spec, loc_spec, cache_spec, cache_spec, tab_spec, tab_spec,
            pl.BlockSpec((4, DIFF_HEAD_DIM), lambda b, h: (0, 0)),
            pl.BlockSpec((1, LANES), lambda b, h: (0, 0)),
        ],
        out_specs=loc_spec,
        out_shape=jax.ShapeDtypeStruct((nb * seq, HEAD_COLS), BF16),
        compiler_params=_cparams("arbitrary", "arbitrary"),
        name="lat_diff_attn",
    )(q, k, v, cache_k, cache_v, cos, sin, lamp, sw)


NA_Q_ROWS = 4


def _lat_na_kernel(q_ref, k_ref, v_ref, ck_ref, cv_ref, tb_ref, o_ref, bias_s, *, rows):
    scale = NA_HEAD_DIM ** -0.5
    kh = min(NA_KH, rows)
    win0 = lambda qr: min(max(qr - kh // 2, 0), rows - kh)

    @pl.when(pl.program_id(1) == 0)
    def _():
        bias_s[...] = jnp.full(bias_s.shape, -jnp.inf, F32)
        for t in (0, 1):
            for qr in range(rows):
                r0 = win0(qr)
                a0 = r0 - qr + NA_KH - 1
                bias_s[t, qr * GRID_W:(qr + 1) * GRID_W, r0 * GRID_W:(r0 + kh) * GRID_W] = (
                    tb_ref[t, :, a0 * GRID_W:(a0 + kh) * GRID_W])

    k_h = [k_ref[:, t * NA_HEAD_DIM:(t + 1) * NA_HEAD_DIM].astype(BF16) for t in (0, 1)]
    ck_h = [ck_ref[:, t * NA_HEAD_DIM:(t + 1) * NA_HEAD_DIM].astype(BF16) for t in (0, 1)]
    v_h = _lane_halves(v_ref[...])
    cv_h = _lane_halves(cv_ref[...])
    ones_h = lambda n: _lane_halves(jnp.ones((n, LANES), F32))
    cones_h = ones_h(cv_ref.shape[0])
    for qb in range(rows // NA_Q_ROWS):
        lo = win0(qb * NA_Q_ROWS) // 2 * 2
        hi = min(rows, (win0((qb + 1) * NA_Q_ROWS - 1) + kh + 1) // 2 * 2)
        rs = slice(qb * NA_Q_ROWS * GRID_W, (qb + 1) * NA_Q_ROWS * GRID_W)
        ks = slice(lo * GRID_W, hi * GRID_W)
        num = den = None
        for t in (0, 1):
            qh = q_ref[rs, t * NA_HEAD_DIM:(t + 1) * NA_HEAD_DIM].astype(BF16)
            s_loc = _bdot_nt(qh, k_h[t][ks]) * scale + bias_s[t, rs, ks]
            s_ctx = _bdot_nt(qh, ck_h[t]) * scale
            e_loc, e_ctx = _exp_rows([s_loc, s_ctx])
            n_t = _bdot(e_loc, v_h[t][ks]) + _bdot(e_ctx, cv_h[t])
            d_t = _bdot(e_loc, ones_h((hi - lo) * GRID_W)[t]) + _bdot(e_ctx, cones_h[t])
            num = n_t if num is None else num + n_t
            den = d_t if den is None else den + d_t
        o_ref[rs, :] = (num / den).astype(BF16)


def _lat_na(q, k, v, nb, seq, cache_k, cache_v, layer, table):
    past = cache_k.shape[2]
    rows = seq // GRID_W
    ncol = table.shape[-1]
    loc_spec = pl.BlockSpec((seq, LANES), lambda h, b: (b, h))
    cache_spec = pl.BlockSpec((None, None, past, LANES), lambda h, b: (b, layer, 0, h))
    return pl.pallas_call(
        functools.partial(_lat_na_kernel, rows=rows),
        grid=(NA_HEADS // 2, nb),
        in_specs=[loc_spec, loc_spec, loc_spec, cache_spec, cache_spec,
                  pl.BlockSpec((2, GRID_W, ncol), lambda h, b: (h, 0, 0))],
        out_specs=pl.BlockSpec((seq, LANES), lambda h, b: (b, h)),
        out_shape=jax.ShapeDtypeStruct((nb * seq, HEAD_COLS), BF16),
        scratch_shapes=[pltpu.VMEM((2, seq, seq), F32)],
        compiler_params=_cparams("arbitrary", "arbitrary"),
        name="lat_nbr_attn",
    )(q, k, v, cache_k, cache_v, table)


def _merge_kernel(ya_ref, yb_ref, yc_ref, wa_ref, wb_ref, wc_ref, ga_ref, gb_ref, gc_ref, o_ref):
    ya, yb, yc = ya_ref[...], yb_ref[...], yc_ref[...]
    for s in range(D_MODEL // SUB):
        cols = slice(s * SUB, (s + 1) * SUB)
        m = (ga_ref[:, cols] * _bdot(ya, wa_ref[:, cols])
             + gb_ref[:, cols] * _bdot(yb, wb_ref[:, cols])
             + gc_ref[:, cols] * _bdot(yc, wc_ref[:, cols]))
        o_ref[:, cols] = m.astype(BF16)


def _merge(ya, yb, yc, wa, wb, wc, u1, layer):
    t, kk = ya.shape
    n = wa.shape[2]
    tm = TM_MERGE
    ysp = pl.BlockSpec((tm, kk), lambda i: (i, 0))
    wsp = _resident((kk, n), layer)
    return pl.pallas_call(
        _merge_kernel,
        grid=(t // tm,),
        in_specs=[ysp, ysp, ysp, wsp, wsp, wsp,
                  pl.BlockSpec((tm, n), lambda i: (i, 0)),
                  pl.BlockSpec((tm, n), lambda i: (i, 1)),
                  pl.BlockSpec((tm, n), lambda i: (i, 2))],
        out_specs=pl.BlockSpec((tm, n), lambda i: (i, 0)),
        out_shape=jax.ShapeDtypeStruct((t, n), BF16),
        compiler_params=_cparams("arbitrary"),
        name="branch_merge",
    )(ya, yb, yc, wa, wb, wc, u1, u1, u1)


def _resid_mm_kernel(*refs, gate_row, norm_rows):
    if norm_rows is None:
        a_ref, w_ref, x_ref, mod_ref, xo_ref = refs
    else:
        a_ref, w_ref, x_ref, mod_ref, nw_ref, nmod_ref, xo_ref, h_ref = refs
    a = a_ref[...]
    for s in range(D_MODEL // SUB):
        cols = slice(s * SUB, (s + 1) * SUB)
        xo_ref[:, cols] = x_ref[:, cols] + mod_ref[0, gate_row:gate_row + 1, cols] * _bdot(a, w_ref[:, cols])
    if norm_rows is not None:
        shift_row, scale_row = norm_rows
        h = _modulated_norm(xo_ref[...], nw_ref[...], nmod_ref[0, shift_row:shift_row + 1, :],
                            nmod_ref[0, scale_row:scale_row + 1, :])
        h_ref[...] = h.astype(BF16)


TM_MERGE_OUT = 256


def _merge_out_kernel(ya_ref, yb_ref, yc_ref, wa_ref, wb_ref, wc_ref, ga_ref, gb_ref, gc_ref, wo_ref, x_ref, mod_ref,
                      nw_ref, xo_ref, h_ref, m_s):
    ya, yb, yc = ya_ref[...], yb_ref[...], yc_ref[...]
    for s in range(D_MODEL // SUB):
        cols = slice(s * SUB, (s + 1) * SUB)
        m = (ga_ref[:, cols] * _bdot(ya, wa_ref[:, cols])
             + gb_ref[:, cols] * _bdot(yb, wb_ref[:, cols])
             + gc_ref[:, cols] * _bdot(yc, wc_ref[:, cols]))
        m_s[:, cols] = m.astype(BF16)
    a = m_s[...]
    for s in range(D_MODEL // SUB):
        cols = slice(s * SUB, (s + 1) * SUB)
        xo_ref[:, cols] = x_ref[:, cols] + mod_ref[0, 2:3, cols] * _bdot(a, wo_ref[:, cols])
    h = _modulated_norm(xo_ref[...], nw_ref[...], mod_ref[0, 3:4, :], mod_ref[0, 4:5, :])
    h_ref[...] = h.astype(BF16)


def _merge_out(ya, yb, yc, wts, u1, layer, x2d, mod, mod_row, nw):
    t, kk = ya.shape
    d = x2d.shape[1]
    tm = TM_MERGE_OUT
    row = lambda i: (i, 0)
    ysp = pl.BlockSpec((tm, kk), row)
    wsp = _resident((kk, d), layer)
    return pl.pallas_call(
        _merge_out_kernel,
        grid=(t // tm,),
        in_specs=[ysp, ysp, ysp, wsp, wsp, wsp,
                  pl.BlockSpec((tm, d), lambda i: (i, 0)),
                  pl.BlockSpec((tm, d), lambda i: (i, 1)),
                  pl.BlockSpec((tm, d), lambda i: (i, 2)),
                  _resident((d, d), layer), pl.BlockSpec((tm, d), row),
                  pl.BlockSpec((1, 6, d), lambda i: (mod_row(i, tm), 0, 0)),
                  pl.BlockSpec((1, d), lambda i: (0, 0))],
        out_specs=[pl.BlockSpec((tm, d), row), pl.BlockSpec((tm, d), row)],
        out_shape=[jax.ShapeDtypeStruct((t, d), F32), jax.ShapeDtypeStruct((t, d), BF16)],
        scratch_shapes=[pltpu.VMEM((tm, d), BF16)],
        compiler_params=_cparams("arbitrary"),
        name="merge_out_proj",
    )(ya, yb, yc, wts["a"], wts["b"], wts["c"], u1, u1, u1, wts["out"], x2d, mod, nw)


def _resid_mm(a, w, layer, x2d, mod, gate_row, tm, mod_row, norm, name):
    t, kk = a.shape
    d = w.shape[2]
    row = lambda i: (i, 0)
    mrow = lambda i: (mod_row(i, tm), 0, 0)
    in_specs = [pl.BlockSpec((tm, kk), row), _resident((kk, d), layer), pl.BlockSpec((tm, d), row),
                pl.BlockSpec((1, 6, d), mrow)]
    args = [a, w, x2d, mod]
    out_specs = [pl.BlockSpec((tm, d), row)]
    out_shape = [jax.ShapeDtypeStruct((t, d), F32)]
    norm_rows = None
    if norm is not None:
        nw, nmod, shift_row, scale_row = norm
        norm_rows = (shift_row, scale_row)
        in_specs += [pl.BlockSpec((1, d), lambda i: (0, 0)), pl.BlockSpec((1, 6, d), mrow)]
        args += [nw, nmod]
        out_specs.append(pl.BlockSpec((tm, d), row))
        out_shape.append(jax.ShapeDtypeStruct((t, d), BF16))
    res = pl.pallas_call(
        functools.partial(_resid_mm_kernel, gate_row=gate_row, norm_rows=norm_rows),
        grid=(t // tm,),
        in_specs=in_specs,
        out_specs=out_specs,
        out_shape=out_shape,
        compiler_params=_cparams("arbitrary"),
        name=name,
    )(*args)
    return (res[0], res[1]) if norm is not None else (res[0], None)


def _ffn_up_kernel(h_ref, wv_ref, wg_ref, cwv_ref, cwg_ref, cbv_ref, cbg_ref, o_ref, wv_s, wg_s, *, seq):
    @pl.when(pl.program_id(1) == 0)
    def _():
        wv_s[...] = wv_ref[...].astype(BF16)
        wg_s[...] = wg_ref[...].astype(BF16)

    h = h_ref[...]
    tm = h.shape[0]
    pos = lax.broadcasted_iota(jnp.int32, (tm, SUB), 0) % seq

    def conv(acc, cw, cb):
        prev = jnp.where(pos == 0, 0.0, pltpu.roll(acc, 1, 0))
        nxt = jnp.where(pos == seq - 1, 0.0, pltpu.roll(acc, tm - 1, 0))
        return cw[0:1, :] * prev + cw[1:2, :] * acc + cw[2:3, :] * nxt + cb[...]

    val = conv(_bdot(h, wv_s[...]), cwv_ref, cbv_ref)
    gt = conv(_bdot(h, wg_s[...]), cwg_ref, cbg_ref)
    o_ref[...] = (_silu(gt) * val).astype(BF16)


def _ffn_up(h, w_up, layer, cw, cb, seq):
    t, d = h.shape
    tm = max(TM_UP, seq)
    assert tm % seq == 0
    nt = D_FF // SUB
    return pl.pallas_call(
        functools.partial(_ffn_up_kernel, seq=seq),
        grid=(nt, t // tm),
        in_specs=[
            pl.BlockSpec((tm, d), lambda j, i: (i, 0)),
            pl.BlockSpec((None, d, SUB), lambda j, i: (layer, 0, j)),
            pl.BlockSpec((None, d, SUB), lambda j, i: (layer, 0, nt + j)),
            pl.BlockSpec((3, SUB), lambda j, i: (0, j)),
            pl.BlockSpec((3, SUB), lambda j, i: (0, nt + j)),
            pl.BlockSpec((1, SUB), lambda j, i: (0, j)),
            pl.BlockSpec((1, SUB), lambda j, i: (0, nt + j)),
        ],
        out_specs=pl.BlockSpec((tm, SUB), lambda j, i: (i, j)),
        out_shape=jax.ShapeDtypeStruct((t, D_FF), BF16),
        scratch_shapes=[pltpu.VMEM((d, SUB), BF16), pltpu.VMEM((d, SUB), BF16)],
        compiler_params=_cparams("arbitrary", "arbitrary"),
        name="ffn_up_conv",
    )(h, w_up, w_up, cw, cw, cb, cb)


def _rope_tables(seq):
    t = np.arange(seq)
    row = (t // GRID_W).astype(np.float32)
    col = (t % GRID_W).astype(np.float32)
    n_freq = DIFF_HEAD_DIM // 4
    inv = jnp.asarray(ROPE_BASE, F32) ** (-jnp.arange(n_freq, dtype=F32) / n_freq)
    ang_r = jnp.asarray(row)[:, None] * inv
    ang_c = jnp.asarray(col)[:, None] * inv
    cr, sr, cc, sc = jnp.cos(ang_r), jnp.sin(ang_r), jnp.cos(ang_c), jnp.sin(ang_c)
    cos64 = jnp.concatenate([cr, cr, cc, cc], axis=-1)
    sin64 = jnp.concatenate([-sr, sr, -sc, sc], axis=-1)
    return jnp.tile(cos64, (1, 2)), jnp.tile(sin64, (1, 2))


def _na_bias_table(rpb):
    qc = np.arange(GRID_W)[:, None]
    kc = np.arange(GRID_W)[None, :]
    onehot = (kc - qc + NA_KW - 1 == np.arange(2 * NA_KW - 1)[:, None, None]).astype(np.float32)
    wstart = np.clip(qc - NA_KW // 2, 0, GRID_W - NA_KW)
    valid = (kc >= wstart) & (kc < wstart + NA_KW)
    toe = jnp.einsum("...ab,bqk->...qak", rpb.astype(F32), jnp.asarray(onehot), precision=lax.Precision.HIGHEST)
    tbl = jnp.where(valid[:, None, :], toe, -jnp.inf)
    return tbl.reshape(tbl.shape[:-2] + (-1,))


def _head_expand():
    e = np.zeros((2, LANES, SSD_INNER), np.float32)
    for d in range(2):
        for h in range(SSD_HEADS):
            e[d, d * SSD_HEADS + h, h * SSD_HEAD_DIM:(h + 1) * SSD_HEAD_DIM] = 1.0
    return jnp.asarray(e, BF16)


def _seg_ones():
    i = np.arange(SEG_BLOCK)
    return jnp.asarray((i[:, None] // 64 == i[None, :] // 64).astype(np.float32), BF16)


def _pad_lanes(v, n):
    return jnp.pad(v.reshape(1, -1), ((0, 0), (0, n - v.size)))


def _split_w_in(w_in):
    o = np.cumsum([0, 1024 + 1536 + 32, 3 * HEAD_COLS, 3 * HEAD_COLS, 6144])
    ssd, wd, wn, g = [w_in[..., o[i]:o[i + 1]].astype(BF16) for i in range(4)]
    ssd = jnp.pad(ssd, ((0, 0), (0, 0), (0, P1_COLS - COL_Z - ssd.shape[-1])))
    return g, ssd, wd, wn


def _layer_params(l, norm1_w, norm2_w, ssd_conv_w, ssd_conv_b, ssd_dt_bias, ssd_a_log, ssd_d, ssd_norm_w,
                  diff_q_norm, diff_k_norm, diff_lam, diff_subln_w, na_q_norm, na_k_norm, ffn_conv_w, ffn_conv_b):
    rep = HEAD_COLS // DIFF_HEAD_DIM
    return {
        "norm1_w": norm1_w[l].reshape(1, -1), "norm2_w": norm2_w[l].reshape(1, -1),
        "diff_qn": jnp.tile(diff_q_norm[l], rep).reshape(1, -1), "diff_kn": jnp.tile(diff_k_norm[l], rep).reshape(1, -1),
        "na_qn": jnp.tile(na_q_norm[l], rep).reshape(1, -1), "na_kn": jnp.tile(na_k_norm[l], rep).reshape(1, -1),
        "cw_xs": ssd_conv_w[l][:, :SSD_INNER], "cw_bc": ssd_conv_w[l][:, SSD_INNER:],
        "cb_xs": ssd_conv_b[l][:SSD_INNER].reshape(1, -1), "cb_bc": ssd_conv_b[l][SSD_INNER:].reshape(1, -1),
        "dt_bias": _pad_lanes(ssd_dt_bias[l], LANES), "a_log": _pad_lanes(ssd_a_log[l], LANES),
        "d_exp": jnp.repeat(ssd_d[l], SSD_HEAD_DIM).reshape(1, -1), "ssd_nw": ssd_norm_w[l].reshape(1, -1),
        "e_heads": _head_expand(),
        "diff_lam": diff_lam[l], "subln_w": diff_subln_w[l].reshape(1, -1),
        "ffn_cw": ffn_conv_w[l], "ffn_cb": ffn_conv_b[l].reshape(1, -1),
    }


def _trunk_layer(x2d, h1, nb, seq, mod, mod_row, pw, layer, ctx_out, cache, consts, next_norm):
    wts = consts["weights"]
    u1 = _proj1(h1, wts["gate"], wts["ssd"], layer)
    lam_init = 0.8 - 0.6 * math.exp(-0.3 * layer)
    bd = consts["bd"]
    if cache is None:
        depth, dk, dv, nk, nv, st = ctx_out
        qd, dk, dv = _qkv(h1, wts["diff"], pw["diff_qn"], pw["diff_kn"], bd, nb, seq, layer, (depth, dk, dv),
                          "proj_diff_qkv")
        qn, nk, nv = _qkv(h1, wts["na"], pw["na_qn"], pw["na_kn"], bd, nb, seq, layer, (depth, nk, nv),
                          "proj_na_qkv")
        y_a, st = _ssd(u1, nb, seq, pw, None, layer, (depth, st))
        y_b = _ctx_attn(qd, dk, dv, nb, seq, layer, [pw["diff_lam"], pw["subln_w"]],
                        functools.partial(_ctx_diff_kernel, lam_init=lam_init), "ctx_diff_attn")
        y_c = _ctx_attn(qn, nk, nv, nb, seq, layer, [], _ctx_softmax_kernel, "ctx_softmax_attn")
        ctx_out = (depth, dk, dv, nk, nv, st)
    else:
        cdk, cdv, cnk, cnv, h0 = cache
        qd, kd, vd = _qkv(h1, wts["diff"], pw["diff_qn"], pw["diff_kn"], bd, nb, seq, layer, None, "proj_diff_qkv")
        qn, kn, vn = _qkv(h1, wts["na"], pw["na_qn"], pw["na_kn"], bd, nb, seq, layer, None, "proj_na_qkv")
        y_a, _ = _ssd(u1, nb, seq, pw, h0, layer, None)
        y_b = _lat_diff(qd, kd, vd, nb, seq, cdk, cdv, layer, consts["cos"], consts["sin"], pw["diff_lam"],
                        pw["subln_w"], lam_init)
        y_c = _lat_na(qn, kn, vn, nb, seq, cnk, cnv, layer, pw["na_table"])
    x2d, h2 = _merge_out(y_a, y_b, y_c, wts, u1, layer, x2d, mod, mod_row, pw["norm2_w"])
    act = _ffn_up(h2, wts["up"], layer, pw["ffn_cw"], pw["ffn_cb"], seq)
    x2d, h1_next = _resid_mm(act, wts["down"], layer, x2d, mod, 5, TM_DOWN, mod_row, next_norm, "ffn_down")
    return x2d, h1_next, ctx_out


def kernel(x_prompt, x_sample, c, cache_diff_k, cache_diff_v, cache_na_k, cache_na_v, state_ssm, c_ctx, norm1_w, norm2_w, w_ada, b_ada, w_in, ssd_conv_w, ssd_conv_b, ssd_dt_bias, ssd_a_log, ssd_d, ssd_norm_w, diff_q_norm, diff_k_norm, diff_lam, diff_subln_w, na_q_norm, na_k_norm, na_rpb, w_branch_a, w_branch_b, w_branch_c, w_out, ffn_w_up, ffn_conv_w, ffn_conv_b, ffn_w_down):
    batch, seq, d = x_prompt.shape
    dec_batch, dec_seq, _ = x_sample.shape
    depth = w_in.shape[0]
    past = cache_diff_k.shape[2]
    assert d == D_MODEL and dec_batch + 1 <= 8 and seq % SSD_CHUNK == 0 and dec_seq % (NA_KH * GRID_W) == 0

    c_rows = jnp.concatenate([c_ctx.reshape(1, d), c, jnp.zeros((8 - 1 - dec_batch, d), F32)], axis=0)
    mod_all = _ada(c_rows, w_ada, b_ada).reshape(depth, 8, 6, d)

    cos, sin = _rope_tables(dec_seq)
    consts = {"bd": _seg_ones(), "cos": cos, "sin": sin}
    na_tables = _na_bias_table(na_rpb)
    cdk = cache_diff_k.reshape(dec_batch, depth, past, HEAD_COLS)
    cdv = cache_diff_v.reshape(dec_batch, depth, past, HEAD_COLS)
    cnk = cache_na_k.reshape(dec_batch, depth, past, HEAD_COLS)
    cnv = cache_na_v.reshape(dec_batch, depth, past, HEAD_COLS)
    h0 = state_ssm.reshape(dec_batch, depth, 2, SSD_INNER, SSD_STATE)

    w_gate, w_ssd, w_diff, w_na = _split_w_in(w_in)
    consts["weights"] = {
        "gate": w_gate, "ssd": w_ssd, "diff": w_diff, "na": w_na,
        "a": w_branch_a.astype(BF16), "b": w_branch_b.astype(BF16), "c": w_branch_c.astype(BF16),
        "out": w_out.astype(BF16), "up": ffn_w_up, "down": ffn_w_down.astype(BF16),
    }
    pws = []
    for l in range(depth):
        pw = _layer_params(l, norm1_w, norm2_w, ssd_conv_w, ssd_conv_b, ssd_dt_bias, ssd_a_log, ssd_d,
                           ssd_norm_w, diff_q_norm, diff_k_norm, diff_lam, diff_subln_w, na_q_norm, na_k_norm,
                           ffn_conv_w, ffn_conv_b)
        pw["na_table"] = na_tables[l]
        pws.append(pw)

    ctx_row = lambda i, tm: 0
    lat_row = lambda i, tm: 1 + (i * tm) // dec_seq
    y_p = x_prompt.reshape(batch * seq, d)
    y_s = x_sample.reshape(dec_batch * dec_seq, d)
    h_p = _norm(y_p, pws[0]["norm1_w"], mod_all[0], ctx_row)
    h_s = _norm(y_s, pws[0]["norm1_w"], mod_all[0], lat_row)
    ctx_out = (depth, None, None, None, None, None)
    cache = (cdk, cdv, cnk, cnv, h0)
    for l in range(depth):
        next_norm = (pws[l + 1]["norm1_w"], mod_all[l + 1], 0, 1) if l + 1 < depth else None
        y_p, h_p, ctx_out = _trunk_layer(y_p, h_p, batch, seq, mod_all[l], ctx_row, pws[l], l, ctx_out, None,
                                         consts, next_norm)
        y_s, h_s, _ = _trunk_layer(y_s, h_s, dec_batch, dec_seq, mod_all[l], lat_row, pws[l], l, None, cache,
                                   consts, next_norm)
    _, dk, dv, nk, nv, st = ctx_out
    return (y_p.reshape(batch, seq, d), y_s.reshape(dec_batch, dec_seq, d),
            dk.reshape(batch, depth, seq, DIFF_HEADS, 2, DIFF_HEAD_DIM),
            dv.reshape(batch, depth, seq, DIFF_HEADS, 2 * DIFF_HEAD_DIM),
            nk.reshape(batch, depth, seq, NA_HEADS, NA_HEAD_DIM),
            nv.reshape(batch, depth, seq, NA_HEADS, NA_HEAD_DIM),
            st.reshape(batch, depth, 2, SSD_HEADS, SSD_HEAD_DIM, SSD_STATE))
```
